```python
import jax, jax.numpy as jnp
from jax import lax
import numpy as np

D_MODEL = 2048
BATCH = 8
SEQ = 4096
DEPTH = 2
DEC_BATCH = 8
DEC_SEQ = 2048
PAST_LEN = 128

HEAD_DIM = 128
GRID_W = 64
ROPE_THETA = 10000.0
Q_BLOCK = 128
A_CONFIGS = ((128, 1), (512, 4), (2048, 16))
A_HEADS_PER_GROUP = 6
A_HEADS = A_HEADS_PER_GROUP * len(A_CONFIGS)
A_OUT = A_HEADS_PER_GROUP * HEAD_DIM
B_Q_HEADS = 8
B_KV_HEADS = 2
B_OUT = B_Q_HEADS * HEAD_DIM
C_HEADS = 4
C_OUT = C_HEADS * 2 * HEAD_DIM
N_BRANCH = 3
A_COLS = 3 * A_HEADS * HEAD_DIM
B_COLS = (B_Q_HEADS + 2 * B_KV_HEADS) * HEAD_DIM
C_COLS = 3 * C_HEADS * 2 * HEAD_DIM
QKV_COLS = A_COLS + B_COLS + C_COLS
N_EXPERTS = 16
N_EXPERT_GROUPS = 4
EXPERTS_PER_GROUP = N_EXPERTS // N_EXPERT_GROUPS
TOP_K = 2
D_FF_EXPERT = 1408
MOE_BLOCK = 128
DEEPNORM_ALPHA = (2 * DEPTH) ** 0.25
DEEPNORM_BETA = (8 * DEPTH) ** -0.25
LN_EPS = 1e-5
RMS_EPS = 1e-6
NEG_INF = -1e30

kernel_name = 'hybrid_dilated_gqa_diff_moe_encoder'


def layer_norm(x, g, b):
    xf = x.astype(jnp.float32)
    mu = jnp.mean(xf, -1, keepdims=True)
    var = jnp.mean(jnp.square(xf - mu), -1, keepdims=True)
    return ((xf - mu) * lax.rsqrt(var + LN_EPS) * g.astype(jnp.float32) + b.astype(jnp.float32)).astype(x.dtype)


def rms_norm(x, g):
    xf = x.astype(jnp.float32)
    return (xf * lax.rsqrt(jnp.mean(jnp.square(xf), -1, keepdims=True) + RMS_EPS) * g.astype(jnp.float32)).astype(x.dtype)


def rope_tables(pos, dim):
    inv = ROPE_THETA ** (-jnp.arange(0, dim, 2, dtype=jnp.float32) / dim)
    ang = pos.astype(jnp.float32)[:, None] * inv[None, :]
    return jnp.cos(ang), jnp.sin(ang)


def apply_rope(x, cos, sin):
    shape = (1, cos.shape[0]) + (1,) * (x.ndim - 3) + (cos.shape[1],)
    c = cos.reshape(shape)
    s = sin.reshape(shape)
    xf = x.astype(jnp.float32)
    x1, x2 = jnp.split(xf, 2, axis=-1)
    return jnp.concatenate([x1 * c - x2 * s, x2 * c + x1 * s], axis=-1).astype(x.dtype)


def apply_axial_rope(x, cos_r, sin_r, cos_c, sin_c):
    xr, xc = jnp.split(x, 2, axis=-1)
    return jnp.concatenate([apply_rope(xr, cos_r, sin_r), apply_rope(xc, cos_c, sin_c)], axis=-1)


def dilated_window_attention(q, k, v, dilation, radius):
    B, S, H, d = q.shape
    L = S // dilation

    def to_sub(t):
        return t.reshape(B, L, dilation, H, d).transpose(0, 2, 1, 3, 4).reshape(B * dilation, L, H, d)

    qs, ks, vs = to_sub(q), to_sub(k), to_sub(v)
    nq = -(-L // Q_BLOCK)
    Lq = nq * Q_BLOCK
    band = Q_BLOCK + 2 * radius
    qs = jnp.pad(qs, ((0, 0), (0, Lq - L), (0, 0), (0, 0)))
    kpad = ((0, 0), (radius, Lq - L + radius), (0, 0), (0, 0))
    ks = jnp.pad(ks, kpad)
    vs = jnp.pad(vs, kpad)
    idx = jnp.arange(nq)[:, None] * Q_BLOCK + jnp.arange(band)[None, :]
    kb = ks[:, idx]
    vb = vs[:, idx]
    qb = qs.reshape(B * dilation, nq, Q_BLOCK, H, d)
    s = jnp.einsum('bnqhd,bnkhd->bnhqk', qb, kb).astype(jnp.float32) * (d ** -0.5)
    key_pos = idx - radius
    q_pos = jnp.arange(nq)[:, None] * Q_BLOCK + jnp.arange(Q_BLOCK)[None, :]
    dist = key_pos[:, None, :] - q_pos[:, :, None]
    valid = (jnp.abs(dist) <= radius) & (key_pos[:, None, :] >= 0) & (key_pos[:, None, :] < L)
    s = jnp.where(valid[None, :, None], s, NEG_INF)
    m = jnp.max(s, -1, keepdims=True)
    p = jnp.exp(s - m)
    den = jnp.sum(p, -1, keepdims=True)
    o = jnp.einsum('bnhqk,bnkhd->bnqhd', (p / den).astype(v.dtype), vb)
    lse = (m + jnp.log(den))[..., 0]
    o = o.reshape(B, dilation, Lq, H, d)[:, :, :L]
    lse = lse.transpose(0, 1, 3, 2).reshape(B, dilation, Lq, H)[:, :, :L]

    def from_sub(t):
        return t.transpose((0, 2, 1) + tuple(range(3, t.ndim))).reshape((B, S) + t.shape[3:])

    return from_sub(o), from_sub(lse)


def gqa_attention(q, k, v):
    B, S, Hq, d = q.shape
    Hkv = k.shape[2]
    G = Hq // Hkv
    nq = S // Q_BLOCK
    qb = q.reshape(B, nq, Q_BLOCK, Hkv, G, d).transpose(1, 0, 2, 3, 4, 5)

    def block(qi):
        s = jnp.einsum('bqhgd,bkhd->bhgqk', qi, k).astype(jnp.float32) * (d ** -0.5)
        p = jax.nn.softmax(s, axis=-1)
        return jnp.einsum('bhgqk,bkhd->bqhgd', p.astype(v.dtype), v)

    o = lax.map(block, qb)
    return o.transpose(1, 0, 2, 3, 4, 5).reshape(B, S, Hq * d)


def diff_attention(q, k, v, lam):
    B, S, H, _, d = q.shape
    nq = S // Q_BLOCK
    qb = q.reshape(B, nq, Q_BLOCK, H, 2, d).transpose(1, 0, 2, 3, 4, 5)

    def block(qi):
        s = jnp.einsum('bqhcd,bkhcd->bhcqk', qi, k).astype(jnp.float32) * (d ** -0.5)
        p = jax.nn.softmax(s, axis=-1)
        a = p[:, :, 0] - lam * p[:, :, 1]
        return jnp.einsum('bhqk,bkhe->bqhe', a.astype(v.dtype), v)

    o = lax.map(block, qb)
    return o.transpose(1, 0, 2, 3, 4).reshape(B, S, H, 2 * d)


def mixer(h, layer_idx, w_in, q_norm_g, k_norm_g, lambda_q1, lambda_k1, lambda_q2, lambda_k2, subln_g,
          w_gate, b_gate, w_proj_a, w_proj_b, w_proj_c, w_out):
    B, S, D = h.shape
    rows = S // GRID_W
    pos = jnp.arange(S)
    pos_row = jnp.repeat(jnp.arange(rows), GRID_W)
    pos_col = jnp.tile(jnp.arange(GRID_W), rows)
    cos1, sin1 = rope_tables(pos, HEAD_DIM)
    cos_r, sin_r = rope_tables(pos_row, HEAD_DIM // 2)
    cos_c, sin_c = rope_tables(pos_col, HEAD_DIM // 2)

    qkv = h @ w_in
    a_part, b_part, c_part = jnp.split(qkv, [A_COLS, A_COLS + B_COLS], axis=-1)

    qa, ka, va = [t.reshape(B, S, A_HEADS, HEAD_DIM) for t in jnp.split(a_part, 3, axis=-1)]
    qa = apply_rope(qa, cos1, sin1)
    ka = apply_rope(ka, cos1, sin1)
    outs, lses = [], []
    for g, (window, dilation) in enumerate(A_CONFIGS):
        hs = slice(g * A_HEADS_PER_GROUP, (g + 1) * A_HEADS_PER_GROUP)
        o_g, lse_g = dilated_window_attention(qa[:, :, hs], ka[:, :, hs], va[:, :, hs], dilation, window // (2 * dilation))
        outs.append(o_g)
        lses.append(lse_g)
    wts = jax.nn.softmax(jnp.stack(lses), axis=0)
    o_a = jnp.einsum('gbshd,gbsh->bshd', jnp.stack(outs), wts.astype(h.dtype)).reshape(B, S, A_OUT)

    qb_ = b_part[..., :B_OUT].reshape(B, S, B_Q_HEADS, HEAD_DIM)
    kb_ = b_part[..., B_OUT:B_OUT + B_KV_HEADS * HEAD_DIM].reshape(B, S, B_KV_HEADS, HEAD_DIM)
    vb_ = b_part[..., B_OUT + B_KV_HEADS * HEAD_DIM:].reshape(B, S, B_KV_HEADS, HEAD_DIM)
    qb_ = apply_axial_rope(rms_norm(qb_, q_norm_g), cos_r, sin_r, cos_c, sin_c)
    kb_ = apply_axial_rope(rms_norm(kb_, k_norm_g), cos_r, sin_r, cos_c, sin_c)
    o_b = gqa_attention(qb_, kb_, vb_)

    qc, kc, vc = jnp.split(c_part, 3, axis=-1)
    qc = apply_rope(qc.reshape(B, S, C_HEADS, 2, HEAD_DIM), cos1, sin1)
    kc = apply_rope(kc.reshape(B, S, C_HEADS, 2, HEAD_DIM), cos1, sin1)
    vc = vc.reshape(B, S, C_HEADS, 2 * HEAD_DIM)
    lambda_init = 0.8 - 0.6 * float(np.exp(-0.3 * layer_idx))
    f32 = jnp.float32
    lam = (jnp.exp(jnp.sum(lambda_q1.astype(f32) * lambda_k1.astype(f32)))
           - jnp.exp(jnp.sum(lambda_q2.astype(f32) * lambda_k2.astype(f32))) + lambda_init)
    o_c = diff_attention(qc, kc, vc, lam)
    o_c = (rms_norm(o_c, subln_g) * (1.0 - lambda_init)).reshape(B, S, C_OUT)

    gates = jax.nn.sigmoid(h @ w_gate + b_gate).reshape(B, S, N_BRANCH, D)
    merged = (gates[:, :, 0] * (o_a @ w_proj_a) + gates[:, :, 1] * (o_b @ w_proj_b)
              + gates[:, :, 2] * (o_c @ w_proj_c))
    return merged @ w_out


def expert_dispatch(xt, experts, gates, w1, w3, w2):
    T, D = xt.shape
    A = T * TOP_K
    e_flat = experts.reshape(A)
    g_flat = gates.reshape(A)
    tok_flat = jnp.arange(A, dtype=jnp.int32) // TOP_K
    order = jnp.argsort(e_flat)
    e_s = e_flat[order]
    tok_s = tok_flat[order]
    g_s = g_flat[order]
    counts = jnp.zeros((N_EXPERTS,), jnp.int32).at[e_flat].add(1)
    padded = (counts + MOE_BLOCK - 1) // MOE_BLOCK * MOE_BLOCK
    pad_end = jnp.cumsum(padded)
    pad_start = pad_end - padded
    cnt_start = jnp.cumsum(counts) - counts
    dest = pad_start[e_s] + (jnp.arange(A, dtype=jnp.int32) - cnt_start[e_s])
    P = A + N_EXPERTS * MOE_BLOCK
    n_blk = P // MOE_BLOCK
    row_tok = jnp.full((P,), T, jnp.int32).at[dest].set(tok_s)
    x_ext = jnp.concatenate([xt, jnp.zeros((1, D), xt.dtype)], axis=0)
    rows = x_ext[row_tok].reshape(n_blk, MOE_BLOCK, D)
    blk_e = jnp.minimum(jnp.searchsorted(pad_end, jnp.arange(n_blk, dtype=jnp.int32) * MOE_BLOCK, side='right'),
                        N_EXPERTS - 1).astype(jnp.int32)

    def expert_block(args):
        xb, e = args
        hb = jax.nn.silu(xb @ w1[e]) * (xb @ w3[e])
        return hb @ w2[e]

    y = lax.map(expert_block, (rows, blk_e)).reshape(P, D)
    y_assign = y[dest] * g_s[:, None].astype(y.dtype)
    return jax.ops.segment_sum(y_assign, tok_s, num_segments=T)


def moe_ffn(h, w_router, router_bias, w1, w3, w2):
    B, S, D = h.shape
    T = B * S
    xt = h.reshape(T, D)
    scores = jax.nn.sigmoid((xt @ w_router).astype(jnp.float32))
    biased = scores + router_bias.astype(jnp.float32)
    grouped = biased.reshape(T, N_EXPERT_GROUPS, EXPERTS_PER_GROUP)
    group_score = jnp.sum(lax.top_k(grouped, TOP_K)[0], axis=-1)
    g_sel = jnp.argmax(group_score, axis=-1).astype(jnp.int32)
    in_group = jnp.take_along_axis(grouped, g_sel[:, None, None], axis=1)[:, 0]
    _, local = lax.top_k(in_group, TOP_K)
    experts = g_sel[:, None] * EXPERTS_PER_GROUP + local.astype(jnp.int32)
    sel = jnp.take_along_axis(scores, experts, axis=1)
    gates = sel / jnp.sum(sel, axis=-1, keepdims=True)
    return expert_dispatch(xt, experts, gates, w1, w3, w2).reshape(B, S, D)


def setup_inputs(seed: int = 0) -> dict:
    key = jax.random.key(seed)
    ks = jax.random.split(key, 25)
    D = D_MODEL

    def nrm(k, shape, scale):
        return jax.random.normal(k, shape, jnp.float32) * scale

    return {
        'x_prompt': nrm(ks[0], (BATCH, SEQ, D), 1.0),
        'x_sample': nrm(ks[1], (DEC_BATCH, DEC_SEQ, D), 1.0),
        'w_in': nrm(ks[2], (DEPTH, D, QKV_COLS), D ** -0.5),
        'q_norm_g': 1.0 + nrm(ks[3], (DEPTH, HEAD_DIM), 0.02),
        'k_norm_g': 1.0 + nrm(ks[4], (DEPTH, HEAD_DIM), 0.02),
        'lambda_q1': nrm(ks[5], (DEPTH, HEAD_DIM), 0.1),
        'lambda_k1': nrm(ks[6], (DEPTH, HEAD_DIM), 0.1),
        'lambda_q2': nrm(ks[7], (DEPTH, HEAD_DIM), 0.1),
        'lambda_k2': nrm(ks[8], (DEPTH, HEAD_DIM), 0.1),
        'subln_g': 1.0 + nrm(ks[9], (DEPTH, 2 * HEAD_DIM), 0.02),
        'w_gate': nrm(ks[10], (DEPTH, D, N_BRANCH * D), D ** -0.5),
        'b_gate': nrm(ks[11], (DEPTH, N_BRANCH * D), 0.02),
        'w_proj_a': nrm(ks[12], (DEPTH, A_OUT, D), A_OUT ** -0.5),
        'w_proj_b': nrm(ks[13], (DEPTH, B_OUT, D), B_OUT ** -0.5),
        'w_proj_c': nrm(ks[14], (DEPTH, C_OUT, D), C_OUT ** -0.5),
        'w_out': nrm(ks[15], (DEPTH, D, D), DEEPNORM_BETA * D ** -0.5),
        'ln1_g': 1.0 + nrm(ks[16], (DEPTH, D), 0.02),
        'ln1_b': nrm(ks[17], (DEPTH, D), 0.02),
        'w_router': nrm(ks[18], (D, N_EXPERTS), D ** -0.5),
        'router_bias': nrm(ks[19], (N_EXPERTS,), 0.01),
        'w1': nrm(ks[20], (DEPTH, N_EXPERTS, D, D_FF_EXPERT), D ** -0.5),
        'w3': nrm(ks[21], (DEPTH, N_EXPERTS, D, D_FF_EXPERT), D ** -0.5),
        'w2': nrm(ks[22], (DEPTH, N_EXPERTS, D_FF_EXPERT, D), DEEPNORM_BETA * D_FF_EXPERT ** -0.5),
        'ln2_g': 1.0 + nrm(ks[23], (DEPTH, D), 0.02),
        'ln2_b': nrm(ks[24], (DEPTH, D), 0.02),
    }


def reference(x_prompt, x_sample, w_in, q_norm_g, k_norm_g, lambda_q1, lambda_k1, lambda_q2, lambda_k2, subln_g,
              w_gate, b_gate, w_proj_a, w_proj_b, w_proj_c, w_out, ln1_g, ln1_b, w_router, router_bias,
              w1, w3, w2, ln2_g, ln2_b):
    def trunk(x):
        for l in range(DEPTH):
            mix = mixer(x, l, w_in[l], q_norm_g[l], k_norm_g[l], lambda_q1[l], lambda_k1[l], lambda_q2[l],
                        lambda_k2[l], subln_g[l], w_gate[l], b_gate[l], w_proj_a[l], w_proj_b[l], w_proj_c[l], w_out[l])
            x = layer_norm(DEEPNORM_ALPHA * x + mix, ln1_g[l], ln1_b[l])
            ffn = moe_ffn(x, w_router, router_bias, w1[l], w3[l], w2[l])
            x = layer_norm(DEEPNORM_ALPHA * x + ffn, ln2_g[l], ln2_b[l])
        return x

    y_prompt = trunk(x_prompt)
    y_sample = trunk(x_sample)
    return (y_prompt, y_sample)
```

```python
import functools

import numpy as np
import jax
import jax.numpy as jnp
from jax import lax
from jax.experimental import pallas as pl
from jax.experimental.pallas import tpu as pltpu

F32 = jnp.float32
BF16 = jnp.bfloat16

DEPTH = 2
HEAD_DIM = 128
GRID_W = 64
ROPE_THETA = 10000.0
A_CONFIGS = ((128, 1), (512, 4), (2048, 16))
A_HEADS_PER_GROUP = 6
A_HEADS = A_HEADS_PER_GROUP * len(A_CONFIGS)
A_GROUP_COLS = A_HEADS_PER_GROUP * HEAD_DIM
A_OUT = A_GROUP_COLS
B_Q_HEADS = 8
B_KV_HEADS = 2
B_GROUP = B_Q_HEADS // B_KV_HEADS
B_OUT = B_Q_HEADS * HEAD_DIM
C_HEADS = 4
C_OUT = C_HEADS * 2 * HEAD_DIM
N_BRANCH = 3
A_COLS = 3 * A_HEADS * HEAD_DIM
B_COLS = (B_Q_HEADS + 2 * B_KV_HEADS) * HEAD_DIM
C_COLS = 3 * C_HEADS * 2 * HEAD_DIM
QKV_COLS = A_COLS + B_COLS + C_COLS
N_EXPERTS = 16
N_EXPERT_GROUPS = 4
EXPERTS_PER_GROUP = N_EXPERTS // N_EXPERT_GROUPS
TOP_K = 2
DEEPNORM_ALPHA = (2 * DEPTH) ** 0.25
LN_EPS = 1e-5
RMS_EPS = 1e-6
NEG_INF = -1e30
ATTN_SCALE = HEAD_DIM ** -0.5
A_RADIUS = 64

A_Q0, A_K0, A_V0 = 0, A_HEADS, 2 * A_HEADS
B_Q0 = 3 * A_HEADS
B_K0 = B_Q0 + B_Q_HEADS
B_V0 = B_K0 + B_KV_HEADS
C_Q0 = B_V0 + B_KV_HEADS
C_K0 = C_Q0 + 2 * C_HEADS
C_V0 = C_K0 + 2 * C_HEADS
N_HEAD_COLS = C_V0 + 2 * C_HEADS

LANE = 128
QKV_TN = 2 * LANE
QKV_TM = 1024
GATE_TM, GATE_TN = 1024, 1024
PROJ_TM, PROJ_TN = 1024, 512
OUT_TM = 512
ROUTER_TM = 512
MOE_BLOCK = 256
COMBINE_TM = 256
DISPATCH_TM = 512
ATTN_TQ = 256
MERGE_TM = 1024
VMEM_LIMIT = 56 * 1024 * 1024


def _params(sem, vmem=VMEM_LIMIT):
    return pltpu.CompilerParams(dimension_semantics=sem, vmem_limit_bytes=vmem)


def _tile(n, t):
    t = min(n, t)
    assert n % t == 0, (n, t)
    return t


def _qkv_kernel(x_ref, w_ref, c1_ref, s1_ref, cax_ref, sa_ref, sb_ref, g_ref, o_ref):
    j = pl.program_id(1)
    hpb = QKV_TN // LANE
    acc = jnp.dot(x_ref[...], w_ref[...], preferred_element_type=F32)

    def blk(h):
        return h // hpb

    is_q = (j < blk(A_K0)) | ((j >= blk(B_Q0)) & (j < blk(B_K0))) | ((j >= blk(C_Q0)) & (j < blk(C_K0)))
    scale = jnp.where(is_q, ATTN_SCALE, 1.0).astype(F32)
    rope = (j < blk(A_V0)) | ((j >= blk(C_Q0)) & (j < blk(C_V0)))
    axial = (j >= blk(B_Q0)) & (j < blk(B_V0))
    plain = jnp.logical_not(rope | axial)

    @pl.when(plain)
    def _():
        o_ref[...] = acc.astype(o_ref.dtype)

    @pl.when(rope)
    def _():
        for c in range(hpb):
            xh = acc[:, c * LANE:(c + 1) * LANE]
            y = xh * c1_ref[...] + pltpu.roll(xh, 64, 1) * s1_ref[...]
            o_ref[:, c * LANE:(c + 1) * LANE] = (y * scale).astype(o_ref.dtype)

    @pl.when(axial)
    def _():
        g = jnp.where(j < blk(B_K0), g_ref[0:1, :], g_ref[1:2, :])
        for c in range(hpb):
            xh = acc[:, c * LANE:(c + 1) * LANE]
            r = xh * lax.rsqrt(jnp.mean(xh * xh, axis=-1, keepdims=True) + RMS_EPS) * g
            y = r * cax_ref[...] + pltpu.roll(r, 96, 1) * sa_ref[...] + pltpu.roll(r, 32, 1) * sb_ref[...]
            o_ref[:, c * LANE:(c + 1) * LANE] = (y * scale).astype(o_ref.dtype)


def _qkv_proj(xb, w, tables, gains, seq):
    t, d = xb.shape
    tm = _tile(seq, QKV_TM)
    nsb = seq // tm
    tab_spec = pl.BlockSpec((tm, LANE), lambda i, j: (i % nsb, 0))
    return pl.pallas_call(
        _qkv_kernel,
        out_shape=jax.ShapeDtypeStruct((t, QKV_COLS), BF16),
        grid=(t // tm, QKV_COLS // QKV_TN),
        in_specs=[pl.BlockSpec((tm, d), lambda i, j: (i, 0)),
                  pl.BlockSpec((d, QKV_TN), lambda i, j: (0, j)),
                  tab_spec, tab_spec, tab_spec, tab_spec, tab_spec,
                  pl.BlockSpec((2, LANE), lambda i, j: (0, 0))],
        out_specs=pl.BlockSpec((tm, QKV_TN), lambda i, j: (i, j)),
        compiler_params=_params(("parallel", "arbitrary")),
        name="qkv_proj",
    )(xb, w, *tables, gains)


def _gate_kernel(x_ref, w_ref, b_ref, o_ref):
    acc = jnp.dot(x_ref[...], w_ref[...], preferred_element_type=F32) + b_ref[...]
    o_ref[...] = jax.nn.sigmoid(acc).astype(o_ref.dtype)


def _gate_proj(xb, w, b):
    t, d = xb.shape
    n = w.shape[1]
    tm, tn = _tile(t, GATE_TM), _tile(n, GATE_TN)
    return pl.pallas_call(
        _gate_kernel,
        out_shape=jax.ShapeDtypeStruct((t, n), BF16),
        grid=(t // tm, n // tn),
        in_specs=[pl.BlockSpec((tm, d), lambda i, j: (i, 0)),
                  pl.BlockSpec((d, tn), lambda i, j: (0, j)),
                  pl.BlockSpec((1, tn), lambda i, j: (0, j))],
        out_specs=pl.BlockSpec((tm, tn), lambda i, j: (i, j)),
        compiler_params=_params(("parallel", "arbitrary")),
        name="gate_proj",
    )(xb, w, b)


def _attn_a_kernel(q_ref, k_ref, v_ref, o_ref, lse_ref, *, tq, win, sub_len):
    q0 = pl.program_id(2) * tq
    start = jnp.clip(q0 - A_RADIUS, 0, sub_len - win)
    if win % 16 == 0 and tq % 64 == 0:
        start = pl.multiple_of(start, 16)
    q_pos = q0 + lax.broadcasted_iota(jnp.int32, (tq, win), 0)
    k_pos = start + lax.broadcasted_iota(jnp.int32, (tq, win), 1)
    valid = jnp.abs(k_pos - q_pos) <= A_RADIUS
    lane = lax.broadcasted_iota(jnp.int32, (tq, LANE), 1)
    lse_all = jnp.zeros((tq, LANE), F32)
    for h in range(A_HEADS_PER_GROUP):
        cols = slice(h * LANE, (h + 1) * LANE)
        q = q_ref[0, :, cols]
        k = k_ref[0, pl.ds(start, win), cols]
        v = v_ref[0, pl.ds(start, win), cols]
        s = lax.dot_general(q, k, (((1,), (1,)), ((), ())), preferred_element_type=F32)
        s = jnp.where(valid, s, NEG_INF)
        m = jnp.max(s, axis=-1, keepdims=True)
        p = jnp.exp(s - m)
        den = jnp.sum(p, axis=-1, keepdims=True)
        o = jnp.dot(p.astype(BF16), v, preferred_element_type=F32) / den
        o_ref[0, :, cols] = o.astype(o_ref.dtype)
        lse_all = jnp.where(lane == h, m + jnp.log(den), lse_all)
    lse_ref[0] = lse_all


def _attn_a(qkv3, group, dilation):
    b, s, _ = qkv3.shape
    sub_len = s // dilation
    tq = _tile(sub_len, ATTN_TQ)
    win = min(tq + 2 * A_RADIUS, sub_len)
    view = qkv3.reshape(b, sub_len, dilation * QKV_COLS)
    cpr = QKV_COLS // A_GROUP_COLS
    qb, kb, vb = group, A_HEADS // A_HEADS_PER_GROUP + group, 2 * (A_HEADS // A_HEADS_PER_GROUP) + group
    o, lse = pl.pallas_call(
        functools.partial(_attn_a_kernel, tq=tq, win=win, sub_len=sub_len),
        out_shape=(jax.ShapeDtypeStruct((b, sub_len, dilation * A_GROUP_COLS), BF16),
                   jax.ShapeDtypeStruct((b, sub_len, dilation * LANE), F32)),
        grid=(b, dilation, sub_len // tq),
        in_specs=[pl.BlockSpec((1, tq, A_GROUP_COLS), lambda bi, c, qi: (bi, qi, c * cpr + qb)),
                  pl.BlockSpec((1, sub_len, A_GROUP_COLS), lambda bi, c, qi: (bi, 0, c * cpr + kb)),
                  pl.BlockSpec((1, sub_len, A_GROUP_COLS), lambda bi, c, qi: (bi, 0, c * cpr + vb))],
        out_specs=(pl.BlockSpec((1, tq, A_GROUP_COLS), lambda bi, c, qi: (bi, qi, c)),
                   pl.BlockSpec((1, tq, LANE), lambda bi, c, qi: (bi, qi, c))),
        compiler_params=_params(("parallel", "parallel", "arbitrary")),
        name=f"attn_a{group}",
    )(view, view, view)
    return o.reshape(b * s, A_GROUP_COLS), lse.reshape(b * s, LANE)


def _merge_a_kernel(o0_ref, o1_ref, o2_ref, l0_ref, l1_ref, l2_ref, o_ref):
    l0, l1, l2 = l0_ref[...], l1_ref[...], l2_ref[...]
    m = jnp.maximum(jnp.maximum(l0, l1), l2)
    e0, e1, e2 = jnp.exp(l0 - m), jnp.exp(l1 - m), jnp.exp(l2 - m)
    den = e0 + e1 + e2
    w0, w1, w2 = e0 / den, e1 / den, e2 / den
    for h in range(A_HEADS_PER_GROUP):
        cols = slice(h * LANE, (h + 1) * LANE)
        acc = (o0_ref[:, cols].astype(F32) * w0[:, h:h + 1]
               + o1_ref[:, cols].astype(F32) * w1[:, h:h + 1]
               + o2_ref[:, cols].astype(F32) * w2[:, h:h + 1])
        o_ref[:, cols] = acc.astype(o_ref.dtype)


def _merge_a(outs, lses):
    t = outs[0].shape[0]
    tm = _tile(t, MERGE_TM)
    ospec = pl.BlockSpec((tm, A_GROUP_COLS), lambda i: (i, 0))
    lspec = pl.BlockSpec((tm, LANE), lambda i: (i, 0))
    return pl.pallas_call(
        _merge_a_kernel,
        out_shape=jax.ShapeDtypeStruct((t, A_OUT), BF16),
        grid=(t // tm,),
        in_specs=[ospec, ospec, ospec, lspec, lspec, lspec],
        out_specs=ospec,
        compiler_params=_params(("parallel",)),
        name="merge_a",
    )(*outs, *lses)


def _softmax_pv(q, k, v):
    s = lax.dot_general(q, k, (((1,), (1,)), ((), ())), preferred_element_type=F32)
    m = jnp.max(s, axis=-1, keepdims=True)
    p = jnp.exp(s - m)
    den = jnp.sum(p, axis=-1, keepdims=True)
    return jnp.dot(p.astype(BF16), v, preferred_element_type=F32) / den


def _attn_b_kernel(*refs):
    q_refs, (k_ref, v_ref, o_ref) = refs[:B_GROUP], refs[B_GROUP:]
    k = k_ref[0]
    v = v_ref[0]
    for g in range(B_GROUP):
        o = _softmax_pv(q_refs[g][0], k, v)
        o_ref[0, :, g * LANE:(g + 1) * LANE] = o.astype(o_ref.dtype)


def _attn_b(qkv3):
    b, s, _ = qkv3.shape
    tq = _tile(s, ATTN_TQ)
    q_specs = [pl.BlockSpec((1, tq, LANE), functools.partial(lambda bi, h, qi, g: (bi, qi, B_Q0 + h * B_GROUP + g), g=g))
               for g in range(B_GROUP)]
    o = pl.pallas_call(
        _attn_b_kernel,
        out_shape=jax.ShapeDtypeStruct((b, s, B_OUT), BF16),
        grid=(b, B_KV_HEADS, s // tq),
        in_specs=q_specs + [pl.BlockSpec((1, s, LANE), lambda bi, h, qi: (bi, 0, B_K0 + h)),
                            pl.BlockSpec((1, s, LANE), lambda bi, h, qi: (bi, 0, B_V0 + h))],
        out_specs=pl.BlockSpec((1, tq, B_GROUP * LANE), lambda bi, h, qi: (bi, qi, h)),
        compiler_params=_params(("parallel", "parallel", "arbitrary")),
        name="attn_b",
    )(*([qkv3] * (B_GROUP + 2)))
    return o.reshape(b * s, B_OUT)


def _attn_c_kernel(q1_ref, q2_ref, k1_ref, k2_ref, v_ref, lam_ref, g_ref, o_ref, *, lambda_init):
    lam_p = lam_ref[...].astype(F32)
    lam = (jnp.exp(jnp.sum(lam_p[0:1] * lam_p[1:2], axis=-1, keepdims=True))
           - jnp.exp(jnp.sum(lam_p[2:3] * lam_p[3:4], axis=-1, keepdims=True)) + lambda_init)
    v = v_ref[0]
    o = _softmax_pv(q1_ref[0], k1_ref[0], v) - lam * _softmax_pv(q2_ref[0], k2_ref[0], v)
    o = o * lax.rsqrt(jnp.mean(o * o, axis=-1, keepdims=True) + RMS_EPS) * g_ref[...].astype(F32)
    o_ref[0] = (o * (1.0 - lambda_init)).astype(o_ref.dtype)


def _attn_c(qkv3, lam_rows, subln_g, lambda_init):
    b, s, _ = qkv3.shape
    tq = _tile(s, ATTN_TQ)
    o = pl.pallas_call(
        functools.partial(_attn_c_kernel, lambda_init=lambda_init),
        out_shape=jax.ShapeDtypeStruct((b, s, C_OUT), BF16),
        grid=(b, C_HEADS, s // tq),
        in_specs=[pl.BlockSpec((1, tq, LANE), lambda bi, h, qi: (bi, qi, C_Q0 + 2 * h)),
                  pl.BlockSpec((1, tq, LANE), lambda bi, h, qi: (bi, qi, C_Q0 + 2 * h + 1)),
                  pl.BlockSpec((1, s, LANE), lambda bi, h, qi: (bi, 0, C_K0 + 2 * h)),
                  pl.BlockSpec((1, s, LANE), lambda bi, h, qi: (bi, 0, C_K0 + 2 * h + 1)),
                  pl.BlockSpec((1, s, 2 * LANE), lambda bi, h, qi: (bi, 0, C_V0 // 2 + h)),
                  pl.BlockSpec((4, LANE), lambda bi, h, qi: (0, 0)),
                  pl.BlockSpec((1, 2 * LANE), lambda bi, h, qi: (0, 0))],
        out_specs=pl.BlockSpec((1, tq, 2 * LANE), lambda bi, h, qi: (bi, qi, h)),
        compiler_params=_params(("parallel", "parallel", "arbitrary")),
        name="attn_c",
    )(qkv3, qkv3, qkv3, qkv3, qkv3, lam_rows, subln_g)
    return o.reshape(b * s, C_OUT)


def _branch_proj_kernel(oa_ref, ob_ref, oc_ref, g0_ref, g1_ref, g2_ref, wa_ref, wb_ref, wc_ref, o_ref):
    acc = g0_ref[...].astype(F32) * jnp.dot(oa_ref[...], wa_ref[...], preferred_element_type=F32)
    acc = acc + g1_ref[...].astype(F32) * jnp.dot(ob_ref[...], wb_ref[...], preferred_element_type=F32)
    acc = acc + g2_ref[...].astype(F32) * jnp.dot(oc_ref[...], wc_ref[...], preferred_element_type=F32)
    o_ref[...] = acc.astype(o_ref.dtype)


def _branch_proj(oa, ob, oc, gates, wa, wb, wc):
    t = oa.shape[0]
    d = wa.shape[1]
    tm, tn = _tile(t, PROJ_TM), _tile(d, PROJ_TN)
    nb = d // tn
    gspecs = [pl.BlockSpec((tm, tn), functools.partial(lambda i, j, br: (i, br * nb + j), br=br)) for br in range(N_BRANCH)]
    return pl.pallas_call(
        _branch_proj_kernel,
        out_shape=jax.ShapeDtypeStruct((t, d), BF16),
        grid=(t // tm, nb),
        in_specs=[pl.BlockSpec((tm, A_OUT), lambda i, j: (i, 0)),
                  pl.BlockSpec((tm, B_OUT), lambda i, j: (i, 0)),
                  pl.BlockSpec((tm, C_OUT), lambda i, j: (i, 0))] + gspecs +
                 [pl.BlockSpec((A_OUT, tn), lambda i, j: (0, j)),
                  pl.BlockSpec((B_OUT, tn), lambda i, j: (0, j)),
                  pl.BlockSpec((C_OUT, tn), lambda i, j: (0, j))],
        out_specs=pl.BlockSpec((tm, tn), lambda i, j: (i, j)),
        compiler_params=_params(("parallel", "arbitrary")),
        name="branch_proj",
    )(oa, ob, oc, gates, gates, gates, wa, wb, wc)


def _layer_norm(z, g, b):
    mu = jnp.mean(z, axis=-1, keepdims=True)
    zc = z - mu
    var = jnp.mean(zc * zc, axis=-1, keepdims=True)
    return zc * lax.rsqrt(var + LN_EPS) * g + b


def _out_ln_kernel(m_ref, w_ref, x_ref, g_ref, b_ref, o_ref):
    mix = jnp.dot(m_ref[...], w_ref[...], preferred_element_type=F32)
    o_ref[...] = _layer_norm(DEEPNORM_ALPHA * x_ref[...] + mix, g_ref[...], b_ref[...])


def _out_proj_ln(merged, w_out, x, g, b):
    t, d = x.shape
    tm = _tile(t, OUT_TM)
    row = pl.BlockSpec((tm, d), lambda i: (i, 0))
    vec = pl.BlockSpec((1, d), lambda i: (0, 0))
    return pl.pallas_call(
        _out_ln_kernel,
        out_shape=jax.ShapeDtypeStruct((t, d), F32),
        grid=(t // tm,),
        in_specs=[row, pl.BlockSpec((d, d), lambda i: (0, 0)), row, vec, vec],
        out_specs=row,
        compiler_params=_params(("parallel",)),
        name="out_proj_ln",
    )(merged, w_out, x, g, b)


def _router_kernel(x_ref, whi_ref, wlo_ref, bias_ref, info_ref, cnt_ref, carry_ref, *, tm):
    @pl.when(pl.program_id(0) == 0)
    def _():
        carry_ref[...] = jnp.zeros_like(carry_ref)

    x = x_ref[...]
    x_hi = x.astype(BF16)
    x_lo = (x - x_hi.astype(F32)).astype(BF16)
    nt = (((1,), (1,)), ((), ()))
    logits = (lax.dot_general(whi_ref[...], x_hi, nt, preferred_element_type=F32)
              + lax.dot_general(wlo_ref[...], x_hi, nt, preferred_element_type=F32)
              + lax.dot_general(whi_ref[...], x_lo, nt, preferred_element_type=F32))
    scores = jax.nn.sigmoid(logits)
    biased = scores + bias_ref[...]

    def row(a, e):
        return a[e:e + 1, :]

    gscore = []
    for g in range(N_EXPERT_GROUPS):
        a, b, c, d = (row(biased, g * EXPERTS_PER_GROUP + i) for i in range(EXPERTS_PER_GROUP))
        hi1, lo1, hi2, lo2 = jnp.maximum(a, b), jnp.minimum(a, b), jnp.maximum(c, d), jnp.minimum(c, d)
        gscore.append(jnp.maximum(hi1, hi2) + jnp.maximum(jnp.minimum(hi1, hi2), jnp.maximum(lo1, lo2)))
    gsel = jnp.zeros((1, tm), jnp.int32)
    best = gscore[0]
    for g in range(1, N_EXPERT_GROUPS):
        better = gscore[g] > best
        gsel = jnp.where(better, g, gsel)
        best = jnp.where(better, gscore[g], best)

    def pick(a, i):
        out = row(a, i)
        for g in range(1, N_EXPERT_GROUPS):
            out = jnp.where(gsel == g, row(a, g * EXPERTS_PER_GROUP + i), out)
        return out

    bv = [pick(biased, i) for i in range(EXPERTS_PER_GROUP)]
    sv = [pick(scores, i) for i in range(EXPERTS_PER_GROUP)]
    i0 = jnp.zeros((1, tm), jnp.int32)
    b0 = bv[0]
    for i in range(1, EXPERTS_PER_GROUP):
        better = bv[i] > b0
        i0 = jnp.where(better, i, i0)
        b0 = jnp.where(better, bv[i], b0)
    i1 = jnp.full((1, tm), -1, jnp.int32)
    b1 = jnp.full((1, tm), -jnp.inf, F32)
    for i in range(EXPERTS_PER_GROUP):
        better = (i0 != i) & ((i1 < 0) | (bv[i] > b1))
        i1 = jnp.where(better, i, i1)
        b1 = jnp.where(better, bv[i], b1)
    s0 = jnp.zeros((1, tm), F32)
    s1 = jnp.zeros((1, tm), F32)
    for i in range(EXPERTS_PER_GROUP):
        s0 = jnp.where(i0 == i, sv[i], s0)
        s1 = jnp.where(i1 == i, sv[i], s1)
    e0 = gsel * EXPERTS_PER_GROUP + i0
    e1 = gsel * EXPERTS_PER_GROUP + i1
    den = s0 + s1
    g0, g1 = s0 / den, s1 / den

    erow = lax.broadcasted_iota(jnp.int32, (N_EXPERTS, tm), 0)
    member = ((erow == e0) | (erow == e1))
    tri = (lax.broadcasted_iota(jnp.int32, (tm, tm), 0) < lax.broadcasted_iota(jnp.int32, (tm, tm), 1))
    prefix = jnp.dot(member.astype(BF16), tri.astype(BF16), preferred_element_type=F32) + carry_ref[:, 0:1]
    r0 = jnp.sum(jnp.where(erow == e0, prefix, 0.0), axis=0, keepdims=True)
    r1 = jnp.sum(jnp.where(erow == e1, prefix, 0.0), axis=0, keepdims=True)
    carry_ref[...] = carry_ref[...] + jnp.sum(member.astype(F32), axis=1, keepdims=True)
    cnt_ref[...] = carry_ref[...]

    zero = jnp.zeros((1, tm), F32)
    info_ref[...] = jnp.concatenate([e0.astype(F32), e1.astype(F32), r0, r1, g0, g1, zero, zero], axis=0)


def _router(x, w_hi_t, w_lo_t, bias_col):
    t, d = x.shape
    tm = _tile(t, ROUTER_TM)
    return pl.pallas_call(
        functools.partial(_router_kernel, tm=tm),
        out_shape=(jax.ShapeDtypeStruct((8, t), F32), jax.ShapeDtypeStruct((N_EXPERTS, LANE), F32)),
        grid=(t // tm,),
        in_specs=[pl.BlockSpec((tm, d), lambda i: (i, 0)),
                  pl.BlockSpec((N_EXPERTS, d), lambda i: (0, 0)),
                  pl.BlockSpec((N_EXPERTS, d), lambda i: (0, 0)),
                  pl.BlockSpec((N_EXPERTS, 1), lambda i: (0, 0))],
        out_specs=(pl.BlockSpec((8, tm), lambda i: (0, i)),
                   pl.BlockSpec((N_EXPERTS, LANE), lambda i: (0, 0))),
        scratch_shapes=[pltpu.VMEM((N_EXPERTS, LANE), F32)],
        compiler_params=_params(("arbitrary",)),
        name="router",
    )(x, w_hi_t, w_lo_t, bias_col)


def _dispatch_kernel(dest_ref, x_hbm, init_hbm, xs_hbm, sem, *, tm):
    del init_hbm
    i = pl.program_id(0)
    n = pl.num_programs(0)

    def row_copy(r, slot):
        return pltpu.make_async_copy(x_hbm.at[pl.ds(i * tm + r, 1), :],
                                     xs_hbm.at[pl.ds(dest_ref[slot, r], 1), :], sem)

    def issue(r, c):
        row_copy(r, 0).start()
        row_copy(r, 1).start()
        return c

    lax.fori_loop(0, tm, issue, 0)

    def drain(r, c):
        row_copy(0, 0).wait()
        row_copy(0, 0).wait()
        return c

    @pl.when(i > 0)
    def _():
        lax.fori_loop(0, tm, drain, 0)

    @pl.when(i == n - 1)
    def _():
        lax.fori_loop(0, tm, drain, 0)


def _dispatch(x, dest, n_rows):
    t, d = x.shape
    tm = _tile(t, DISPATCH_TM)
    init = jnp.zeros((n_rows, d), F32)
    return pl.pallas_call(
        functools.partial(_dispatch_kernel, tm=tm),
        out_shape=jax.ShapeDtypeStruct((n_rows, d), F32),
        grid=(t // tm,),
        in_specs=[pl.BlockSpec((TOP_K, tm), lambda i: (0, i), memory_space=pltpu.SMEM),
                  pl.BlockSpec(memory_space=pl.ANY),
                  pl.BlockSpec(memory_space=pl.ANY)],
        out_specs=pl.BlockSpec(memory_space=pl.ANY),
        scratch_shapes=[pltpu.SemaphoreType.DMA],
        input_output_aliases={2: 0},
        compiler_params=_params(("arbitrary",)),
        name="moe_dispatch",
    )(dest, x, init)


def _expert_kernel(blk_e_ref, nact_ref, x_ref, w1_ref, w3_ref, w2_ref, o_ref):
    del blk_e_ref

    @pl.when(pl.program_id(0) < nact_ref[0])
    def _():
        x = x_ref[...].astype(BF16)
        h1 = jnp.dot(x, w1_ref[0], preferred_element_type=F32)
        h3 = jnp.dot(x, w3_ref[0], preferred_element_type=F32)
        h = (h1 * jax.nn.sigmoid(h1) * h3).astype(BF16)
        o_ref[...] = jnp.dot(h, w2_ref[0], preferred_element_type=F32)

    @pl.when(pl.program_id(0) >= nact_ref[0])
    def _():
        o_ref[...] = jnp.zeros_like(o_ref)


def _experts(xs, blk_e, n_active, w1, w3, w2):
    p, d = xs.shape
    f = w1.shape[2]
    nb = p // MOE_BLOCK
    grid_spec = pltpu.PrefetchScalarGridSpec(
        num_scalar_prefetch=2,
        grid=(nb,),
        in_specs=[pl.BlockSpec((MOE_BLOCK, d), lambda i, be, na: (i, 0)),
                  pl.BlockSpec((1, d, f), lambda i, be, na: (be[i], 0, 0)),
                  pl.BlockSpec((1, d, f), lambda i, be, na: (be[i], 0, 0)),
                  pl.BlockSpec((1, f, d), lambda i, be, na: (be[i], 0, 0))],
        out_specs=pl.BlockSpec((MOE_BLOCK, d), lambda i, be, na: (i, 0)),
    )
    return pl.pallas_call(
        _expert_kernel,
        out_shape=jax.ShapeDtypeStruct((p, d), F32),
        grid_spec=grid_spec,
        compiler_params=_params(("arbitrary",)),
        name="moe_experts",
    )(blk_e, n_active, xs, w1, w3, w2)


def _combine_kernel(dest_ref, x_ref, gt_ref, g_ref, b_ref, ys_hbm, o_ref, ob_ref, buf, sem, *, tm):
    def row_copy(r, slot):
        return pltpu.make_async_copy(ys_hbm.at[pl.ds(dest_ref[slot, r], 1), :],
                                     buf.at[slot, pl.ds(r, 1), :], sem)

    def issue(r, c):
        row_copy(r, 0).start()
        row_copy(r, 1).start()
        return c

    lax.fori_loop(0, tm, issue, 0)

    def drain(r, c):
        row_copy(r, 0).wait()
        row_copy(r, 1).wait()
        return c

    lax.fori_loop(0, tm, drain, 0)
    gt = gt_ref[...]
    ffn = gt[:, 0:1] * buf[0] + gt[:, 1:2] * buf[1]
    y = _layer_norm(DEEPNORM_ALPHA * x_ref[...] + ffn, g_ref[...], b_ref[...])
    o_ref[...] = y
    ob_ref[...] = y.astype(ob_ref.dtype)


def _combine_ln(x, ys, dest, gates_t, g, b):
    t, d = x.shape
    tm = _tile(t, COMBINE_TM)
    row = pl.BlockSpec((tm, d), lambda i: (i, 0))
    vec = pl.BlockSpec((1, d), lambda i: (0, 0))
    return pl.pallas_call(
        functools.partial(_combine_kernel, tm=tm),
        out_shape=(jax.ShapeDtypeStruct((t, d), F32), jax.ShapeDtypeStruct((t, d), BF16)),
        grid=(t // tm,),
        in_specs=[pl.BlockSpec((TOP_K, tm), lambda i: (0, i), memory_space=pltpu.SMEM),
                  row,
                  pl.BlockSpec((tm, TOP_K), lambda i: (i, 0)),
                  vec, vec,
                  pl.BlockSpec(memory_space=pl.ANY)],
        out_specs=(row, row),
        scratch_shapes=[pltpu.VMEM((TOP_K, tm, d), F32), pltpu.SemaphoreType.DMA],
        compiler_params=_params(("arbitrary",)),
        name="moe_combine_ln",
    )(dest, x, gates_t, g, b, ys)


def _rope_tables(seq):
    def tab(pos, dim):
        inv = ROPE_THETA ** (-jnp.arange(0, dim, 2, dtype=F32) / dim)
        ang = pos.astype(F32)[:, None] * inv[None, :]
        return jnp.cos(ang), jnp.sin(ang)

    pos = jnp.arange(seq)
    cos1, sin1 = tab(pos, HEAD_DIM)
    cos_r, sin_r = tab(pos // GRID_W, HEAD_DIM // 2)
    cos_c, sin_c = tab(pos % GRID_W, HEAD_DIM // 2)
    z = jnp.zeros_like(sin_r)
    c1 = jnp.concatenate([cos1, cos1], axis=1)
    s1 = jnp.concatenate([-sin1, sin1], axis=1)
    cax = jnp.concatenate([cos_r, cos_r, cos_c, cos_c], axis=1)
    sa = jnp.concatenate([-sin_r, z, -sin_c, z], axis=1)
    sb = jnp.concatenate([z, sin_r, z, sin_c], axis=1)
    return c1, s1, cax, sa, sb


def _moe(x1, lw, shared):
    t, d = x1.shape
    info, cnt = _router(x1, shared["wr_hi"], shared["wr_lo"], shared["r_bias"])
    experts = info[0:2].astype(jnp.int32)
    ranks = info[2:4].astype(jnp.int32)
    gates_t = info[4:6].T
    counts = cnt[:, 0].astype(jnp.int32)
    padded = (counts + MOE_BLOCK - 1) // MOE_BLOCK * MOE_BLOCK
    pad_end = jnp.cumsum(padded)
    pad_start = pad_end - padded
    dest = pad_start[experts] + ranks
    n_rows = t * TOP_K + N_EXPERTS * MOE_BLOCK
    nb = n_rows // MOE_BLOCK
    blk_e = jnp.minimum(jnp.searchsorted(pad_end, jnp.arange(nb, dtype=jnp.int32) * MOE_BLOCK, side="right"),
                        N_EXPERTS - 1).astype(jnp.int32)
    n_active = (pad_end[-1:] // MOE_BLOCK).astype(jnp.int32)
    xs = _dispatch(x1, dest, n_rows)
    ys = _experts(xs, blk_e, n_active, lw["w1"], lw["w3"], lw["w2"])
    return _combine_ln(x1, ys, dest, gates_t, lw["ln2_g"], lw["ln2_b"])


def _trunk(x, layers, shared):
    b, s, d = x.shape
    t = b * s
    tables = _rope_tables(s)
    xf = x.reshape(t, d)
    xb = xf.astype(BF16)
    for l, lw in enumerate(layers):
        qkv = _qkv_proj(xb, lw["w_in"], tables, lw["qk_gains"], s)
        gates = _gate_proj(xb, lw["w_gate"], lw["b_gate"])
        qkv3 = qkv.reshape(b, s, QKV_COLS)
        a_parts = [_attn_a(qkv3, g, dil) for g, (_, dil) in enumerate(A_CONFIGS)]
        oa = _merge_a([p[0] for p in a_parts], [p[1] for p in a_parts])
        ob = _attn_b(qkv3)
        lambda_init = 0.8 - 0.6 * float(np.exp(-0.3 * l))
        oc = _attn_c(qkv3, lw["lam_rows"], lw["subln_g"], lambda_init)
        merged = _branch_proj(oa, ob, oc, gates, lw["w_proj_a"], lw["w_proj_b"], lw["w_proj_c"])
        x1 = _out_proj_ln(merged, lw["w_out"], xf, lw["ln1_g"], lw["ln1_b"])
        xf, xb = _moe(x1, lw, shared)
    return xf.reshape(b, s, d)


def kernel(x_prompt, x_sample, w_in, q_norm_g, k_norm_g, lambda_q1, lambda_k1, lambda_q2, lambda_k2, subln_g,
           w_gate, b_gate, w_proj_a, w_proj_b, w_proj_c, w_out, ln1_g, ln1_b, w_router, router_bias,
           w1, w3, w2, ln2_g, ln2_b):
    depth = w_in.shape[0]
    layers = []
    for l in range(depth):
        layers.append(dict(
            w_in=w_in[l].astype(BF16),
            qk_gains=jnp.stack([q_norm_g[l], k_norm_g[l]]).astype(F32),
            lam_rows=jnp.stack([lambda_q1[l], lambda_k1[l], lambda_q2[l], lambda_k2[l]]).astype(F32),
            subln_g=subln_g[l].reshape(1, -1).astype(F32),
            w_gate=w_gate[l].astype(BF16),
            b_gate=b_gate[l].reshape(1, -1).astype(F32),
            w_proj_a=w_proj_a[l].astype(BF16),
            w_proj_b=w_proj_b[l].astype(BF16),
            w_proj_c=w_proj_c[l].astype(BF16),
            w_out=w_out[l].astype(BF16),
            ln1_g=ln1_g[l].reshape(1, -1).astype(F32),
            ln1_b=ln1_b[l].reshape(1, -1).astype(F32),
            w1=w1[l].astype(BF16), w3=w3[l].astype(BF16), w2=w2[l].astype(BF16),
            ln2_g=ln2_g[l].reshape(1, -1).astype(F32),
            ln2_b=ln2_b[l].reshape(1, -1).astype(F32),
        ))
    wr_t = w_router.astype(F32).T
    wr_hi = wr_t.astype(BF16)
    wr_lo = (wr_t - wr_hi.astype(F32)).astype(BF16)
    shared = dict(wr_hi=wr_hi, wr_lo=wr_lo, r_bias=router_bias.astype(F32).reshape(-1, 1))
    return (_trunk(x_prompt, layers, shared), _trunk(x_sample, layers, shared))
```

```python
import functools

import numpy as np
import jax
import jax.numpy as jnp
from jax import lax
from jax.experimental import pallas as pl
from jax.experimental.pallas import tpu as pltpu

F32 = jnp.float32
BF16 = jnp.bfloat16
I32 = jnp.int32

DEPTH = 2
HEAD_DIM = 128
GRID_W = 64
ROPE_THETA = 10000.0
A_CONFIGS = ((128, 1), (512, 4), (2048, 16))
A_HEADS_PER_GROUP = 6
A_HEADS = A_HEADS_PER_GROUP * len(A_CONFIGS)
A_GROUP_COLS = A_HEADS_PER_GROUP * HEAD_DIM
A_OUT = A_GROUP_COLS
B_Q_HEADS = 8
B_KV_HEADS = 2
B_GROUP = B_Q_HEADS // B_KV_HEADS
B_OUT = B_Q_HEADS * HEAD_DIM
C_HEADS = 4
C_OUT = C_HEADS * 2 * HEAD_DIM
N_BRANCH = 3
N_EXPERTS = 16
N_EXPERT_GROUPS = 4
EXPERTS_PER_GROUP = N_EXPERTS // N_EXPERT_GROUPS
TOP_K = 2
DEEPNORM_ALPHA = (2 * DEPTH) ** 0.25
LN_EPS = 1e-5
RMS_EPS = 1e-6
NEG_INF = -1e30
ATTN_SCALE = HEAD_DIM ** -0.5
A_RADIUS = 64

HPG = A_HEADS_PER_GROUP
A_Q0, A_K0, A_V0 = 0, HPG, 2 * HPG
B_Q0 = 3 * HPG
B_K0 = B_Q0 + B_Q_HEADS
B_V0 = B_K0 + B_KV_HEADS
C_Q0 = B_V0 + B_KV_HEADS
C_K0 = C_Q0 + 2 * C_HEADS
C_V0 = C_K0 + 2 * C_HEADS
MAIN_HEAD_COLS = C_V0 + 2 * C_HEADS
MAIN_COLS = MAIN_HEAD_COLS * HEAD_DIM
SRC_BC0 = 3 * A_HEADS

LANE = 128
QKV_TN = 2 * LANE
QKV_TM = 1024
GATE_TM, GATE_TN = 1024, 1024
PROJ_TM, PROJ_TN = 1024, 512
OUT_TM = 512
ROUTER_TM = 512
MOE_BLOCK = 256
COMBINE_TM = 256
DISPATCH_TM = 256
ATTN_TQ = 256
MERGE_TM = 1024
VMEM_LIMIT = 56 * 1024 * 1024
ROW_E, ROW_RANK, ROW_GATE = 0, 2, 4


def _params(sem, vmem=VMEM_LIMIT):
    return pltpu.CompilerParams(dimension_semantics=sem, vmem_limit_bytes=vmem)


def _tile(n, t):
    t = min(n, t)
    assert n % t == 0, (n, t)
    return t


def _rope_head(xh, c1_ref, s1_ref):
    return xh * c1_ref[...] + pltpu.roll(xh, 64, 1) * s1_ref[...]


def _qkv_main_kernel(x_ref, w_ref, c1_ref, s1_ref, cax_ref, sa_ref, sb_ref, g_ref, o_ref):
    j = pl.program_id(1)
    hpb = QKV_TN // LANE
    acc = jnp.dot(x_ref[...], w_ref[...], preferred_element_type=F32)

    def blk(h):
        return h // hpb

    is_q = (j < blk(A_K0)) | ((j >= blk(B_Q0)) & (j < blk(B_K0))) | ((j >= blk(C_Q0)) & (j < blk(C_K0)))
    scale = jnp.where(is_q, ATTN_SCALE, 1.0).astype(F32)
    rope = (j < blk(A_V0)) | ((j >= blk(C_Q0)) & (j < blk(C_V0)))
    axial = (j >= blk(B_Q0)) & (j < blk(B_V0))
    plain = jnp.logical_not(rope | axial)

    @pl.when(plain)
    def _():
        o_ref[...] = acc.astype(o_ref.dtype)

    @pl.when(rope)
    def _():
        for c in range(hpb):
            cols = slice(c * LANE, (c + 1) * LANE)
            o_ref[:, cols] = (_rope_head(acc[:, cols], c1_ref, s1_ref) * scale).astype(o_ref.dtype)

    @pl.when(axial)
    def _():
        g = jnp.where(j < blk(B_K0), g_ref[0:1, :], g_ref[1:2, :])
        for c in range(hpb):
            cols = slice(c * LANE, (c + 1) * LANE)
            xh = acc[:, cols]
            r = xh * lax.rsqrt(jnp.mean(xh * xh, axis=-1, keepdims=True) + RMS_EPS) * g
            y = r * cax_ref[...] + pltpu.roll(r, 96, 1) * sa_ref[...] + pltpu.roll(r, 32, 1) * sb_ref[...]
            o_ref[:, cols] = (y * scale).astype(o_ref.dtype)


def _qkv_main(xb, w, tables, gains, seq):
    t, d = xb.shape
    tm = _tile(seq, QKV_TM)
    nsb = seq // tm
    hpb = QKV_TN // LANE
    a_tiles = 3 * HPG // hpb
    tpp = HPG // hpb
    src_part = A_HEADS // hpb

    def w_map(i, j):
        return (0, jnp.where(j < a_tiles, (j // tpp) * src_part + j % tpp, j - a_tiles + SRC_BC0 // hpb))

    tab_spec = pl.BlockSpec((tm, LANE), lambda i, j: (i % nsb, 0))
    return pl.pallas_call(
        _qkv_main_kernel,
        out_shape=jax.ShapeDtypeStruct((t, MAIN_COLS), BF16),
        grid=(t // tm, MAIN_COLS // QKV_TN),
        in_specs=[pl.BlockSpec((tm, d), lambda i, j: (i, 0)),
                  pl.BlockSpec((d, QKV_TN), w_map),
                  tab_spec, tab_spec, tab_spec, tab_spec, tab_spec,
                  pl.BlockSpec((2, LANE), lambda i, j: (0, 0))],
        out_specs=pl.BlockSpec((tm, QKV_TN), lambda i, j: (i, j)),
        compiler_params=_params(("parallel", "arbitrary")),
        name="qkv_main",
    )(xb, w, *tables, gains)


def _qkv_dil_kernel(x_ref, w_ref, c1_ref, s1_ref, o_ref, y_ref, *, dilation, tpp):
    part = pl.program_id(1) // tpp
    acc = jnp.dot(x_ref[...], w_ref[...], preferred_element_type=F32)
    scale = jnp.where(part == 0, ATTN_SCALE, 1.0).astype(F32)

    hpb = QKV_TN // LANE

    @pl.when(part < 2)
    def _():
        for h in range(hpb):
            y_ref[h] = _rope_head(acc[:, h * LANE:(h + 1) * LANE], c1_ref, s1_ref) * scale

    @pl.when(part == 2)
    def _():
        for h in range(hpb):
            y_ref[h] = acc[:, h * LANE:(h + 1) * LANE]

    rows = y_ref.shape[1] // dilation
    for c in range(dilation):
        for h in range(hpb):
            o_ref[0, c, :, h * LANE:(h + 1) * LANE] = y_ref[h, pl.ds(c, rows, stride=dilation), :].astype(o_ref.dtype)


def _qkv_dil(xb, w, tables, seq, group, dilation):
    t, d = xb.shape
    tm = _tile(seq, QKV_TM)
    nsb = seq // tm
    hpb = QKV_TN // LANE
    tpp = HPG // hpb
    src_part = A_HEADS // hpb
    tab_spec = pl.BlockSpec((tm, LANE), lambda i, j: (i % nsb, 0))
    return pl.pallas_call(
        functools.partial(_qkv_dil_kernel, dilation=dilation, tpp=tpp),
        out_shape=jax.ShapeDtypeStruct((t // seq, dilation, seq // dilation, 3 * A_GROUP_COLS), BF16),
        grid=(t // tm, 3 * tpp),
        in_specs=[pl.BlockSpec((tm, d), lambda i, j: (i, 0)),
                  pl.BlockSpec((d, QKV_TN), lambda i, j: (0, (j // tpp) * src_part + group * tpp + j % tpp)),
                  tab_spec, tab_spec],
        out_specs=pl.BlockSpec((1, dilation, tm // dilation, QKV_TN), lambda i, j: (i // nsb, 0, i % nsb, j)),
        scratch_shapes=[pltpu.VMEM((hpb, tm, LANE), F32)],
        compiler_params=_params(("parallel", "arbitrary")),
        name=f"qkv_dil{group}",
    )(xb, w, tables[0], tables[1])


def _gate_kernel(x_ref, w_ref, b_ref, o_ref):
    acc = jnp.dot(x_ref[...], w_ref[...], preferred_element_type=F32) + b_ref[...]
    o_ref[...] = jax.nn.sigmoid(acc).astype(o_ref.dtype)


def _gate_proj(xb, w, b):
    t, d = xb.shape
    n = w.shape[1]
    tm, tn = _tile(t, GATE_TM), _tile(n, GATE_TN)
    return pl.pallas_call(
        _gate_kernel,
        out_shape=jax.ShapeDtypeStruct((t, n), BF16),
        grid=(t // tm, n // tn),
        in_specs=[pl.BlockSpec((tm, d), lambda i, j: (i, 0)),
                  pl.BlockSpec((d, tn), lambda i, j: (0, j)),
                  pl.BlockSpec((1, tn), lambda i, j: (0, j))],
        out_specs=pl.BlockSpec((tm, tn), lambda i, j: (i, j)),
        compiler_params=_params(("parallel", "arbitrary")),
        name="gate_proj",
    )(xb, w, b)


def _attn_a_kernel(q_ref, k_ref, v_ref, o_ref, lse_ref, *, tq, win, sub_len):
    q0 = pl.program_id(2) * tq
    start = jnp.clip(q0 - A_RADIUS, 0, sub_len - win)
    if win % 16 == 0 and tq % 64 == 0:
        start = pl.multiple_of(start, 16)
    q_pos = q0 + lax.broadcasted_iota(I32, (tq, win), 0)
    k_pos = start + lax.broadcasted_iota(I32, (tq, win), 1)
    valid = jnp.abs(k_pos - q_pos) <= A_RADIUS
    lane = lax.broadcasted_iota(I32, (tq, LANE), 1)
    lse_all = jnp.zeros((tq, LANE), F32)
    for h in range(HPG):
        cols = slice(h * LANE, (h + 1) * LANE)
        q = q_ref[0, 0, :, cols]
        k = k_ref[0, 0, pl.ds(start, win), cols]
        v = v_ref[0, 0, pl.ds(start, win), cols]
        s = lax.dot_general(q, k, (((1,), (1,)), ((), ())), preferred_element_type=F32)
        s = jnp.where(valid, s, NEG_INF)
        m = jnp.max(s, axis=-1, keepdims=True)
        p = jnp.exp(s - m)
        den = jnp.sum(p, axis=-1, keepdims=True)
        o = jnp.dot(p.astype(BF16), v, preferred_element_type=F32) / den
        o_ref[0, 0, :, cols] = o.astype(o_ref.dtype)
        lse_all = jnp.where(lane == h, m + jnp.log(den), lse_all)
    lse_ref[0, 0] = lse_all


def _attn_a(qkv4):
    b, r, sub_len, _ = qkv4.shape
    tq = _tile(sub_len, ATTN_TQ)
    win = min(tq + 2 * A_RADIUS, sub_len)
    return pl.pallas_call(
        functools.partial(_attn_a_kernel, tq=tq, win=win, sub_len=sub_len),
        out_shape=(jax.ShapeDtypeStruct((b, r, sub_len, A_GROUP_COLS), BF16),
                   jax.ShapeDtypeStruct((b, r, sub_len, LANE), F32)),
        grid=(b, r, sub_len // tq),
        in_specs=[pl.BlockSpec((1, 1, tq, A_GROUP_COLS), lambda bi, c, qi: (bi, c, qi, 0)),
                  pl.BlockSpec((1, 1, sub_len, A_GROUP_COLS), lambda bi, c, qi: (bi, c, 0, 1)),
                  pl.BlockSpec((1, 1, sub_len, A_GROUP_COLS), lambda bi, c, qi: (bi, c, 0, 2))],
        out_specs=(pl.BlockSpec((1, 1, tq, A_GROUP_COLS), lambda bi, c, qi: (bi, c, qi, 0)),
                   pl.BlockSpec((1, 1, tq, LANE), lambda bi, c, qi: (bi, c, qi, 0))),
        compiler_params=_params(("parallel", "parallel", "arbitrary")),
        name=f"attn_a_r{r}",
    )(qkv4, qkv4, qkv4)


def _merge_a_kernel(o0_ref, l0_ref, o1_ref, l1_ref, o2_ref, l2_ref, o_ref, so1, sl1, so2, sl2):
    def to_token_order(src_ref, dst_ref):
        r, rows = src_ref.shape[1], src_ref.shape[2]
        for c in range(r):
            for h in range(dst_ref.shape[0]):
                dst_ref[h, pl.ds(c, rows, stride=r), :] = src_ref[0, c, :, h * LANE:(h + 1) * LANE].astype(F32)

    to_token_order(o1_ref, so1)
    to_token_order(l1_ref, sl1)
    to_token_order(o2_ref, so2)
    to_token_order(l2_ref, sl2)
    l0, l1, l2 = l0_ref[...], sl1[0], sl2[0]
    m = jnp.maximum(jnp.maximum(l0, l1), l2)
    e0, e1, e2 = jnp.exp(l0 - m), jnp.exp(l1 - m), jnp.exp(l2 - m)
    den = e0 + e1 + e2
    w0, w1, w2 = e0 / den, e1 / den, e2 / den
    for h in range(HPG):
        cols = slice(h * LANE, (h + 1) * LANE)
        acc = (o0_ref[:, cols].astype(F32) * w0[:, h:h + 1]
               + so1[h] * w1[:, h:h + 1]
               + so2[h] * w2[:, h:h + 1])
        o_ref[:, cols] = acc.astype(o_ref.dtype)


def _merge_a(o0, l0, o1, l1, o2, l2, seq):
    t = o0.shape[0]
    tm = _tile(seq, MERGE_TM)
    nsb = seq // tm

    def res_spec(arr):
        r, n = arr.shape[1], arr.shape[3]
        return pl.BlockSpec((1, r, tm // r, n), lambda i: (i // nsb, 0, i % nsb, 0))

    return pl.pallas_call(
        _merge_a_kernel,
        out_shape=jax.ShapeDtypeStruct((t, A_OUT), BF16),
        grid=(t // tm,),
        in_specs=[pl.BlockSpec((tm, A_GROUP_COLS), lambda i: (i, 0)), pl.BlockSpec((tm, LANE), lambda i: (i, 0)),
                  res_spec(o1), res_spec(l1), res_spec(o2), res_spec(l2)],
        out_specs=pl.BlockSpec((tm, A_OUT), lambda i: (i, 0)),
        scratch_shapes=[pltpu.VMEM((HPG, tm, LANE), F32), pltpu.VMEM((1, tm, LANE), F32),
                        pltpu.VMEM((HPG, tm, LANE), F32), pltpu.VMEM((1, tm, LANE), F32)],
        compiler_params=_params(("parallel",)),
        name="merge_a",
    )(o0, l0, o1, l1, o2, l2)


def _softmax_pv(q, k, v):
    s = lax.dot_general(q, k, (((1,), (1,)), ((), ())), preferred_element_type=F32)
    m = jnp.max(s, axis=-1, keepdims=True)
    p = jnp.exp(s - m)
    den = jnp.sum(p, axis=-1, keepdims=True)
    return jnp.dot(p.astype(BF16), v, preferred_element_type=F32) / den


def _attn_b_kernel(*refs):
    q_refs, (k_ref, v_ref, o_ref) = refs[:B_GROUP], refs[B_GROUP:]
    k = k_ref[0]
    v = v_ref[0]
    for g in range(B_GROUP):
        o = _softmax_pv(q_refs[g][0], k, v)
        o_ref[0, :, g * LANE:(g + 1) * LANE] = o.astype(o_ref.dtype)


def _attn_b(qkv3):
    b, s, _ = qkv3.shape
    tq = _tile(s, ATTN_TQ)
    q_specs = [pl.BlockSpec((1, tq, LANE), functools.partial(lambda bi, h, qi, g: (bi, qi, B_Q0 + h * B_GROUP + g), g=g))
               for g in range(B_GROUP)]
    o = pl.pallas_call(
        _attn_b_kernel,
        out_shape=jax.ShapeDtypeStruct((b, s, B_OUT), BF16),
        grid=(b, B_KV_HEADS, s // tq),
        in_specs=q_specs + [pl.BlockSpec((1, s, LANE), lambda bi, h, qi: (bi, 0, B_K0 + h)),
                            pl.BlockSpec((1, s, LANE), lambda bi, h, qi: (bi, 0, B_V0 + h))],
        out_specs=pl.BlockSpec((1, tq, B_GROUP * LANE), lambda bi, h, qi: (bi, qi, h)),
        compiler_params=_params(("parallel", "parallel", "arbitrary")),
        name="attn_b",
    )(*([qkv3] * (B_GROUP + 2)))
    return o.reshape(b * s, B_OUT)


def _attn_c_kernel(q1_ref, q2_ref, k1_ref, k2_ref, v_ref, lam_ref, g_ref, o_ref, *, lambda_init):
    lam_p = lam_ref[...].astype(F32)
    lam = (jnp.exp(jnp.sum(lam_p[0:1] * lam_p[1:2], axis=-1, keepdims=True))
           - jnp.exp(jnp.sum(lam_p[2:3] * lam_p[3:4], axis=-1, keepdims=True)) + lambda_init)
    v = v_ref[0]
    o = _softmax_pv(q1_ref[0], k1_ref[0], v) - lam * _softmax_pv(q2_ref[0], k2_ref[0], v)
    o = o * lax.rsqrt(jnp.mean(o * o, axis=-1, keepdims=True) + RMS_EPS) * g_ref[...].astype(F32)
    o_ref[0] = (o * (1.0 - lambda_init)).astype(o_ref.dtype)


def _attn_c(qkv3, lam_rows, subln_g, lambda_init):
    b, s, _ = qkv3.shape
    tq = _tile(s, ATTN_TQ)
    o = pl.pallas_call(
        functools.partial(_attn_c_kernel, lambda_init=lambda_init),
        out_shape=jax.ShapeDtypeStruct((b, s, C_OUT), BF16),
        grid=(b, C_HEADS, s // tq),
        in_specs=[pl.BlockSpec((1, tq, LANE), lambda bi, h, qi: (bi, qi, C_Q0 + 2 * h)),
                  pl.BlockSpec((1, tq, LANE), lambda bi, h, qi: (bi, qi, C_Q0 + 2 * h + 1)),
                  pl.BlockSpec((1, s, LANE), lambda bi, h, qi: (bi, 0, C_K0 + 2 * h)),
                  pl.BlockSpec((1, s, LANE), lambda bi, h, qi: (bi, 0, C_K0 + 2 * h + 1)),
                  pl.BlockSpec((1, s, 2 * LANE), lambda bi, h, qi: (bi, 0, C_V0 // 2 + h)),
                  pl.BlockSpec((4, LANE), lambda bi, h, qi: (0, 0)),
                  pl.BlockSpec((1, 2 * LANE), lambda bi, h, qi: (0, 0))],
        out_specs=pl.BlockSpec((1, tq, 2 * LANE), lambda bi, h, qi: (bi, qi, h)),
        compiler_params=_params(("parallel", "parallel", "arbitrary")),
        name="attn_c",
    )(qkv3, qkv3, qkv3, qkv3, qkv3, lam_rows, subln_g)
    return o.reshape(b * s, C_OUT)


def _branch_proj_kernel(oa_ref, ob_ref, oc_ref, g0_ref, g1_ref, g2_ref, wa_ref, wb_ref, wc_ref, o_ref):
    acc = g0_ref[...].astype(F32) * jnp.dot(oa_ref[...], wa_ref[...], preferred_element_type=F32)
    acc = acc + g1_ref[...].astype(F32) * jnp.dot(ob_ref[...], wb_ref[...], preferred_element_type=F32)
    acc = acc + g2_ref[...].astype(F32) * jnp.dot(oc_ref[...], wc_ref[...], preferred_element_type=F32)
    o_ref[...] = acc.astype(o_ref.dtype)


def _branch_proj(oa, ob, oc, gates, wa, wb, wc):
    t = oa.shape[0]
    d = wa.shape[1]
    tm, tn = _tile(t, PROJ_TM), _tile(d, PROJ_TN)
    nb = d // tn
    gspecs = [pl.BlockSpec((tm, tn), functools.partial(lambda i, j, br: (i, br * nb + j), br=br)) for br in range(N_BRANCH)]
    return pl.pallas_call(
        _branch_proj_kernel,
        out_shape=jax.ShapeDtypeStruct((t, d), BF16),
        grid=(t // tm, nb),
        in_specs=[pl.BlockSpec((tm, A_OUT), lambda i, j: (i, 0)),
                  pl.BlockSpec((tm, B_OUT), lambda i, j: (i, 0)),
                  pl.BlockSpec((tm, C_OUT), lambda i, j: (i, 0))] + gspecs +
                 [pl.BlockSpec((A_OUT, tn), lambda i, j: (0, j)),
                  pl.BlockSpec((B_OUT, tn), lambda i, j: (0, j)),
                  pl.BlockSpec((C_OUT, tn), lambda i, j: (0, j))],
        out_specs=pl.BlockSpec((tm, tn), lambda i, j: (i, j)),
        compiler_params=_params(("parallel", "arbitrary")),
        name="branch_proj",
    )(oa, ob, oc, gates, gates, gates, wa, wb, wc)


def _layer_norm(z, g, b):
    mu = jnp.mean(z, axis=-1, keepdims=True)
    zc = z - mu
    var = jnp.mean(zc * zc, axis=-1, keepdims=True)
    return zc * lax.rsqrt(var + LN_EPS) * g + b


def _out_ln_kernel(m_ref, w_ref, x_ref, g_ref, b_ref, o_ref):
    mix = jnp.dot(m_ref[...], w_ref[...], preferred_element_type=F32)
    o_ref[...] = _layer_norm(DEEPNORM_ALPHA * x_ref[...] + mix, g_ref[...], b_ref[...])


def _out_proj_ln(merged, w_out, x, g, b):
    t, d = x.shape
    tm = _tile(t, OUT_TM)
    row = pl.BlockSpec((tm, d), lambda i: (i, 0))
    vec = pl.BlockSpec((1, d), lambda i: (0, 0))
    return pl.pallas_call(
        _out_ln_kernel,
        out_shape=jax.ShapeDtypeStruct((t, d), F32),
        grid=(t // tm,),
        in_specs=[row, pl.BlockSpec((d, d), lambda i: (0, 0)), row, vec, vec],
        out_specs=row,
        compiler_params=_params(("parallel",)),
        name="out_proj_ln",
    )(merged, w_out, x, g, b)


def _router_kernel(x_ref, whi_ref, wlo_ref, bias_ref, er_ref, gt_ref, cnt_ref, carry_ref, *, tm):
    @pl.when(pl.program_id(0) == 0)
    def _():
        carry_ref[...] = jnp.zeros_like(carry_ref)

    x = x_ref[...]
    x_hi = x.astype(BF16)
    x_lo = (x - x_hi.astype(F32)).astype(BF16)
    nt = (((1,), (1,)), ((), ()))
    logits = (lax.dot_general(whi_ref[...], x_hi, nt, preferred_element_type=F32)
              + lax.dot_general(wlo_ref[...], x_hi, nt, preferred_element_type=F32)
              + lax.dot_general(whi_ref[...], x_lo, nt, preferred_element_type=F32))
    scores = jax.nn.sigmoid(logits)
    biased = scores + bias_ref[...]

    def row(a, e):
        return a[e:e + 1, :]

    gscore = []
    for g in range(N_EXPERT_GROUPS):
        a, b, c, d = (row(biased, g * EXPERTS_PER_GROUP + i) for i in range(EXPERTS_PER_GROUP))
        hi1, lo1, hi2, lo2 = jnp.maximum(a, b), jnp.minimum(a, b), jnp.maximum(c, d), jnp.minimum(c, d)
        gscore.append(jnp.maximum(hi1, hi2) + jnp.maximum(jnp.minimum(hi1, hi2), jnp.maximum(lo1, lo2)))
    gsel = jnp.zeros((1, tm), I32)
    best = gscore[0]
    for g in range(1, N_EXPERT_GROUPS):
        better = gscore[g] > best
        gsel = jnp.where(better, g, gsel)
        best = jnp.where(better, gscore[g], best)

    def pick(a, i):
        out = row(a, i)
        for g in range(1, N_EXPERT_GROUPS):
            out = jnp.where(gsel == g, row(a, g * EXPERTS_PER_GROUP + i), out)
        return out

    bv = [pick(biased, i) for i in range(EXPERTS_PER_GROUP)]
    sv = [pick(scores, i) for i in range(EXPERTS_PER_GROUP)]
    i0 = jnp.zeros((1, tm), I32)
    b0 = bv[0]
    for i in range(1, EXPERTS_PER_GROUP):
        better = bv[i] > b0
        i0 = jnp.where(better, i, i0)
        b0 = jnp.where(better, bv[i], b0)
    i1 = jnp.full((1, tm), -1, I32)
    b1 = jnp.full((1, tm), -jnp.inf, F32)
    for i in range(EXPERTS_PER_GROUP):
        better = (i0 != i) & ((i1 < 0) | (bv[i] > b1))
        i1 = jnp.where(better, i, i1)
        b1 = jnp.where(better, bv[i], b1)
    s0 = jnp.zeros((1, tm), F32)
    s1 = jnp.zeros((1, tm), F32)
    for i in range(EXPERTS_PER_GROUP):
        s0 = jnp.where(i0 == i, sv[i], s0)
        s1 = jnp.where(i1 == i, sv[i], s1)
    e0 = gsel * EXPERTS_PER_GROUP + i0
    e1 = gsel * EXPERTS_PER_GROUP + i1
    den = s0 + s1
    g0, g1 = s0 / den, s1 / den

    erow = lax.broadcasted_iota(I32, (N_EXPERTS, tm), 0)
    member = ((erow == e0) | (erow == e1))
    tri = (lax.broadcasted_iota(I32, (tm, tm), 0) < lax.broadcasted_iota(I32, (tm, tm), 1))
    prefix = jnp.dot(member.astype(BF16), tri.astype(BF16), preferred_element_type=F32) + carry_ref[:, 0:1]
    r0 = jnp.sum(jnp.where(erow == e0, prefix, 0.0), axis=0, keepdims=True)
    r1 = jnp.sum(jnp.where(erow == e1, prefix, 0.0), axis=0, keepdims=True)
    carry_ref[...] = carry_ref[...] + jnp.sum(member.astype(F32), axis=1, keepdims=True)
    cnt_ref[...] = carry_ref[...]

    zi = jnp.zeros((1, tm), I32)
    er_ref[...] = jnp.concatenate([e0, e1, r0.astype(I32), r1.astype(I32), zi, zi, zi, zi], axis=0)
    zf = jnp.zeros((1, tm), F32)
    rec = jnp.concatenate([zf, zf, zf, zf, g0, g1, zf, zf, jnp.zeros((LANE - 8, tm), F32)], axis=0)
    gt_ref[...] = rec.T


def _router(x, w_hi_t, w_lo_t, bias_col):
    t, d = x.shape
    tm = _tile(t, ROUTER_TM)
    return pl.pallas_call(
        functools.partial(_router_kernel, tm=tm),
        out_shape=(jax.ShapeDtypeStruct((8, t), I32), jax.ShapeDtypeStruct((t, LANE), F32),
                   jax.ShapeDtypeStruct((N_EXPERTS, LANE), F32)),
        grid=(t // tm,),
        in_specs=[pl.BlockSpec((tm, d), lambda i: (i, 0)),
                  pl.BlockSpec((N_EXPERTS, d), lambda i: (0, 0)),
                  pl.BlockSpec((N_EXPERTS, d), lambda i: (0, 0)),
                  pl.BlockSpec((N_EXPERTS, 1), lambda i: (0, 0))],
        out_specs=(pl.BlockSpec((8, tm), lambda i: (0, i)),
                   pl.BlockSpec((tm, LANE), lambda i: (i, 0)),
                   pl.BlockSpec((N_EXPERTS, LANE), lambda i: (0, 0))),
        scratch_shapes=[pltpu.VMEM((N_EXPERTS, LANE), F32)],
        compiler_params=_params(("arbitrary",)),
        name="router",
    )(x, w_hi_t, w_lo_t, bias_col)


def _slot_row(pad_start_ref, er_ref, r, slot):
    return pad_start_ref[er_ref[ROW_E + slot, r]] + er_ref[ROW_RANK + slot, r]


def _dispatch_kernel(pad_start_ref, er_ref, x_ref, init_hbm, xs_hbm, sem, *, tm):
    del init_hbm

    def row_copy(r, dst_row):
        return pltpu.make_async_copy(x_ref.at[pl.ds(r, 1), :], xs_hbm.at[pl.ds(dst_row, 1), :], sem)

    def issue(r, c):
        row_copy(r, _slot_row(pad_start_ref, er_ref, r, 0)).start()
        row_copy(r, _slot_row(pad_start_ref, er_ref, r, 1)).start()
        return c

    lax.fori_loop(0, tm, issue, 0)

    def drain(r, c):
        row_copy(0, 0).wait()
        row_copy(0, 0).wait()
        return c

    lax.fori_loop(0, tm, drain, 0)


def _dispatch(x, pad_start, er, n_rows):
    t, d = x.shape
    tm = _tile(t, DISPATCH_TM)
    init = jnp.zeros((n_rows, d), F32)
    grid_spec = pltpu.PrefetchScalarGridSpec(
        num_scalar_prefetch=1,
        grid=(t // tm,),
        in_specs=[pl.BlockSpec((8, tm), lambda i, ps: (0, i), memory_space=pltpu.SMEM),
                  pl.BlockSpec((tm, d), lambda i, ps: (i, 0)),
                  pl.BlockSpec(memory_space=pl.ANY)],
        out_specs=pl.BlockSpec(memory_space=pl.ANY),
        scratch_shapes=[pltpu.SemaphoreType.DMA],
    )
    return pl.pallas_call(
        functools.partial(_dispatch_kernel, tm=tm),
        out_shape=jax.ShapeDtypeStruct((n_rows, d), F32),
        grid_spec=grid_spec,
        input_output_aliases={3: 0},
        compiler_params=_params(("arbitrary",)),
        name="moe_dispatch",
    )(pad_start, er, x, init)


def _expert_kernel(blk_e_ref, nact_ref, x_ref, w1_ref, w3_ref, w2_ref, o_ref):
    del blk_e_ref

    @pl.when(pl.program_id(0) < nact_ref[0])
    def _():
        x = x_ref[...].astype(BF16)
        h1 = jnp.dot(x, w1_ref[0], preferred_element_type=F32)
        h3 = jnp.dot(x, w3_ref[0], preferred_element_type=F32)
        h = (h1 * jax.nn.sigmoid(h1) * h3).astype(BF16)
        o_ref[...] = jnp.dot(h, w2_ref[0], preferred_element_type=F32)

    @pl.when(pl.program_id(0) >= nact_ref[0])
    def _():
        o_ref[...] = jnp.zeros_like(o_ref)


def _experts(xs, blk_e, n_active, w1, w3, w2):
    p, d = xs.shape
    f = w1.shape[2]
    nb = p // MOE_BLOCK
    grid_spec = pltpu.PrefetchScalarGridSpec(
        num_scalar_prefetch=2,
        grid=(nb,),
        in_specs=[pl.BlockSpec((MOE_BLOCK, d), lambda i, be, na: (i, 0)),
                  pl.BlockSpec((1, d, f), lambda i, be, na: (be[i], 0, 0)),
                  pl.BlockSpec((1, d, f), lambda i, be, na: (be[i], 0, 0)),
                  pl.BlockSpec((1, f, d), lambda i, be, na: (be[i], 0, 0))],
        out_specs=pl.BlockSpec((MOE_BLOCK, d), lambda i, be, na: (i, 0)),
    )
    return pl.pallas_call(
        _expert_kernel,
        out_shape=jax.ShapeDtypeStruct((p, d), F32),
        grid_spec=grid_spec,
        compiler_params=_params(("arbitrary",)),
        name="moe_experts",
    )(blk_e, n_active, xs, w1, w3, w2)


def _combine_kernel(pad_start_ref, er_ref, x_ref, gt_ref, g_ref, b_ref, ys_hbm, o_ref, ob_ref, buf, sem, *, tm):
    def row_copy(r, slot, src_row):
        return pltpu.make_async_copy(ys_hbm.at[pl.ds(src_row, 1), :], buf.at[slot, pl.ds(r, 1), :], sem)

    def issue(r, c):
        row_copy(r, 0, _slot_row(pad_start_ref, er_ref, r, 0)).start()
        row_copy(r, 1, _slot_row(pad_start_ref, er_ref, r, 1)).start()
        return c

    lax.fori_loop(0, tm, issue, 0)

    def drain(r, c):
        row_copy(0, 0, 0).wait()
        row_copy(0, 0, 0).wait()
        return c

    lax.fori_loop(0, tm, drain, 0)
    gt = gt_ref[...]
    ffn = gt[:, ROW_GATE:ROW_GATE + 1] * buf[0] + gt[:, ROW_GATE + 1:ROW_GATE + 2] * buf[1]
    y = _layer_norm(DEEPNORM_ALPHA * x_ref[...] + ffn, g_ref[...], b_ref[...])
    o_ref[...] = y
    ob_ref[...] = y.astype(ob_ref.dtype)


def _combine_ln(x, ys, pad_start, er, gt, g, b):
    t, d = x.shape
    tm = _tile(t, COMBINE_TM)
    row = pl.BlockSpec((tm, d), lambda i, ps: (i, 0))
    vec = pl.BlockSpec((1, d), lambda i, ps: (0, 0))
    grid_spec = pltpu.PrefetchScalarGridSpec(
        num_scalar_prefetch=1,
        grid=(t // tm,),
        in_specs=[pl.BlockSpec((8, tm), lambda i, ps: (0, i), memory_space=pltpu.SMEM),
                  row,
                  pl.BlockSpec((tm, LANE), lambda i, ps: (i, 0)),
                  vec, vec,
                  pl.BlockSpec(memory_space=pl.ANY)],
        out_specs=(row, row),
        scratch_shapes=[pltpu.VMEM((TOP_K, tm, d), F32), pltpu.SemaphoreType.DMA],
    )
    return pl.pallas_call(
        functools.partial(_combine_kernel, tm=tm),
        out_shape=(jax.ShapeDtypeStruct((t, d), F32), jax.ShapeDtypeStruct((t, d), BF16)),
        grid_spec=grid_spec,
        compiler_params=_params(("arbitrary",)),
        name="moe_combine_ln",
    )(pad_start, er, x, gt, g, b, ys)


def _rope_tables(seq):
    def tab(pos, dim):
        inv = ROPE_THETA ** (-jnp.arange(0, dim, 2, dtype=F32) / dim)
        ang = pos.astype(F32)[:, None] * inv[None, :]
        return jnp.cos(ang), jnp.sin(ang)

    pos = jnp.arange(seq)
    cos1, sin1 = tab(pos, HEAD_DIM)
    cos_r, sin_r = tab(pos // GRID_W, HEAD_DIM // 2)
    cos_c, sin_c = tab(pos % GRID_W, HEAD_DIM // 2)
    z = jnp.zeros_like(sin_r)
    c1 = jnp.concatenate([cos1, cos1], axis=1)
    s1 = jnp.concatenate([-sin1, sin1], axis=1)
    cax = jnp.concatenate([cos_r, cos_r, cos_c, cos_c], axis=1)
    sa = jnp.concatenate([-sin_r, z, -sin_c, z], axis=1)
    sb = jnp.concatenate([z, sin_r, z, sin_c], axis=1)
    return c1, s1, cax, sa, sb


def _moe(x1, lw, shared):
    t, d = x1.shape
    er, gt, cnt = _router(x1, shared["wr_hi"], shared["wr_lo"], shared["r_bias"])
    counts = cnt[:, 0].astype(I32)
    padded = (counts + MOE_BLOCK - 1) // MOE_BLOCK * MOE_BLOCK
    pad_end = jnp.cumsum(padded)
    pad_start = (pad_end - padded).astype(I32)
    n_rows = t * TOP_K + N_EXPERTS * MOE_BLOCK
    nb = n_rows // MOE_BLOCK
    blk_e = jnp.minimum(jnp.searchsorted(pad_end, jnp.arange(nb, dtype=I32) * MOE_BLOCK, side="right"),
                        N_EXPERTS - 1).astype(I32)
    n_active = (pad_end[-1:] // MOE_BLOCK).astype(I32)
    xs = _dispatch(x1, pad_start, er, n_rows)
    ys = _experts(xs, blk_e, n_active, lw["w1"], lw["w3"], lw["w2"])
    return _combine_ln(x1, ys, pad_start, er, gt, lw["ln2_g"], lw["ln2_b"])


def _trunk(x, layers, shared):
    b, s, d = x.shape
    t = b * s
    tables = _rope_tables(s)
    xf = x.reshape(t, d)
    xb = xf.astype(BF16)
    for l, lw in enumerate(layers):
        qkv = _qkv_main(xb, lw["w_in"], tables, lw["qk_gains"], s)
        gates = _gate_proj(xb, lw["w_gate"], lw["b_gate"])
        o0, l0 = _attn_a(qkv.reshape(b, 1, s, MAIN_COLS))
        a_dil = [_attn_a(_qkv_dil(xb, lw["w_in"], tables, s, g, dil))
                 for g, (_, dil) in enumerate(A_CONFIGS) if dil > 1]
        oa = _merge_a(o0.reshape(t, A_GROUP_COLS), l0.reshape(t, LANE), *a_dil[0], *a_dil[1], s)
        qkv3 = qkv.reshape(b, s, MAIN_COLS)
        ob = _attn_b(qkv3)
        lambda_init = 0.8 - 0.6 * float(np.exp(-0.3 * l))
        oc = _attn_c(qkv3, lw["lam_rows"], lw["subln_g"], lambda_init)
        merged = _branch_proj(oa, ob, oc, gates, lw["w_proj_a"], lw["w_proj_b"], lw["w_proj_c"])
        x1 = _out_proj_ln(merged, lw["w_out"], xf, lw["ln1_g"], lw["ln1_b"])
        xf, xb = _moe(x1, lw, shared)
    return xf.reshape(b, s, d)


def kernel(x_prompt, x_sample, w_in, q_norm_g, k_norm_g, lambda_q1, lambda_k1, lambda_q2, lambda_k2, subln_g,
           w_gate, b_gate, w_proj_a, w_proj_b, w_proj_c, w_out, ln1_g, ln1_b, w_router, router_bias,
           w1, w3, w2, ln2_g, ln2_b):
    assert [dil for _, dil in A_CONFIGS] == [1, 4, 16] and all(w // (2 * dil) == A_RADIUS for w, dil in A_CONFIGS)
    depth = w_in.shape[0]
    layers = []
    for l in range(depth):
        layers.append(dict(
            w_in=w_in[l].astype(BF16),
            qk_gains=jnp.stack([q_norm_g[l], k_norm_g[l]]).astype(F32),
            lam_rows=jnp.stack([lambda_q1[l], lambda_k1[l], lambda_q2[l], lambda_k2[l]]).astype(F32),
            subln_g=subln_g[l].reshape(1, -1).astype(F32),
            w_gate=w_gate[l].astype(BF16),
            b_gate=b_gate[l].reshape(1, -1).astype(F32),
            w_proj_a=w_proj_a[l].astype(BF16),
            w_proj_b=w_proj_b[l].astype(BF16),
            w_proj_c=w_proj_c[l].astype(BF16),
            w_out=w_out[l].astype(BF16),
            ln1_g=ln1_g[l].reshape(1, -1).astype(F32),
            ln1_b=ln1_b[l].reshape(1, -1).astype(F32),
            w1=w1[l].astype(BF16), w3=w3[l].astype(BF16), w2=w2[l].astype(BF16),
            ln2_g=ln2_g[l].reshape(1, -1).astype(F32),
            ln2_b=ln2_b[l].reshape(1, -1).astype(F32),
        ))
    wr_t = w_router.astype(F32).T
    wr_hi = wr_t.astype(BF16)
    wr_lo = (wr_t - wr_hi.astype(F32)).astype(BF16)
    shared = dict(wr_hi=wr_hi, wr_lo=wr_lo, r_bias=router_bias.astype(F32).reshape(-1, 1))
    return (_trunk(x_prompt, layers, shared), _trunk(x_sample, layers, shared))
```

```python
import functools

import numpy as np
import jax
import jax.numpy as jnp
from jax import lax
from jax.experimental import pallas as pl
from jax.experimental.pallas import tpu as pltpu

F32 = jnp.float32
BF16 = jnp.bfloat16
I32 = jnp.int32

DEPTH = 2
HEAD_DIM = 128
GRID_W = 64
ROPE_THETA = 10000.0
A_CONFIGS = ((128, 1), (512, 4), (2048, 16))
A_HEADS_PER_GROUP = 6
A_HEADS = A_HEADS_PER_GROUP * len(A_CONFIGS)
A_GROUP_COLS = A_HEADS_PER_GROUP * HEAD_DIM
A_OUT = A_GROUP_COLS
B_Q_HEADS = 8
B_KV_HEADS = 2
B_GROUP = B_Q_HEADS // B_KV_HEADS
B_OUT = B_Q_HEADS * HEAD_DIM
C_HEADS = 4
C_OUT = C_HEADS * 2 * HEAD_DIM
N_BRANCH = 3
N_EXPERTS = 16
N_EXPERT_GROUPS = 4
EXPERTS_PER_GROUP = N_EXPERTS // N_EXPERT_GROUPS
TOP_K = 2
DEEPNORM_ALPHA = (2 * DEPTH) ** 0.25
LN_EPS = 1e-5
RMS_EPS = 1e-6
NEG_INF = -1e30
ATTN_SCALE = HEAD_DIM ** -0.5
A_RADIUS = 64

HPG = A_HEADS_PER_GROUP
B_Q0, B_K0, B_V0 = 0, B_Q_HEADS, B_Q_HEADS + B_KV_HEADS
B_COLS = (B_Q_HEADS + 2 * B_KV_HEADS) * HEAD_DIM
C_Q0, C_K0, C_V0 = 0, 2 * C_HEADS, 4 * C_HEADS
C_COLS = 6 * C_HEADS * HEAD_DIM
A_COLS = 3 * A_HEADS * HEAD_DIM
K_PLAIN, K_ROPE, K_ROPE_Q, K_AXIAL_Q, K_AXIAL_K = range(5)

LANE = 128
QKV_TM = 1024
A_TN, B_TN, C_TN = A_GROUP_COLS, 4 * LANE, 8 * LANE
GATE_TM, GATE_TN = 1024, 1024
PROJ_TM, PROJ_TN = 1024, 512
OUT_TM = 512
ROUTER_TM = 512
MOE_BLOCK = 256
COMBINE_TM = 256
DISPATCH_TM = 256
SLOT_TM = 2048
DMA_UNROLL = 8
ATTN_TQ = 256
MERGE_TM = 1024
VMEM_LIMIT = 56 * 1024 * 1024
ROW_E, ROW_RANK, ROW_GATE = 0, 2, 4


def _params(sem, vmem=VMEM_LIMIT):
    return pltpu.CompilerParams(dimension_semantics=sem, vmem_limit_bytes=vmem)


def _tile(n, t):
    t = min(n, t)
    assert n % t == 0, (n, t)
    return t


def _rope_head(xh, c1_ref, s1_ref):
    return xh * c1_ref[...] + pltpu.roll(xh, 64, 1) * s1_ref[...]


def _head_epilogue(kind, xh, c1_ref, s1_ref, cax_ref, sa_ref, sb_ref, g_ref):
    if kind == K_PLAIN:
        return xh
    if kind in (K_ROPE, K_ROPE_Q):
        y = _rope_head(xh, c1_ref, s1_ref)
    else:
        g = g_ref[0:1, :] if kind == K_AXIAL_Q else g_ref[1:2, :]
        r = xh * lax.rsqrt(jnp.mean(xh * xh, axis=-1, keepdims=True) + RMS_EPS) * g
        y = r * cax_ref[...] + pltpu.roll(r, 96, 1) * sa_ref[...] + pltpu.roll(r, 32, 1) * sb_ref[...]
    return y * ATTN_SCALE if kind in (K_ROPE_Q, K_AXIAL_Q) else y


def _project_kernel(x_ref, w_ref, c1_ref, s1_ref, cax_ref, sa_ref, sb_ref, g_ref, o_ref, *scratch,
                    tile_kinds, dilation):
    j = pl.program_id(1)
    acc = jnp.dot(x_ref[...], w_ref[...], preferred_element_type=F32)
    patterns = {}
    for idx, pattern in enumerate(tile_kinds):
        patterns.setdefault(pattern, []).append(idx)

    def emit(pattern):
        for h, kind in enumerate(pattern):
            cols = slice(h * LANE, (h + 1) * LANE)
            y = _head_epilogue(kind, acc[:, cols], c1_ref, s1_ref, cax_ref, sa_ref, sb_ref, g_ref)
            if dilation == 1:
                o_ref[0, 0, :, cols] = y.astype(o_ref.dtype)
            else:
                scratch[0][h] = y

    for pattern, idxs in patterns.items():
        cond = functools.reduce(jnp.logical_or, [j == i for i in idxs])
        pl.when(cond)(functools.partial(emit, pattern))

    if dilation > 1:
        y_ref = scratch[0]
        rows = y_ref.shape[1] // dilation
        for c in range(dilation):
            for h in range(y_ref.shape[0]):
                o_ref[0, c, :, h * LANE:(h + 1) * LANE] = (
                    y_ref[h, pl.ds(c, rows, stride=dilation), :].astype(o_ref.dtype))


def _project(xb, w, tables, gains, seq, tn, tile_kinds, dilation, name):
    t, d = xb.shape
    n = w.shape[1]
    assert n == tn * len(tile_kinds) and all(len(p) * LANE == tn for p in tile_kinds)
    tm = _tile(seq, QKV_TM)
    nsb = seq // tm
    tab_spec = pl.BlockSpec((tm, LANE), lambda i, j: (i % nsb, 0))
    scratch = [] if dilation == 1 else [pltpu.VMEM((tn // LANE, tm, LANE), F32)]
    return pl.pallas_call(
        functools.partial(_project_kernel, tile_kinds=tuple(tile_kinds), dilation=dilation),
        out_shape=jax.ShapeDtypeStruct((t // seq, dilation, seq // dilation, n), BF16),
        grid=(t // tm, n // tn),
        in_specs=[pl.BlockSpec((tm, d), lambda i, j: (i, 0)),
                  pl.BlockSpec((d, tn), lambda i, j: (0, j)),
                  tab_spec, tab_spec, tab_spec, tab_spec, tab_spec,
                  pl.BlockSpec((2, LANE), lambda i, j: (0, 0))],
        out_specs=pl.BlockSpec((1, dilation, tm // dilation, tn), lambda i, j: (i // nsb, 0, i % nsb, j)),
        scratch_shapes=scratch,
        compiler_params=_params(("parallel", "arbitrary")),
        name=name,
    )(xb, w, *tables, gains)


A_TILE_KINDS = ((K_ROPE_Q,) * HPG, (K_ROPE,) * HPG, (K_PLAIN,) * HPG)
B_TILE_KINDS = ((K_AXIAL_Q,) * 4, (K_AXIAL_Q,) * 4, (K_AXIAL_K,) * B_KV_HEADS + (K_PLAIN,) * B_KV_HEADS)
C_TILE_KINDS = ((K_ROPE_Q,) * 8, (K_ROPE,) * 8, (K_PLAIN,) * 8)


def _gate_kernel(x_ref, w_ref, b_ref, o_ref):
    acc = jnp.dot(x_ref[...], w_ref[...], preferred_element_type=F32) + b_ref[...]
    o_ref[...] = jax.nn.sigmoid(acc).astype(o_ref.dtype)


def _gate_proj(xb, w, b):
    t, d = xb.shape
    n = w.shape[1]
    tm, tn = _tile(t, GATE_TM), _tile(n, GATE_TN)
    return pl.pallas_call(
        _gate_kernel,
        out_shape=jax.ShapeDtypeStruct((t, n), BF16),
        grid=(t // tm, n // tn),
        in_specs=[pl.BlockSpec((tm, d), lambda i, j: (i, 0)),
                  pl.BlockSpec((d, tn), lambda i, j: (0, j)),
                  pl.BlockSpec((1, tn), lambda i, j: (0, j))],
        out_specs=pl.BlockSpec((tm, tn), lambda i, j: (i, j)),
        compiler_params=_params(("parallel", "arbitrary")),
        name="gate_proj",
    )(xb, w, b)


def _attn_a_kernel(q_ref, k_ref, v_ref, o_ref, lse_ref, *, tq, win, sub_len):
    q0 = pl.program_id(2) * tq
    start = jnp.clip(q0 - A_RADIUS, 0, sub_len - win)
    if win % 16 == 0 and tq % 64 == 0:
        start = pl.multiple_of(start, 16)
    q_pos = q0 + lax.broadcasted_iota(I32, (tq, win), 0)
    k_pos = start + lax.broadcasted_iota(I32, (tq, win), 1)
    valid = jnp.abs(k_pos - q_pos) <= A_RADIUS
    lane = lax.broadcasted_iota(I32, (tq, LANE), 1)
    lse_all = jnp.zeros((tq, LANE), F32)
    for h in range(HPG):
        cols = slice(h * LANE, (h + 1) * LANE)
        q = q_ref[0, 0, :, cols]
        k = k_ref[0, 0, pl.ds(start, win), cols]
        v = v_ref[0, 0, pl.ds(start, win), cols]
        s = lax.dot_general(q, k, (((1,), (1,)), ((), ())), preferred_element_type=F32)
        s = jnp.where(valid, s, NEG_INF)
        m = jnp.max(s, axis=-1, keepdims=True)
        p = jnp.exp(s - m)
        den = jnp.sum(p, axis=-1, keepdims=True)
        o = jnp.dot(p.astype(BF16), v, preferred_element_type=F32) / den
        o_ref[0, 0, :, cols] = o.astype(o_ref.dtype)
        lse_all = jnp.where(lane == h, m + jnp.log(den), lse_all)
    lse_ref[0, 0] = lse_all


def _attn_a(qkv4):
    b, r, sub_len, _ = qkv4.shape
    tq = _tile(sub_len, ATTN_TQ)
    win = min(tq + 2 * A_RADIUS, sub_len)
    return pl.pallas_call(
        functools.partial(_attn_a_kernel, tq=tq, win=win, sub_len=sub_len),
        out_shape=(jax.ShapeDtypeStruct((b, r, sub_len, A_GROUP_COLS), BF16),
                   jax.ShapeDtypeStruct((b, r, sub_len, LANE), F32)),
        grid=(b, r, sub_len // tq),
        in_specs=[pl.BlockSpec((1, 1, tq, A_GROUP_COLS), lambda bi, c, qi: (bi, c, qi, 0)),
                  pl.BlockSpec((1, 1, sub_len, A_GROUP_COLS), lambda bi, c, qi: (bi, c, 0, 1)),
                  pl.BlockSpec((1, 1, sub_len, A_GROUP_COLS), lambda bi, c, qi: (bi, c, 0, 2))],
        out_specs=(pl.BlockSpec((1, 1, tq, A_GROUP_COLS), lambda bi, c, qi: (bi, c, qi, 0)),
                   pl.BlockSpec((1, 1, tq, LANE), lambda bi, c, qi: (bi, c, qi, 0))),
        compiler_params=_params(("parallel", "parallel", "arbitrary")),
        name=f"attn_a_r{r}",
    )(qkv4, qkv4, qkv4)


def _merge_a_kernel(o0_ref, l0_ref, o1_ref, l1_ref, o2_ref, l2_ref, o_ref, so1, sl1, so2, sl2):
    def to_token_order(src_ref, dst_ref):
        r, rows = src_ref.shape[1], src_ref.shape[2]
        for c in range(r):
            for h in range(dst_ref.shape[0]):
                dst_ref[h, pl.ds(c, rows, stride=r), :] = src_ref[0, c, :, h * LANE:(h + 1) * LANE].astype(F32)

    to_token_order(o1_ref, so1)
    to_token_order(l1_ref, sl1)
    to_token_order(o2_ref, so2)
    to_token_order(l2_ref, sl2)
    l0, l1, l2 = l0_ref[...], sl1[0], sl2[0]
    m = jnp.maximum(jnp.maximum(l0, l1), l2)
    e0, e1, e2 = jnp.exp(l0 - m), jnp.exp(l1 - m), jnp.exp(l2 - m)
    den = e0 + e1 + e2
    w0, w1, w2 = e0 / den, e1 / den, e2 / den
    for h in range(HPG):
        cols = slice(h * LANE, (h + 1) * LANE)
        acc = (o0_ref[:, cols].astype(F32) * w0[:, h:h + 1]
               + so1[h] * w1[:, h:h + 1]
               + so2[h] * w2[:, h:h + 1])
        o_ref[:, cols] = acc.astype(o_ref.dtype)


def _merge_a(o0, l0, o1, l1, o2, l2, seq):
    t = o0.shape[0]
    tm = _tile(seq, MERGE_TM)
    nsb = seq // tm

    def res_spec(arr):
        r, n = arr.shape[1], arr.shape[3]
        return pl.BlockSpec((1, r, tm // r, n), lambda i: (i // nsb, 0, i % nsb, 0))

    return pl.pallas_call(
        _merge_a_kernel,
        out_shape=jax.ShapeDtypeStruct((t, A_OUT), BF16),
        grid=(t // tm,),
        in_specs=[pl.BlockSpec((tm, A_GROUP_COLS), lambda i: (i, 0)), pl.BlockSpec((tm, LANE), lambda i: (i, 0)),
                  res_spec(o1), res_spec(l1), res_spec(o2), res_spec(l2)],
        out_specs=pl.BlockSpec((tm, A_OUT), lambda i: (i, 0)),
        scratch_shapes=[pltpu.VMEM((HPG, tm, LANE), F32), pltpu.VMEM((1, tm, LANE), F32),
                        pltpu.VMEM((HPG, tm, LANE), F32), pltpu.VMEM((1, tm, LANE), F32)],
        compiler_params=_params(("parallel",)),
        name="merge_a",
    )(o0, l0, o1, l1, o2, l2)


def _softmax_pv(q, k, v):
    s = lax.dot_general(q, k, (((1,), (1,)), ((), ())), preferred_element_type=F32)
    m = jnp.max(s, axis=-1, keepdims=True)
    p = jnp.exp(s - m)
    den = jnp.sum(p, axis=-1, keepdims=True)
    return jnp.dot(p.astype(BF16), v, preferred_element_type=F32) / den


def _attn_b_kernel(*refs):
    q_refs, (k_ref, v_ref, o_ref) = refs[:B_GROUP], refs[B_GROUP:]
    k = k_ref[0]
    v = v_ref[0]
    for g in range(B_GROUP):
        o = _softmax_pv(q_refs[g][0], k, v)
        o_ref[0, :, g * LANE:(g + 1) * LANE] = o.astype(o_ref.dtype)


def _attn_b(qkv3):
    b, s, _ = qkv3.shape
    tq = _tile(s, ATTN_TQ)
    q_specs = [pl.BlockSpec((1, tq, LANE), functools.partial(lambda bi, h, qi, g: (bi, qi, B_Q0 + h * B_GROUP + g), g=g))
               for g in range(B_GROUP)]
    o = pl.pallas_call(
        _attn_b_kernel,
        out_shape=jax.ShapeDtypeStruct((b, s, B_OUT), BF16),
        grid=(b, B_KV_HEADS, s // tq),
        in_specs=q_specs + [pl.BlockSpec((1, s, LANE), lambda bi, h, qi: (bi, 0, B_K0 + h)),
                            pl.BlockSpec((1, s, LANE), lambda bi, h, qi: (bi, 0, B_V0 + h))],
        out_specs=pl.BlockSpec((1, tq, B_GROUP * LANE), lambda bi, h, qi: (bi, qi, h)),
        compiler_params=_params(("parallel", "parallel", "arbitrary")),
        name="attn_b",
    )(*([qkv3] * (B_GROUP + 2)))
    return o.reshape(b * s, B_OUT)


def _attn_c_kernel(q1_ref, q2_ref, k1_ref, k2_ref, v_ref, lam_ref, g_ref, o_ref, *, lambda_init):
    lam_p = lam_ref[...].astype(F32)
    lam = (jnp.exp(jnp.sum(lam_p[0:1] * lam_p[1:2], axis=-1, keepdims=True))
           - jnp.exp(jnp.sum(lam_p[2:3] * lam_p[3:4], axis=-1, keepdims=True)) + lambda_init)
    v = v_ref[0]
    o = _softmax_pv(q1_ref[0], k1_ref[0], v) - lam * _softmax_pv(q2_ref[0], k2_ref[0], v)
    o = o * lax.rsqrt(jnp.mean(o * o, axis=-1, keepdims=True) + RMS_EPS) * g_ref[...].astype(F32)
    o_ref[0] = (o * (1.0 - lambda_init)).astype(o_ref.dtype)


def _attn_c(qkv3, lam_rows, subln_g, lambda_init):
    b, s, _ = qkv3.shape
    tq = _tile(s, ATTN_TQ)
    o = pl.pallas_call(
        functools.partial(_attn_c_kernel, lambda_init=lambda_init),
        out_shape=jax.ShapeDtypeStruct((b, s, C_OUT), BF16),
        grid=(b, C_HEADS, s // tq),
        in_specs=[pl.BlockSpec((1, tq, LANE), lambda bi, h, qi: (bi, qi, C_Q0 + 2 * h)),
                  pl.BlockSpec((1, tq, LANE), lambda bi, h, qi: (bi, qi, C_Q0 + 2 * h + 1)),
                  pl.BlockSpec((1, s, LANE), lambda bi, h, qi: (bi, 0, C_K0 + 2 * h)),
                  pl.BlockSpec((1, s, LANE), lambda bi, h, qi: (bi, 0, C_K0 + 2 * h + 1)),
                  pl.BlockSpec((1, s, 2 * LANE), lambda bi, h, qi: (bi, 0, C_V0 // 2 + h)),
                  pl.BlockSpec((4, LANE), lambda bi, h, qi: (0, 0)),
                  pl.BlockSpec((1, 2 * LANE), lambda bi, h, qi: (0, 0))],
        out_specs=pl.BlockSpec((1, tq, 2 * LANE), lambda bi, h, qi: (bi, qi, h)),
        compiler_params=_params(("parallel", "parallel", "arbitrary")),
        name="attn_c",
    )(qkv3, qkv3, qkv3, qkv3, qkv3, lam_rows, subln_g)
    return o.reshape(b * s, C_OUT)


def _branch_proj_kernel(oa_ref, ob_ref, oc_ref, g0_ref, g1_ref, g2_ref, wa_ref, wb_ref, wc_ref, o_ref):
    acc = g0_ref[...].astype(F32) * jnp.dot(oa_ref[...], wa_ref[...], preferred_element_type=F32)
    acc = acc + g1_ref[...].astype(F32) * jnp.dot(ob_ref[...], wb_ref[...], preferred_element_type=F32)
    acc = acc + g2_ref[...].astype(F32) * jnp.dot(oc_ref[...], wc_ref[...], preferred_element_type=F32)
    o_ref[...] = acc.astype(o_ref.dtype)


def _branch_proj(oa, ob, oc, gates, wa, wb, wc):
    t = oa.shape[0]
    d = wa.shape[1]
    tm, tn = _tile(t, PROJ_TM), _tile(d, PROJ_TN)
    nb = d // tn
    gspecs = [pl.BlockSpec((tm, tn), functools.partial(lambda i, j, br: (i, br * nb + j), br=br)) for br in range(N_BRANCH)]
    return pl.pallas_call(
        _branch_proj_kernel,
        out_shape=jax.ShapeDtypeStruct((t, d), BF16),
        grid=(t // tm, nb),
        in_specs=[pl.BlockSpec((tm, A_OUT), lambda i, j: (i, 0)),
                  pl.BlockSpec((tm, B_OUT), lambda i, j: (i, 0)),
                  pl.BlockSpec((tm, C_OUT), lambda i, j: (i, 0))] + gspecs +
                 [pl.BlockSpec((A_OUT, tn), lambda i, j: (0, j)),
                  pl.BlockSpec((B_OUT, tn), lambda i, j: (0, j)),
                  pl.BlockSpec((C_OUT, tn), lambda i, j: (0, j))],
        out_specs=pl.BlockSpec((tm, tn), lambda i, j: (i, j)),
        compiler_params=_params(("parallel", "arbitrary")),
        name="branch_proj",
    )(oa, ob, oc, gates, gates, gates, wa, wb, wc)


def _layer_norm(z, g, b):
    mu = jnp.mean(z, axis=-1, keepdims=True)
    zc = z - mu
    var = jnp.mean(zc * zc, axis=-1, keepdims=True)
    return zc * lax.rsqrt(var + LN_EPS) * g + b


def _out_ln_kernel(m_ref, w_ref, x_ref, g_ref, b_ref, o_ref):
    mix = jnp.dot(m_ref[...], w_ref[...], preferred_element_type=F32)
    o_ref[...] = _layer_norm(DEEPNORM_ALPHA * x_ref[...] + mix, g_ref[...], b_ref[...])


def _out_proj_ln(merged, w_out, x, g, b):
    t, d = x.shape
    tm = _tile(t, OUT_TM)
    row = pl.BlockSpec((tm, d), lambda i: (i, 0))
    vec = pl.BlockSpec((1, d), lambda i: (0, 0))
    return pl.pallas_call(
        _out_ln_kernel,
        out_shape=jax.ShapeDtypeStruct((t, d), F32),
        grid=(t // tm,),
        in_specs=[row, pl.BlockSpec((d, d), lambda i: (0, 0)), row, vec, vec],
        out_specs=row,
        compiler_params=_params(("parallel",)),
        name="out_proj_ln",
    )(merged, w_out, x, g, b)


def _router_kernel(x_ref, whi_ref, wlo_ref, bias_ref, er_ref, gt_ref, cnt_ref, carry_ref, *, tm):
    @pl.when(pl.program_id(0) == 0)
    def _():
        carry_ref[...] = jnp.zeros_like(carry_ref)

    x = x_ref[...]
    x_hi = x.astype(BF16)
    x_lo = (x - x_hi.astype(F32)).astype(BF16)
    nt = (((1,), (1,)), ((), ()))
    logits = (lax.dot_general(whi_ref[...], x_hi, nt, preferred_element_type=F32)
              + lax.dot_general(wlo_ref[...], x_hi, nt, preferred_element_type=F32)
              + lax.dot_general(whi_ref[...], x_lo, nt, preferred_element_type=F32))
    scores = jax.nn.sigmoid(logits)
    biased = scores + bias_ref[...]

    def row(a, e):
        return a[e:e + 1, :]

    gscore = []
    for g in range(N_EXPERT_GROUPS):
        a, b, c, d = (row(biased, g * EXPERTS_PER_GROUP + i) for i in range(EXPERTS_PER_GROUP))
        hi1, lo1, hi2, lo2 = jnp.maximum(a, b), jnp.minimum(a, b), jnp.maximum(c, d), jnp.minimum(c, d)
        gscore.append(jnp.maximum(hi1, hi2) + jnp.maximum(jnp.minimum(hi1, hi2), jnp.maximum(lo1, lo2)))
    gsel = jnp.zeros((1, tm), I32)
    best = gscore[0]
    for g in range(1, N_EXPERT_GROUPS):
        better = gscore[g] > best
        gsel = jnp.where(better, g, gsel)
        best = jnp.where(better, gscore[g], best)

    def pick(a, i):
        out = row(a, i)
        for g in range(1, N_EXPERT_GROUPS):
            out = jnp.where(gsel == g, row(a, g * EXPERTS_PER_GROUP + i), out)
        return out

    bv = [pick(biased, i) for i in range(EXPERTS_PER_GROUP)]
    sv = [pick(scores, i) for i in range(EXPERTS_PER_GROUP)]
    i0 = jnp.zeros((1, tm), I32)
    b0 = bv[0]
    for i in range(1, EXPERTS_PER_GROUP):
        better = bv[i] > b0
        i0 = jnp.where(better, i, i0)
        b0 = jnp.where(better, bv[i], b0)
    i1 = jnp.full((1, tm), -1, I32)
    b1 = jnp.full((1, tm), -jnp.inf, F32)
    for i in range(EXPERTS_PER_GROUP):
        better = (i0 != i) & ((i1 < 0) | (bv[i] > b1))
        i1 = jnp.where(better, i, i1)
        b1 = jnp.where(better, bv[i], b1)
    s0 = jnp.zeros((1, tm), F32)
    s1 = jnp.zeros((1, tm), F32)
    for i in range(EXPERTS_PER_GROUP):
        s0 = jnp.where(i0 == i, sv[i], s0)
        s1 = jnp.where(i1 == i, sv[i], s1)
    e0 = gsel * EXPERTS_PER_GROUP + i0
    e1 = gsel * EXPERTS_PER_GROUP + i1
    den = s0 + s1
    g0, g1 = s0 / den, s1 / den

    erow = lax.broadcasted_iota(I32, (N_EXPERTS, tm), 0)
    member = ((erow == e0) | (erow == e1))
    tri = (lax.broadcasted_iota(I32, (tm, tm), 0) < lax.broadcasted_iota(I32, (tm, tm), 1))
    prefix = jnp.dot(member.astype(BF16), tri.astype(BF16), preferred_element_type=F32) + carry_ref[:, 0:1]
    r0 = jnp.sum(jnp.where(erow == e0, prefix, 0.0), axis=0, keepdims=True)
    r1 = jnp.sum(jnp.where(erow == e1, prefix, 0.0), axis=0, keepdims=True)
    carry_ref[...] = carry_ref[...] + jnp.sum(member.astype(F32), axis=1, keepdims=True)
    cnt_ref[...] = carry_ref[...]

    zi = jnp.zeros((1, tm), I32)
    er_ref[...] = jnp.concatenate([e0, e1, r0.astype(I32), r1.astype(I32), zi, zi, zi, zi], axis=0)
    zf = jnp.zeros((1, tm), F32)
    rec = jnp.concatenate([zf, zf, zf, zf, g0, g1, zf, zf, jnp.zeros((LANE - 8, tm), F32)], axis=0)
    gt_ref[...] = rec.T


def _router(x, w_hi_t, w_lo_t, bias_col):
    t, d = x.shape
    tm = _tile(t, ROUTER_TM)
    return pl.pallas_call(
        functools.partial(_router_kernel, tm=tm),
        out_shape=(jax.ShapeDtypeStruct((8, t), I32), jax.ShapeDtypeStruct((t, LANE), F32),
                   jax.ShapeDtypeStruct((N_EXPERTS, LANE), F32)),
        grid=(t // tm,),
        in_specs=[pl.BlockSpec((tm, d), lambda i: (i, 0)),
                  pl.BlockSpec((N_EXPERTS, d), lambda i: (0, 0)),
                  pl.BlockSpec((N_EXPERTS, d), lambda i: (0, 0)),
                  pl.BlockSpec((N_EXPERTS, 1), lambda i: (0, 0))],
        out_specs=(pl.BlockSpec((8, tm), lambda i: (0, i)),
                   pl.BlockSpec((tm, LANE), lambda i: (i, 0)),
                   pl.BlockSpec((N_EXPERTS, LANE), lambda i: (0, 0))),
        scratch_shapes=[pltpu.VMEM((N_EXPERTS, LANE), F32)],
        compiler_params=_params(("arbitrary",)),
        name="router",
    )(x, w_hi_t, w_lo_t, bias_col)


def _slot_rows_kernel(pad_start_ref, er_ref, o_ref):
    e = er_ref[ROW_E:ROW_E + TOP_K, :]
    base = jnp.zeros_like(e)
    for x in range(N_EXPERTS):
        base = jnp.where(e == x, pad_start_ref[x], base)
    rows = base + er_ref[ROW_RANK:ROW_RANK + TOP_K, :]
    o_ref[...] = jnp.concatenate([rows, jnp.zeros((8 - TOP_K, rows.shape[1]), I32)], axis=0)


def _slot_rows(pad_start, er):
    t = er.shape[1]
    tm = _tile(t, SLOT_TM)
    grid_spec = pltpu.PrefetchScalarGridSpec(
        num_scalar_prefetch=1,
        grid=(t // tm,),
        in_specs=[pl.BlockSpec((8, tm), lambda i, ps: (0, i))],
        out_specs=pl.BlockSpec((8, tm), lambda i, ps: (0, i)),
    )
    return pl.pallas_call(
        _slot_rows_kernel,
        out_shape=jax.ShapeDtypeStruct((8, t), I32),
        grid_spec=grid_spec,
        compiler_params=_params(("arbitrary",)),
        name="moe_slot_rows",
    )(pad_start, er)


def _dispatch_kernel(rows_ref, x_ref, init_hbm, xs_hbm, sem, *, tm):
    del init_hbm

    def row_copy(r, dst_row):
        return pltpu.make_async_copy(x_ref.at[pl.ds(r, 1), :], xs_hbm.at[pl.ds(dst_row, 1), :], sem)

    def issue(r, c):
        row_copy(r, rows_ref[0, r]).start()
        row_copy(r, rows_ref[1, r]).start()
        return c

    lax.fori_loop(0, tm, issue, 0, unroll=DMA_UNROLL)

    def drain(r, c):
        row_copy(0, 0).wait()
        row_copy(0, 0).wait()
        return c

    lax.fori_loop(0, tm, drain, 0, unroll=DMA_UNROLL)


def _dispatch(x, rows, n_rows):
    t, d = x.shape
    tm = _tile(t, DISPATCH_TM)
    init = jnp.zeros((n_rows, d), F32)
    return pl.pallas_call(
        functools.partial(_dispatch_kernel, tm=tm),
        out_shape=jax.ShapeDtypeStruct((n_rows, d), F32),
        grid=(t // tm,),
        in_specs=[pl.BlockSpec((8, tm), lambda i: (0, i), memory_space=pltpu.SMEM),
                  pl.BlockSpec((tm, d), lambda i: (i, 0)),
                  pl.BlockSpec(memory_space=pl.ANY)],
        out_specs=pl.BlockSpec(memory_space=pl.ANY),
        scratch_shapes=[pltpu.SemaphoreType.DMA],
        input_output_aliases={2: 0},
        compiler_params=_params(("arbitrary",)),
        name="moe_dispatch",
    )(rows, x, init)


def _expert_kernel(blk_e_ref, nact_ref, x_ref, w1_ref, w3_ref, w2_ref, o_ref):
    del blk_e_ref

    @pl.when(pl.program_id(0) < nact_ref[0])
    def _():
        x = x_ref[...].astype(BF16)
        h1 = jnp.dot(x, w1_ref[0], preferred_element_type=F32)
        h3 = jnp.dot(x, w3_ref[0], preferred_element_type=F32)
        h = (h1 * jax.nn.sigmoid(h1) * h3).astype(BF16)
        o_ref[...] = jnp.dot(h, w2_ref[0], preferred_element_type=F32)

    @pl.when(pl.program_id(0) >= nact_ref[0])
    def _():
        o_ref[...] = jnp.zeros_like(o_ref)


def _experts(xs, blk_e, n_active, w1, w3, w2):
    p, d = xs.shape
    f = w1.shape[2]
    nb = p // MOE_BLOCK
    grid_spec = pltpu.PrefetchScalarGridSpec(
        num_scalar_prefetch=2,
        grid=(nb,),
        in_specs=[pl.BlockSpec((MOE_BLOCK, d), lambda i, be, na: (i, 0)),
                  pl.BlockSpec((1, d, f), lambda i, be, na: (be[i], 0, 0)),
                  pl.BlockSpec((1, d, f), lambda i, be, na: (be[i], 0, 0)),
                  pl.BlockSpec((1, f, d), lambda i, be, na: (be[i], 0, 0))],
        out_specs=pl.BlockSpec((MOE_BLOCK, d), lambda i, be, na: (i, 0)),
    )
    return pl.pallas_call(
        _expert_kernel,
        out_shape=jax.ShapeDtypeStruct((p, d), F32),
        grid_spec=grid_spec,
        compiler_params=_params(("arbitrary",)),
        name="moe_experts",
    )(blk_e, n_active, xs, w1, w3, w2)


def _combine_kernel(rows_ref, next_rows_ref, x_ref, gt_ref, g_ref, b_ref, ys_hbm, o_ref, ob_ref, buf, sems, *, tm):
    i = pl.program_id(0)
    n = pl.num_programs(0)

    def row_copy(par, r, slot, src_row):
        return pltpu.make_async_copy(ys_hbm.at[pl.ds(src_row, 1), :], buf.at[par, slot, pl.ds(r, 1), :], sems.at[par])

    def gather(par, idx_ref):
        def issue(r, c):
            row_copy(par, r, 0, idx_ref[0, r]).start()
            row_copy(par, r, 1, idx_ref[1, r]).start()
            return c
        lax.fori_loop(0, tm, issue, 0, unroll=DMA_UNROLL)

    @pl.when(i == 0)
    def _():
        gather(0, rows_ref)

    @pl.when(i + 1 < n)
    def _():
        gather((i + 1) % 2, next_rows_ref)

    par = i % 2

    def drain(r, c):
        row_copy(par, 0, 0, 0).wait()
        row_copy(par, 0, 0, 0).wait()
        return c

    lax.fori_loop(0, tm, drain, 0, unroll=DMA_UNROLL)
    gt = gt_ref[...]
    ffn = gt[:, ROW_GATE:ROW_GATE + 1] * buf[par, 0] + gt[:, ROW_GATE + 1:ROW_GATE + 2] * buf[par, 1]
    y = _layer_norm(DEEPNORM_ALPHA * x_ref[...] + ffn, g_ref[...], b_ref[...])
    o_ref[...] = y
    ob_ref[...] = y.astype(ob_ref.dtype)


def _combine_ln(x, ys, rows, gt, g, b):
    t, d = x.shape
    tm = _tile(t, COMBINE_TM)
    n = t // tm
    row = pl.BlockSpec((tm, d), lambda i: (i, 0))
    vec = pl.BlockSpec((1, d), lambda i: (0, 0))
    return pl.pallas_call(
        functools.partial(_combine_kernel, tm=tm),
        out_shape=(jax.ShapeDtypeStruct((t, d), F32), jax.ShapeDtypeStruct((t, d), BF16)),
        grid=(n,),
        in_specs=[pl.BlockSpec((8, tm), lambda i: (0, i), memory_space=pltpu.SMEM),
                  pl.BlockSpec((8, tm), lambda i: (0, jnp.minimum(i + 1, n - 1)), memory_space=pltpu.SMEM),
                  row,
                  pl.BlockSpec((tm, LANE), lambda i: (i, 0)),
                  vec, vec,
                  pl.BlockSpec(memory_space=pl.ANY)],
        out_specs=(row, row),
        scratch_shapes=[pltpu.VMEM((2, TOP_K, tm, d), F32), pltpu.SemaphoreType.DMA((2,))],
        compiler_params=_params(("arbitrary",)),
        name="moe_combine_ln",
    )(rows, rows, x, gt, g, b, ys)


def _rope_tables(seq):
    def tab(pos, dim):
        inv = ROPE_THETA ** (-jnp.arange(0, dim, 2, dtype=F32) / dim)
        ang = pos.astype(F32)[:, None] * inv[None, :]
        return jnp.cos(ang), jnp.sin(ang)

    pos = jnp.arange(seq)
    cos1, sin1 = tab(pos, HEAD_DIM)
    cos_r, sin_r = tab(pos // GRID_W, HEAD_DIM // 2)
    cos_c, sin_c = tab(pos % GRID_W, HEAD_DIM // 2)
    z = jnp.zeros_like(sin_r)
    c1 = jnp.concatenate([cos1, cos1], axis=1)
    s1 = jnp.concatenate([-sin1, sin1], axis=1)
    cax = jnp.concatenate([cos_r, cos_r, cos_c, cos_c], axis=1)
    sa = jnp.concatenate([-sin_r, z, -sin_c, z], axis=1)
    sb = jnp.concatenate([z, sin_r, z, sin_c], axis=1)
    return c1, s1, cax, sa, sb


def _moe(x1, lw, shared):
    t, d = x1.shape
    er, gt, cnt = _router(x1, shared["wr_hi"], shared["wr_lo"], shared["r_bias"])
    counts = cnt[:, 0].astype(I32)
    padded = (counts + MOE_BLOCK - 1) // MOE_BLOCK * MOE_BLOCK
    pad_end = jnp.cumsum(padded)
    pad_start = (pad_end - padded).astype(I32)
    n_rows = t * TOP_K + N_EXPERTS * MOE_BLOCK
    nb = n_rows // MOE_BLOCK
    blk_e = jnp.minimum(jnp.searchsorted(pad_end, jnp.arange(nb, dtype=I32) * MOE_BLOCK, side="right"),
                        N_EXPERTS - 1).astype(I32)
    n_active = (pad_end[-1:] // MOE_BLOCK).astype(I32)
    rows = _slot_rows(pad_start, er)
    xs = _dispatch(x1, rows, n_rows)
    ys = _experts(xs, blk_e, n_active, lw["w1"], lw["w3"], lw["w2"])
    return _combine_ln(x1, ys, rows, gt, lw["ln2_g"], lw["ln2_b"])


def _trunk(x, layers, shared):
    b, s, d = x.shape
    t = b * s
    tables = _rope_tables(s)
    xf = x.reshape(t, d)
    xb = xf.astype(BF16)
    for l, lw in enumerate(layers):
        gains = lw["qk_gains"]
        gates = _gate_proj(xb, lw["w_gate"], lw["b_gate"])
        a_parts = [_attn_a(_project(xb, lw["w_a"][g], tables, gains, s, A_TN, A_TILE_KINDS, dil, f"proj_a{g}"))
                   for g, (_, dil) in enumerate(A_CONFIGS)]
        (o0, l0), (o1, l1), (o2, l2) = a_parts
        oa = _merge_a(o0.reshape(t, A_GROUP_COLS), l0.reshape(t, LANE), o1, l1, o2, l2, s)
        qkv_b = _project(xb, lw["w_b"], tables, gains, s, B_TN, B_TILE_KINDS, 1, "proj_b")
        ob = _attn_b(qkv_b.reshape(b, s, B_COLS))
        lambda_init = 0.8 - 0.6 * float(np.exp(-0.3 * l))
        qkv_c = _project(xb, lw["w_c"], tables, gains, s, C_TN, C_TILE_KINDS, 1, "proj_c")
        oc = _attn_c(qkv_c.reshape(b, s, C_COLS), lw["lam_rows"], lw["subln_g"], lambda_init)
        merged = _branch_proj(oa, ob, oc, gates, lw["w_proj_a"], lw["w_proj_b"], lw["w_proj_c"])
        x1 = _out_proj_ln(merged, lw["w_out"], xf, lw["ln1_g"], lw["ln1_b"])
        xf, xb = _moe(x1, lw, shared)
    return xf.reshape(b, s, d)


def kernel(x_prompt, x_sample, w_in, q_norm_g, k_norm_g, lambda_q1, lambda_k1, lambda_q2, lambda_k2, subln_g,
           w_gate, b_gate, w_proj_a, w_proj_b, w_proj_c, w_out, ln1_g, ln1_b, w_router, router_bias,
           w1, w3, w2, ln2_g, ln2_b):
    assert [dil for _, dil in A_CONFIGS] == [1, 4, 16] and all(w // (2 * dil) == A_RADIUS for w, dil in A_CONFIGS)
    depth = w_in.shape[0]
    layers = []
    part = A_HEADS * HEAD_DIM
    for l in range(depth):
        w_l = w_in[l].astype(BF16)
        w_a = [jnp.concatenate([w_l[:, p * part + g * A_GROUP_COLS: p * part + (g + 1) * A_GROUP_COLS] for p in range(3)],
                               axis=1) for g in range(len(A_CONFIGS))]
        layers.append(dict(
            w_a=w_a, w_b=w_l[:, A_COLS:A_COLS + B_COLS], w_c=w_l[:, A_COLS + B_COLS:],
            qk_gains=jnp.stack([q_norm_g[l], k_norm_g[l]]).astype(F32),
            lam_rows=jnp.stack([lambda_q1[l], lambda_k1[l], lambda_q2[l], lambda_k2[l]]).astype(F32),
            subln_g=subln_g[l].reshape(1, -1).astype(F32),
            w_gate=w_gate[l].astype(BF16),
            b_gate=b_gate[l].reshape(1, -1).astype(F32),
            w_proj_a=w_proj_a[l].astype(BF16),
            w_proj_b=w_proj_b[l].astype(BF16),
            w_proj_c=w_proj_c[l].astype(BF16),
            w_out=w_out[l].astype(BF16),
            ln1_g=ln1_g[l].reshape(1, -1).astype(F32),
            ln1_b=ln1_b[l].reshape(1, -1).astype(F32),
            w1=w1[l].astype(BF16), w3=w3[l].astype(BF16), w2=w2[l].astype(BF16),
            ln2_g=ln2_g[l].reshape(1, -1).astype(F32),
            ln2_b=ln2_b[l].reshape(1, -1).astype(F32),
        ))
    wr_t = w_router.astype(F32).T
    wr_hi = wr_t.astype(BF16)
    wr_lo = (wr_t - wr_hi.astype(F32)).astype(BF16)
    shared = dict(wr_hi=wr_hi, wr_lo=wr_lo, r_bias=router_bias.astype(F32).reshape(-1, 1))
    return (_trunk(x_prompt, layers, shared), _trunk(x_sample, layers, shared))
```

```python
import functools

import numpy as np
import jax
import jax.numpy as jnp
from jax import lax
from jax.experimental import pallas as pl
from jax.experimental.pallas import tpu as pltpu

F32 = jnp.float32
BF16 = jnp.bfloat16
I32 = jnp.int32

DEPTH = 2
HEAD_DIM = 128
GRID_W = 64
ROPE_THETA = 10000.0
A_CONFIGS = ((128, 1), (512, 4), (2048, 16))
A_HEADS_PER_GROUP = 6
A_HEADS = A_HEADS_PER_GROUP * len(A_CONFIGS)
A_GROUP_COLS = A_HEADS_PER_GROUP * HEAD_DIM
A_OUT = A_GROUP_COLS
B_Q_HEADS = 8
B_KV_HEADS = 2
B_GROUP = B_Q_HEADS // B_KV_HEADS
B_OUT = B_Q_HEADS * HEAD_DIM
C_HEADS = 4
C_OUT = C_HEADS * 2 * HEAD_DIM
N_BRANCH = 3
N_EXPERTS = 16
N_EXPERT_GROUPS = 4
EXPERTS_PER_GROUP = N_EXPERTS // N_EXPERT_GROUPS
TOP_K = 2
DEEPNORM_ALPHA = (2 * DEPTH) ** 0.25
LN_EPS = 1e-5
RMS_EPS = 1e-6
NEG_INF = -1e30
ATTN_SCALE = HEAD_DIM ** -0.5
A_RADIUS = 64

HPG = A_HEADS_PER_GROUP
B_Q0, B_K0, B_V0 = 0, B_Q_HEADS, B_Q_HEADS + B_KV_HEADS
B_COLS = (B_Q_HEADS + 2 * B_KV_HEADS) * HEAD_DIM
C_COLS = 6 * C_HEADS * HEAD_DIM
A_COLS = 3 * A_HEADS * HEAD_DIM
K_PLAIN, K_ROPE, K_ROPE_Q, K_AXIAL_Q, K_AXIAL_K = range(5)

LANE = 128
QKV_TM = 1024
A_TN, B_TN, C_TN = A_GROUP_COLS, 4 * LANE, 8 * LANE
GATE_TM, GATE_TN = 1024, 1024
PROJ_TM, PROJ_TN = 1024, 512
OUT_TM = 512
ROUTER_TM = 512
MOE_BLOCK = 256
COMBINE_TM = 256
DISPATCH_TM = 256
SLOT_TM = 2048
DMA_UNROLL = 8
ATTN_TQ = 256
C_HEADS_PER_STEP = 2
MERGE_TM = 1024
VMEM_LIMIT = 56 * 1024 * 1024
ROW_E, ROW_RANK, ROW_GATE = 0, 2, 4


def _params(sem, vmem=VMEM_LIMIT):
    return pltpu.CompilerParams(dimension_semantics=sem, vmem_limit_bytes=vmem)


def _tile(n, t):
    t = min(n, t)
    assert n % t == 0, (n, t)
    return t


def _rope_head(xh, c1_ref, s1_ref):
    return xh * c1_ref[...] + pltpu.roll(xh, 64, 1) * s1_ref[...]


def _head_epilogue(kind, xh, c1_ref, s1_ref, cax_ref, sa_ref, sb_ref, g_ref):
    if kind == K_PLAIN:
        return xh
    if kind in (K_ROPE, K_ROPE_Q):
        y = _rope_head(xh, c1_ref, s1_ref)
    else:
        g = g_ref[0:1, :] if kind == K_AXIAL_Q else g_ref[1:2, :]
        r = xh * lax.rsqrt(jnp.mean(xh * xh, axis=-1, keepdims=True) + RMS_EPS) * g
        y = r * cax_ref[...] + pltpu.roll(r, 96, 1) * sa_ref[...] + pltpu.roll(r, 32, 1) * sb_ref[...]
    return y * ATTN_SCALE if kind in (K_ROPE_Q, K_AXIAL_Q) else y


def _project_kernel(x_ref, w_ref, c1_ref, s1_ref, cax_ref, sa_ref, sb_ref, g_ref, o_ref, *scratch,
                    tile_kinds, dilation):
    j = pl.program_id(1)
    acc = jnp.dot(x_ref[...], w_ref[...], preferred_element_type=F32)
    patterns = {}
    for idx, pattern in enumerate(tile_kinds):
        patterns.setdefault(pattern, []).append(idx)

    def emit(pattern):
        for h, kind in enumerate(pattern):
            cols = slice(h * LANE, (h + 1) * LANE)
            y = _head_epilogue(kind, acc[:, cols], c1_ref, s1_ref, cax_ref, sa_ref, sb_ref, g_ref)
            if dilation == 1:
                o_ref[0, 0, :, cols] = y.astype(o_ref.dtype)
            else:
                scratch[0][h] = y

    for pattern, idxs in patterns.items():
        cond = functools.reduce(jnp.logical_or, [j == i for i in idxs])
        pl.when(cond)(functools.partial(emit, pattern))

    if dilation > 1:
        y_ref = scratch[0]
        rows = y_ref.shape[1] // dilation
        for c in range(dilation):
            for h in range(y_ref.shape[0]):
                o_ref[0, c, :, h * LANE:(h + 1) * LANE] = (
                    y_ref[h, pl.ds(c, rows, stride=dilation), :].astype(o_ref.dtype))


def _project(xb, w, tables, gains, seq, tn, tile_kinds, dilation, name):
    t, d = xb.shape
    n = w.shape[1]
    assert n == tn * len(tile_kinds) and all(len(p) * LANE == tn for p in tile_kinds)
    tm = _tile(seq, QKV_TM)
    nsb = seq // tm
    tab_spec = pl.BlockSpec((tm, LANE), lambda i, j: (i % nsb, 0))
    scratch = [] if dilation == 1 else [pltpu.VMEM((tn // LANE, tm, LANE), F32)]
    return pl.pallas_call(
        functools.partial(_project_kernel, tile_kinds=tuple(tile_kinds), dilation=dilation),
        out_shape=jax.ShapeDtypeStruct((t // seq, dilation, seq // dilation, n), BF16),
        grid=(t // tm, n // tn),
        in_specs=[pl.BlockSpec((tm, d), lambda i, j: (i, 0)),
                  pl.BlockSpec((d, tn), lambda i, j: (0, j)),
                  tab_spec, tab_spec, tab_spec, tab_spec, tab_spec,
                  pl.BlockSpec((2, LANE), lambda i, j: (0, 0))],
        out_specs=pl.BlockSpec((1, dilation, tm // dilation, tn), lambda i, j: (i // nsb, 0, i % nsb, j)),
        scratch_shapes=scratch,
        compiler_params=_params(("parallel", "arbitrary")),
        name=name,
    )(xb, w, *tables, gains)


A_TILE_KINDS = ((K_ROPE_Q,) * HPG, (K_ROPE,) * HPG, (K_PLAIN,) * HPG)
B_TILE_KINDS = ((K_AXIAL_Q,) * 4, (K_AXIAL_Q,) * 4, (K_AXIAL_K,) * B_KV_HEADS + (K_PLAIN,) * B_KV_HEADS)
C_TILE_KINDS = ((K_ROPE_Q,) * 8, (K_ROPE,) * 8, (K_PLAIN,) * 8)


def _gate_kernel(x_ref, w_ref, b_ref, o_ref):
    acc = jnp.dot(x_ref[...], w_ref[...], preferred_element_type=F32) + b_ref[...]
    o_ref[...] = jax.nn.sigmoid(acc).astype(o_ref.dtype)


def _gate_proj(xb, w, b):
    t, d = xb.shape
    n = w.shape[1]
    tm, tn = _tile(t, GATE_TM), _tile(n, GATE_TN)
    return pl.pallas_call(
        _gate_kernel,
        out_shape=jax.ShapeDtypeStruct((t, n), BF16),
        grid=(t // tm, n // tn),
        in_specs=[pl.BlockSpec((tm, d), lambda i, j: (i, 0)),
                  pl.BlockSpec((d, tn), lambda i, j: (0, j)),
                  pl.BlockSpec((1, tn), lambda i, j: (0, j))],
        out_specs=pl.BlockSpec((tm, tn), lambda i, j: (i, j)),
        compiler_params=_params(("parallel", "arbitrary")),
        name="gate_proj",
    )(xb, w, b)


def _attn_a_kernel(q_ref, k_ref, v_ref, o_ref, lse_ref, *, tq, win, sub_len):
    q0 = pl.program_id(2) * tq
    start = jnp.clip(q0 - A_RADIUS, 0, sub_len - win)
    if win % 16 == 0 and tq % 64 == 0:
        start = pl.multiple_of(start, 16)
    q_pos = q0 + lax.broadcasted_iota(I32, (tq, win), 0)
    k_pos = start + lax.broadcasted_iota(I32, (tq, win), 1)
    valid = jnp.abs(k_pos - q_pos) <= A_RADIUS
    lane = lax.broadcasted_iota(I32, (tq, LANE), 1)
    lse_all = jnp.zeros((tq, LANE), F32)
    for h in range(HPG):
        cols = slice(h * LANE, (h + 1) * LANE)
        q = q_ref[0, 0, :, cols]
        k = k_ref[0, 0, pl.ds(start, win), cols]
        v = v_ref[0, 0, pl.ds(start, win), cols]
        s = lax.dot_general(q, k, (((1,), (1,)), ((), ())), preferred_element_type=F32)
        s = jnp.where(valid, s, NEG_INF)
        m = jnp.max(s, axis=-1, keepdims=True)
        p = jnp.exp(s - m)
        den = jnp.sum(p, axis=-1, keepdims=True)
        o = jnp.dot(p.astype(BF16), v, preferred_element_type=F32) / den
        o_ref[0, 0, :, cols] = o.astype(o_ref.dtype)
        lse_all = jnp.where(lane == h, m + jnp.log(den), lse_all)
    lse_ref[0, 0] = lse_all


def _attn_a(qkv4):
    b, r, sub_len, _ = qkv4.shape
    tq = _tile(sub_len, ATTN_TQ)
    win = min(tq + 2 * A_RADIUS, sub_len)
    return pl.pallas_call(
        functools.partial(_attn_a_kernel, tq=tq, win=win, sub_len=sub_len),
        out_shape=(jax.ShapeDtypeStruct((b, r, sub_len, A_GROUP_COLS), BF16),
                   jax.ShapeDtypeStruct((b, r, sub_len, LANE), F32)),
        grid=(b, r, sub_len // tq),
        in_specs=[pl.BlockSpec((1, 1, tq, A_GROUP_COLS), lambda bi, c, qi: (bi, c, qi, 0)),
                  pl.BlockSpec((1, 1, sub_len, A_GROUP_COLS), lambda bi, c, qi: (bi, c, 0, 1)),
                  pl.BlockSpec((1, 1, sub_len, A_GROUP_COLS), lambda bi, c, qi: (bi, c, 0, 2))],
        out_specs=(pl.BlockSpec((1, 1, tq, A_GROUP_COLS), lambda bi, c, qi: (bi, c, qi, 0)),
                   pl.BlockSpec((1, 1, tq, LANE), lambda bi, c, qi: (bi, c, qi, 0))),
        compiler_params=_params(("parallel", "parallel", "arbitrary")),
        name=f"attn_a_r{r}",
    )(qkv4, qkv4, qkv4)


def _merge_a_kernel(o0_ref, l0_ref, o1_ref, l1_ref, o2_ref, l2_ref, o_ref, so1, sl1, so2, sl2):
    def to_token_order(src_ref, dst_ref):
        r, rows = src_ref.shape[1], src_ref.shape[2]
        for c in range(r):
            for h in range(dst_ref.shape[0]):
                dst_ref[h, pl.ds(c, rows, stride=r), :] = src_ref[0, c, :, h * LANE:(h + 1) * LANE].astype(F32)

    to_token_order(o1_ref, so1)
    to_token_order(l1_ref, sl1)
    to_token_order(o2_ref, so2)
    to_token_order(l2_ref, sl2)
    l0, l1, l2 = l0_ref[...], sl1[0], sl2[0]
    m = jnp.maximum(jnp.maximum(l0, l1), l2)
    e0, e1, e2 = jnp.exp(l0 - m), jnp.exp(l1 - m), jnp.exp(l2 - m)
    den = e0 + e1 + e2
    w0, w1, w2 = e0 / den, e1 / den, e2 / den
    for h in range(HPG):
        cols = slice(h * LANE, (h + 1) * LANE)
        acc = (o0_ref[:, cols].astype(F32) * w0[:, h:h + 1]
               + so1[h] * w1[:, h:h + 1]
               + so2[h] * w2[:, h:h + 1])
        o_ref[:, cols] = acc.astype(o_ref.dtype)


def _merge_a(o0, l0, o1, l1, o2, l2, seq):
    t = o0.shape[0]
    tm = _tile(seq, MERGE_TM)
    nsb = seq // tm

    def res_spec(arr):
        r, n = arr.shape[1], arr.shape[3]
        return pl.BlockSpec((1, r, tm // r, n), lambda i: (i // nsb, 0, i % nsb, 0))

    return pl.pallas_call(
        _merge_a_kernel,
        out_shape=jax.ShapeDtypeStruct((t, A_OUT), BF16),
        grid=(t // tm,),
        in_specs=[pl.BlockSpec((tm, A_GROUP_COLS), lambda i: (i, 0)), pl.BlockSpec((tm, LANE), lambda i: (i, 0)),
                  res_spec(o1), res_spec(l1), res_spec(o2), res_spec(l2)],
        out_specs=pl.BlockSpec((tm, A_OUT), lambda i: (i, 0)),
        scratch_shapes=[pltpu.VMEM((HPG, tm, LANE), F32), pltpu.VMEM((1, tm, LANE), F32),
                        pltpu.VMEM((HPG, tm, LANE), F32), pltpu.VMEM((1, tm, LANE), F32)],
        compiler_params=_params(("parallel",)),
        name="merge_a",
    )(o0, l0, o1, l1, o2, l2)


def _softmax_pv(q, k, v):
    s = lax.dot_general(q, k, (((1,), (1,)), ((), ())), preferred_element_type=F32)
    m = jnp.max(s, axis=-1, keepdims=True)
    p = jnp.exp(s - m)
    den = jnp.sum(p, axis=-1, keepdims=True)
    return jnp.dot(p.astype(BF16), v, preferred_element_type=F32) / den


def _attn_b_kernel(q_ref, k_ref, v_ref, o_ref):
    for h in range(B_KV_HEADS):
        k = k_ref[0, :, h * LANE:(h + 1) * LANE]
        v = v_ref[0, :, h * LANE:(h + 1) * LANE]
        for g in range(B_GROUP):
            cols = slice((h * B_GROUP + g) * LANE, (h * B_GROUP + g + 1) * LANE)
            o_ref[0, :, cols] = _softmax_pv(q_ref[0, :, cols], k, v).astype(o_ref.dtype)


def _attn_b(qkv3):
    b, s, _ = qkv3.shape
    tq = _tile(s, ATTN_TQ)
    kv_cols = B_KV_HEADS * LANE
    o = pl.pallas_call(
        _attn_b_kernel,
        out_shape=jax.ShapeDtypeStruct((b, s, B_OUT), BF16),
        grid=(b, s // tq),
        in_specs=[pl.BlockSpec((1, tq, B_OUT), lambda bi, qi: (bi, qi, 0)),
                  pl.BlockSpec((1, s, kv_cols), lambda bi, qi: (bi, 0, B_K0 * LANE // kv_cols)),
                  pl.BlockSpec((1, s, kv_cols), lambda bi, qi: (bi, 0, B_V0 * LANE // kv_cols))],
        out_specs=pl.BlockSpec((1, tq, B_OUT), lambda bi, qi: (bi, qi, 0)),
        compiler_params=_params(("parallel", "arbitrary")),
        name="attn_b",
    )(qkv3, qkv3, qkv3)
    return o.reshape(b * s, B_OUT)


def _attn_c_kernel(q_ref, k_ref, v_ref, lam_ref, g_ref, o_ref, *, lambda_init):
    lam_p = lam_ref[...].astype(F32)
    lam = (jnp.exp(jnp.sum(lam_p[0:1] * lam_p[1:2], axis=-1, keepdims=True))
           - jnp.exp(jnp.sum(lam_p[2:3] * lam_p[3:4], axis=-1, keepdims=True)) + lambda_init)
    for hh in range(C_HEADS_PER_STEP):
        c0 = hh * 2 * LANE
        first, second = slice(c0, c0 + LANE), slice(c0 + LANE, c0 + 2 * LANE)
        v = v_ref[0, :, c0:c0 + 2 * LANE]
        o = (_softmax_pv(q_ref[0, :, first], k_ref[0, :, first], v)
             - lam * _softmax_pv(q_ref[0, :, second], k_ref[0, :, second], v))
        o = o * lax.rsqrt(jnp.mean(o * o, axis=-1, keepdims=True) + RMS_EPS) * g_ref[...].astype(F32)
        o_ref[0, :, c0:c0 + 2 * LANE] = (o * (1.0 - lambda_init)).astype(o_ref.dtype)


def _attn_c(qkv3, lam_rows, subln_g, lambda_init):
    b, s, _ = qkv3.shape
    tq = _tile(s, ATTN_TQ)
    cols = C_HEADS_PER_STEP * 2 * LANE
    steps = C_HEADS // C_HEADS_PER_STEP
    o = pl.pallas_call(
        functools.partial(_attn_c_kernel, lambda_init=lambda_init),
        out_shape=jax.ShapeDtypeStruct((b, s, C_OUT), BF16),
        grid=(b, steps, s // tq),
        in_specs=[pl.BlockSpec((1, tq, cols), lambda bi, h, qi: (bi, qi, h)),
                  pl.BlockSpec((1, s, cols), lambda bi, h, qi: (bi, 0, steps + h)),
                  pl.BlockSpec((1, s, cols), lambda bi, h, qi: (bi, 0, 2 * steps + h)),
                  pl.BlockSpec((4, LANE), lambda bi, h, qi: (0, 0)),
                  pl.BlockSpec((1, 2 * LANE), lambda bi, h, qi: (0, 0))],
        out_specs=pl.BlockSpec((1, tq, cols), lambda bi, h, qi: (bi, qi, h)),
        compiler_params=_params(("parallel", "parallel", "arbitrary")),
        name="attn_c",
    )(qkv3, qkv3, qkv3, lam_rows, subln_g)
    return o.reshape(b * s, C_OUT)


def _branch_proj_kernel(oa_ref, ob_ref, oc_ref, g0_ref, g1_ref, g2_ref, wa_ref, wb_ref, wc_ref, o_ref):
    acc = g0_ref[...].astype(F32) * jnp.dot(oa_ref[...], wa_ref[...], preferred_element_type=F32)
    acc = acc + g1_ref[...].astype(F32) * jnp.dot(ob_ref[...], wb_ref[...], preferred_element_type=F32)
    acc = acc + g2_ref[...].astype(F32) * jnp.dot(oc_ref[...], wc_ref[...], preferred_element_type=F32)
    o_ref[...] = acc.astype(o_ref.dtype)


def _branch_proj(oa, ob, oc, gates, wa, wb, wc):
    t = oa.shape[0]
    d = wa.shape[1]
    tm, tn = _tile(t, PROJ_TM), _tile(d, PROJ_TN)
    nb = d // tn
    gspecs = [pl.BlockSpec((tm, tn), functools.partial(lambda i, j, br: (i, br * nb + j), br=br)) for br in range(N_BRANCH)]
    return pl.pallas_call(
        _branch_proj_kernel,
        out_shape=jax.ShapeDtypeStruct((t, d), BF16),
        grid=(t // tm, nb),
        in_specs=[pl.BlockSpec((tm, A_OUT), lambda i, j: (i, 0)),
                  pl.BlockSpec((tm, B_OUT), lambda i, j: (i, 0)),
                  pl.BlockSpec((tm, C_OUT), lambda i, j: (i, 0))] + gspecs +
                 [pl.BlockSpec((A_OUT, tn), lambda i, j: (0, j)),
                  pl.BlockSpec((B_OUT, tn), lambda i, j: (0, j)),
                  pl.BlockSpec((C_OUT, tn), lambda i, j: (0, j))],
        out_specs=pl.BlockSpec((tm, tn), lambda i, j: (i, j)),
        compiler_params=_params(("parallel", "arbitrary")),
        name="branch_proj",
    )(oa, ob, oc, gates, gates, gates, wa, wb, wc)


def _layer_norm(z, g, b):
    mu = jnp.mean(z, axis=-1, keepdims=True)
    zc = z - mu
    var = jnp.mean(zc * zc, axis=-1, keepdims=True)
    return zc * lax.rsqrt(var + LN_EPS) * g + b


def _out_ln_kernel(m_ref, w_ref, x_ref, g_ref, b_ref, o_ref):
    mix = jnp.dot(m_ref[...], w_ref[...], preferred_element_type=F32)
    o_ref[...] = _layer_norm(DEEPNORM_ALPHA * x_ref[...] + mix, g_ref[...], b_ref[...])


def _out_proj_ln(merged, w_out, x, g, b):
    t, d = x.shape
    tm = _tile(t, OUT_TM)
    row = pl.BlockSpec((tm, d), lambda i: (i, 0))
    vec = pl.BlockSpec((1, d), lambda i: (0, 0))
    return pl.pallas_call(
        _out_ln_kernel,
        out_shape=jax.ShapeDtypeStruct((t, d), F32),
        grid=(t // tm,),
        in_specs=[row, pl.BlockSpec((d, d), lambda i: (0, 0)), row, vec, vec],
        out_specs=row,
        compiler_params=_params(("parallel",)),
        name="out_proj_ln",
    )(merged, w_out, x, g, b)


def _router_kernel(x_ref, whi_ref, wlo_ref, bias_ref, er_ref, gt_ref, cnt_ref, carry_ref, *, tm):
    @pl.when(pl.program_id(0) == 0)
    def _():
        carry_ref[...] = jnp.zeros_like(carry_ref)

    x = x_ref[...]
    x_hi = x.astype(BF16)
    x_lo = (x - x_hi.astype(F32)).astype(BF16)
    nt = (((1,), (1,)), ((), ()))
    logits = (lax.dot_general(whi_ref[...], x_hi, nt, preferred_element_type=F32)
              + lax.dot_general(wlo_ref[...], x_hi, nt, preferred_element_type=F32)
              + lax.dot_general(whi_ref[...], x_lo, nt, preferred_element_type=F32))
    scores = jax.nn.sigmoid(logits)
    biased = scores + bias_ref[...]

    def row(a, e):
        return a[e:e + 1, :]

    gscore = []
    for g in range(N_EXPERT_GROUPS):
        a, b, c, d = (row(biased, g * EXPERTS_PER_GROUP + i) for i in range(EXPERTS_PER_GROUP))
        hi1, lo1, hi2, lo2 = jnp.maximum(a, b), jnp.minimum(a, b), jnp.maximum(c, d), jnp.minimum(c, d)
        gscore.append(jnp.maximum(hi1, hi2) + jnp.maximum(jnp.minimum(hi1, hi2), jnp.maximum(lo1, lo2)))
    gsel = jnp.zeros((1, tm), I32)
    best = gscore[0]
    for g in range(1, N_EXPERT_GROUPS):
        better = gscore[g] > best
        gsel = jnp.where(better, g, gsel)
        best = jnp.where(better, gscore[g], best)

    def pick(a, i):
        out = row(a, i)
        for g in range(1, N_EXPERT_GROUPS):
            out = jnp.where(gsel == g, row(a, g * EXPERTS_PER_GROUP + i), out)
        return out

    bv = [pick(biased, i) for i in range(EXPERTS_PER_GROUP)]
    sv = [pick(scores, i) for i in range(EXPERTS_PER_GROUP)]
    i0 = jnp.zeros((1, tm), I32)
    b0 = bv[0]
    for i in range(1, EXPERTS_PER_GROUP):
        better = bv[i] > b0
        i0 = jnp.where(better, i, i0)
        b0 = jnp.where(better, bv[i], b0)
    i1 = jnp.full((1, tm), -1, I32)
    b1 = jnp.full((1, tm), -jnp.inf, F32)
    for i in range(EXPERTS_PER_GROUP):
        better = (i0 != i) & ((i1 < 0) | (bv[i] > b1))
        i1 = jnp.where(better, i, i1)
        b1 = jnp.where(better, bv[i], b1)
    s0 = jnp.zeros((1, tm), F32)
    s1 = jnp.zeros((1, tm), F32)
    for i in range(EXPERTS_PER_GROUP):
        s0 = jnp.where(i0 == i, sv[i], s0)
        s1 = jnp.where(i1 == i, sv[i], s1)
    e0 = gsel * EXPERTS_PER_GROUP + i0
    e1 = gsel * EXPERTS_PER_GROUP + i1
    den = s0 + s1
    g0, g1 = s0 / den, s1 / den

    erow = lax.broadcasted_iota(I32, (N_EXPERTS, tm), 0)
    member = ((erow == e0) | (erow == e1))
    tri = (lax.broadcasted_iota(I32, (tm, tm), 0) < lax.broadcasted_iota(I32, (tm, tm), 1))
    prefix = jnp.dot(member.astype(BF16), tri.astype(BF16), preferred_element_type=F32) + carry_ref[:, 0:1]
    r0 = jnp.sum(jnp.where(erow == e0, prefix, 0.0), axis=0, keepdims=True)
    r1 = jnp.sum(jnp.where(erow == e1, prefix, 0.0), axis=0, keepdims=True)
    carry_ref[...] = carry_ref[...] + jnp.sum(member.astype(F32), axis=1, keepdims=True)
    cnt_ref[...] = carry_ref[...]

    zi = jnp.zeros((1, tm), I32)
    er_ref[...] = jnp.concatenate([e0, e1, r0.astype(I32), r1.astype(I32), zi, zi, zi, zi], axis=0)
    zf = jnp.zeros((1, tm), F32)
    rec = jnp.concatenate([zf, zf, zf, zf, g0, g1, zf, zf, jnp.zeros((LANE - 8, tm), F32)], axis=0)
    gt_ref[...] = rec.T


def _router(x, w_hi_t, w_lo_t, bias_col):
    t, d = x.shape
    tm = _tile(t, ROUTER_TM)
    return pl.pallas_call(
        functools.partial(_router_kernel, tm=tm),
        out_shape=(jax.ShapeDtypeStruct((8, t), I32), jax.ShapeDtypeStruct((t, LANE), F32),
                   jax.ShapeDtypeStruct((N_EXPERTS, LANE), F32)),
        grid=(t // tm,),
        in_specs=[pl.BlockSpec((tm, d), lambda i: (i, 0)),
                  pl.BlockSpec((N_EXPERTS, d), lambda i: (0, 0)),
                  pl.BlockSpec((N_EXPERTS, d), lambda i: (0, 0)),
                  pl.BlockSpec((N_EXPERTS, 1), lambda i: (0, 0))],
        out_specs=(pl.BlockSpec((8, tm), lambda i: (0, i)),
                   pl.BlockSpec((tm, LANE), lambda i: (i, 0)),
                   pl.BlockSpec((N_EXPERTS, LANE), lambda i: (0, 0))),
        scratch_shapes=[pltpu.VMEM((N_EXPERTS, LANE), F32)],
        compiler_params=_params(("arbitrary",)),
        name="router",
    )(x, w_hi_t, w_lo_t, bias_col)


def _slot_rows_kernel(pad_start_ref, er_ref, o_ref):
    e = er_ref[ROW_E:ROW_E + TOP_K, :]
    base = jnp.zeros_like(e)
    for x in range(N_EXPERTS):
        base = jnp.where(e == x, pad_start_ref[x], base)
    rows = base + er_ref[ROW_RANK:ROW_RANK + TOP_K, :]
    o_ref[...] = jnp.concatenate([rows, jnp.zeros((8 - TOP_K, rows.shape[1]), I32)], axis=0)


def _slot_rows(pad_start, er):
    t = er.shape[1]
    tm = _tile(t, SLOT_TM)
    grid_spec = pltpu.PrefetchScalarGridSpec(
        num_scalar_prefetch=1,
        grid=(t // tm,),
        in_specs=[pl.BlockSpec((8, tm), lambda i, ps: (0, i))],
        out_specs=pl.BlockSpec((8, tm), lambda i, ps: (0, i)),
    )
    return pl.pallas_call(
        _slot_rows_kernel,
        out_shape=jax.ShapeDtypeStruct((8, t), I32),
        grid_spec=grid_spec,
        compiler_params=_params(("arbitrary",)),
        name="moe_slot_rows",
    )(pad_start, er)


def _dispatch_kernel(fill_end_ref, pad_end_ref, rows_ref, x_ref, xs_hbm, zblk, sem, zsem, *, tm):
    def row_copy(r, dst_row):
        return pltpu.make_async_copy(x_ref.at[pl.ds(r, 1), :], xs_hbm.at[pl.ds(dst_row, 1), :], sem)

    def issue(r, c):
        row_copy(r, rows_ref[0, r]).start()
        row_copy(r, rows_ref[1, r]).start()
        return c

    lax.fori_loop(0, tm, issue, 0, unroll=DMA_UNROLL)

    def zero_row(dst_row):
        return pltpu.make_async_copy(zblk.at[pl.ds(0, 1), :], xs_hbm.at[pl.ds(dst_row, 1), :], zsem)

    def zero_block(blk):
        dst = xs_hbm.at[pl.ds(pl.multiple_of(blk * MOE_BLOCK, MOE_BLOCK), MOE_BLOCK), :]
        return pltpu.make_async_copy(zblk, dst, zsem)

    @pl.when(pl.program_id(0) == 0)
    def _():
        zblk[...] = jnp.zeros_like(zblk)
        for e in range(N_EXPERTS):
            lo, hi = fill_end_ref[e], pad_end_ref[e]
            lax.fori_loop(lo, hi, lambda r, c: (zero_row(r).start(), c)[1], 0)
            lax.fori_loop(lo, hi, lambda r, c: (zero_row(0).wait(), c)[1], 0)
        first, last = pad_end_ref[N_EXPERTS - 1] // MOE_BLOCK, xs_hbm.shape[0] // MOE_BLOCK
        lax.fori_loop(first, last, lambda blk, c: (zero_block(blk).start(), c)[1], 0)
        lax.fori_loop(first, last, lambda blk, c: (zero_block(0).wait(), c)[1], 0)

    def drain(r, c):
        row_copy(0, 0).wait()
        row_copy(0, 0).wait()
        return c

    lax.fori_loop(0, tm, drain, 0, unroll=DMA_UNROLL)


def _dispatch(x, rows, fill_end, pad_end, n_rows):
    t, d = x.shape
    tm = _tile(t, DISPATCH_TM)
    grid_spec = pltpu.PrefetchScalarGridSpec(
        num_scalar_prefetch=2,
        grid=(t // tm,),
        in_specs=[pl.BlockSpec((8, tm), lambda i, fe, pe: (0, i), memory_space=pltpu.SMEM),
                  pl.BlockSpec((tm, d), lambda i, fe, pe: (i, 0))],
        out_specs=pl.BlockSpec(memory_space=pl.ANY),
        scratch_shapes=[pltpu.VMEM((MOE_BLOCK, d), F32), pltpu.SemaphoreType.DMA, pltpu.SemaphoreType.DMA],
    )
    return pl.pallas_call(
        functools.partial(_dispatch_kernel, tm=tm),
        out_shape=jax.ShapeDtypeStruct((n_rows, d), F32),
        grid_spec=grid_spec,
        compiler_params=_params(("arbitrary",)),
        name="moe_dispatch",
    )(fill_end, pad_end, rows, x)


def _expert_kernel(blk_e_ref, nact_ref, x_ref, w1_ref, w3_ref, w2_ref, o_ref):
    del blk_e_ref

    @pl.when(pl.program_id(0) < nact_ref[0])
    def _():
        x = x_ref[...].astype(BF16)
        h1 = jnp.dot(x, w1_ref[0], preferred_element_type=F32)
        h3 = jnp.dot(x, w3_ref[0], preferred_element_type=F32)
        h = (h1 * jax.nn.sigmoid(h1) * h3).astype(BF16)
        o_ref[...] = jnp.dot(h, w2_ref[0], preferred_element_type=F32)

    @pl.when(pl.program_id(0) >= nact_ref[0])
    def _():
        o_ref[...] = jnp.zeros_like(o_ref)


def _experts(xs, blk_e, n_active, w1, w3, w2):
    p, d = xs.shape
    f = w1.shape[2]
    nb = p // MOE_BLOCK
    grid_spec = pltpu.PrefetchScalarGridSpec(
        num_scalar_prefetch=2,
        grid=(nb,),
        in_specs=[pl.BlockSpec((MOE_BLOCK, d), lambda i, be, na: (jnp.minimum(i, na[0] - 1), 0)),
                  pl.BlockSpec((1, d, f), lambda i, be, na: (be[i], 0, 0)),
                  pl.BlockSpec((1, d, f), lambda i, be, na: (be[i], 0, 0)),
                  pl.BlockSpec((1, f, d), lambda i, be, na: (be[i], 0, 0))],
        out_specs=pl.BlockSpec((MOE_BLOCK, d), lambda i, be, na: (i, 0)),
    )
    return pl.pallas_call(
        _expert_kernel,
        out_shape=jax.ShapeDtypeStruct((p, d), F32),
        grid_spec=grid_spec,
        compiler_params=_params(("arbitrary",)),
        name="moe_experts",
    )(blk_e, n_active, xs, w1, w3, w2)


def _combine_kernel(rows_ref, next_rows_ref, x_ref, gt_ref, g_ref, b_ref, ys_hbm, o_ref, ob_ref, buf, sems, *, tm):
    i = pl.program_id(0)
    n = pl.num_programs(0)

    def row_copy(par, r, slot, src_row):
        return pltpu.make_async_copy(ys_hbm.at[pl.ds(src_row, 1), :], buf.at[par, slot, pl.ds(r, 1), :], sems.at[par])

    def gather(par, idx_ref):
        def issue(r, c):
            row_copy(par, r, 0, idx_ref[0, r]).start()
            row_copy(par, r, 1, idx_ref[1, r]).start()
            return c
        lax.fori_loop(0, tm, issue, 0, unroll=DMA_UNROLL)

    @pl.when(i == 0)
    def _():
        gather(0, rows_ref)

    @pl.when(i + 1 < n)
    def _():
        gather((i + 1) % 2, next_rows_ref)

    par = i % 2

    def drain(r, c):
        row_copy(par, 0, 0, 0).wait()
        row_copy(par, 0, 0, 0).wait()
        return c

    lax.fori_loop(0, tm, drain, 0, unroll=DMA_UNROLL)
    gt = gt_ref[...]
    ffn = gt[:, ROW_GATE:ROW_GATE + 1] * buf[par, 0] + gt[:, ROW_GATE + 1:ROW_GATE + 2] * buf[par, 1]
    y = _layer_norm(DEEPNORM_ALPHA * x_ref[...] + ffn, g_ref[...], b_ref[...])
    o_ref[...] = y
    ob_ref[...] = y.astype(ob_ref.dtype)


def _combine_ln(x, ys, rows, gt, g, b):
    t, d = x.shape
    tm = _tile(t, COMBINE_TM)
    n = t // tm
    row = pl.BlockSpec((tm, d), lambda i: (i, 0))
    vec = pl.BlockSpec((1, d), lambda i: (0, 0))
    return pl.pallas_call(
        functools.partial(_combine_kernel, tm=tm),
        out_shape=(jax.ShapeDtypeStruct((t, d), F32), jax.ShapeDtypeStruct((t, d), BF16)),
        grid=(n,),
        in_specs=[pl.BlockSpec((8, tm), lambda i: (0, i), memory_space=pltpu.SMEM),
                  pl.BlockSpec((8, tm), lambda i: (0, jnp.minimum(i + 1, n - 1)), memory_space=pltpu.SMEM),
                  row,
                  pl.BlockSpec((tm, LANE), lambda i: (i, 0)),
                  vec, vec,
                  pl.BlockSpec(memory_space=pl.ANY)],
        out_specs=(row, row),
        scratch_shapes=[pltpu.VMEM((2, TOP_K, tm, d), F32), pltpu.SemaphoreType.DMA((2,))],
        compiler_params=_params(("arbitrary",)),
        name="moe_combine_ln",
    )(rows, rows, x, gt, g, b, ys)


def _rope_tables(seq):
    def tab(pos, dim):
        inv = ROPE_THETA ** (-jnp.arange(0, dim, 2, dtype=F32) / dim)
        ang = pos.astype(F32)[:, None] * inv[None, :]
        return jnp.cos(ang), jnp.sin(ang)

    pos = jnp.arange(seq)
    cos1, sin1 = tab(pos, HEAD_DIM)
    cos_r, sin_r = tab(pos // GRID_W, HEAD_DIM // 2)
    cos_c, sin_c = tab(pos % GRID_W, HEAD_DIM // 2)
    z = jnp.zeros_like(sin_r)
    c1 = jnp.concatenate([cos1, cos1], axis=1)
    s1 = jnp.concatenate([-sin1, sin1], axis=1)
    cax = jnp.concatenate([cos_r, cos_r, cos_c, cos_c], axis=1)
    sa = jnp.concatenate([-sin_r, z, -sin_c, z], axis=1)
    sb = jnp.concatenate([z, sin_r, z, sin_c], axis=1)
    return c1, s1, cax, sa, sb


def _moe(x1, lw, shared):
    t, d = x1.shape
    er, gt, cnt = _router(x1, shared["wr_hi"], shared["wr_lo"], shared["r_bias"])
    counts = cnt[:, 0].astype(I32)
    padded = (counts + MOE_BLOCK - 1) // MOE_BLOCK * MOE_BLOCK
    pad_end = jnp.cumsum(padded)
    pad_start = (pad_end - padded).astype(I32)
    n_rows = t * TOP_K + N_EXPERTS * MOE_BLOCK
    nb = n_rows // MOE_BLOCK
    blk_e = jnp.minimum(jnp.searchsorted(pad_end, jnp.arange(nb, dtype=I32) * MOE_BLOCK, side="right"),
                        N_EXPERTS - 1).astype(I32)
    n_active = (pad_end[-1:] // MOE_BLOCK).astype(I32)
    rows = _slot_rows(pad_start, er)
    xs = _dispatch(x1, rows, (pad_start + counts).astype(I32), pad_end.astype(I32), n_rows)
    ys = _experts(xs, blk_e, n_active, lw["w1"], lw["w3"], lw["w2"])
    return _combine_ln(x1, ys, rows, gt, lw["ln2_g"], lw["ln2_b"])


def _trunk(x, layers, shared):
    b, s, d = x.shape
    t = b * s
    tables = _rope_tables(s)
    xf = x.reshape(t, d)
    xb = xf.astype(BF16)
    for l, lw in enumerate(layers):
        gains = lw["qk_gains"]
        gates = _gate_proj(xb, lw["w_gate"], lw["b_gate"])
        a_parts = [_attn_a(_project(xb, lw["w_a"][g], tables, gains, s, A_TN, A_TILE_KINDS, dil, f"proj_a{g}"))
                   for g, (_, dil) in enumerate(A_CONFIGS)]
        (o0, l0), (o1, l1), (o2, l2) = a_parts
        oa = _merge_a(o0.reshape(t, A_GROUP_COLS), l0.reshape(t, LANE), o1, l1, o2, l2, s)
        qkv_b = _project(xb, lw["w_b"], tables, gains, s, B_TN, B_TILE_KINDS, 1, "proj_b")
        ob = _attn_b(qkv_b.reshape(b, s, B_COLS))
        lambda_init = 0.8 - 0.6 * float(np.exp(-0.3 * l))
        qkv_c = _project(xb, lw["w_c"], tables, gains, s, C_TN, C_TILE_KINDS, 1, "proj_c")
        oc = _attn_c(qkv_c.reshape(b, s, C_COLS), lw["lam_rows"], lw["subln_g"], lambda_init)
        merged = _branch_proj(oa, ob, oc, gates, lw["w_proj_a"], lw["w_proj_b"], lw["w_proj_c"])
        x1 = _out_proj_ln(merged, lw["w_out"], xf, lw["ln1_g"], lw["ln1_b"])
        xf, xb = _moe(x1, lw, shared)
    return xf.reshape(b, s, d)


def kernel(x_prompt, x_sample, w_in, q_norm_g, k_norm_g, lambda_q1, lambda_k1, lambda_q2, lambda_k2, subln_g,
           w_gate, b_gate, w_proj_a, w_proj_b, w_proj_c, w_out, ln1_g, ln1_b, w_router, router_bias,
           w1, w3, w2, ln2_g, ln2_b):
    assert [dil for _, dil in A_CONFIGS] == [1, 4, 16] and all(w // (2 * dil) == A_RADIUS for w, dil in A_CONFIGS)
    depth = w_in.shape[0]
    layers = []
    part = A_HEADS * HEAD_DIM
    for l in range(depth):
        w_l = w_in[l].astype(BF16)
        w_a = [jnp.concatenate([w_l[:, p * part + g * A_GROUP_COLS: p * part + (g + 1) * A_GROUP_COLS] for p in range(3)],
                               axis=1) for g in range(len(A_CONFIGS))]
        layers.append(dict(
            w_a=w_a, w_b=w_l[:, A_COLS:A_COLS + B_COLS], w_c=w_l[:, A_COLS + B_COLS:],
            qk_gains=jnp.stack([q_norm_g[l], k_norm_g[l]]).astype(F32),
            lam_rows=jnp.stack([lambda_q1[l], lambda_k1[l], lambda_q2[l], lambda_k2[l]]).astype(F32),
            subln_g=subln_g[l].reshape(1, -1).astype(F32),
            w_gate=w_gate[l].astype(BF16),
            b_gate=b_gate[l].reshape(1, -1).astype(F32),
            w_proj_a=w_proj_a[l].astype(BF16),
            w_proj_b=w_proj_b[l].astype(BF16),
            w_proj_c=w_proj_c[l].astype(BF16),
            w_out=w_out[l].astype(BF16),
            ln1_g=ln1_g[l].reshape(1, -1).astype(F32),
            ln1_b=ln1_b[l].reshape(1, -1).astype(F32),
            w1=w1[l].astype(BF16), w3=w3[l].astype(BF16), w2=w2[l].astype(BF16),
            ln2_g=ln2_g[l].reshape(1, -1).astype(F32),
            ln2_b=ln2_b[l].reshape(1, -1).astype(F32),
        ))
    wr_t = w_router.astype(F32).T
    wr_hi = wr_t.astype(BF16)
    wr_lo = (wr_t - wr_hi.astype(F32)).astype(BF16)
    shared = dict(wr_hi=wr_hi, wr_lo=wr_lo, r_bias=router_bias.astype(F32).reshape(-1, 1))
    return (_trunk(x_prompt, layers, shared), _trunk(x_sample, layers, shared))
```

```python
import functools

import numpy as np
import jax
import jax.numpy as jnp
from jax import lax
from jax.experimental import pallas as pl
from jax.experimental.pallas import tpu as pltpu

F32 = jnp.float32
BF16 = jnp.bfloat16
I32 = jnp.int32

DEPTH = 2
HEAD_DIM = 128
GRID_W = 64
ROPE_THETA = 10000.0
A_CONFIGS = ((128, 1), (512, 4), (2048, 16))
A_HEADS_PER_GROUP = 6
A_HEADS = A_HEADS_PER_GROUP * len(A_CONFIGS)
A_GROUP_COLS = A_HEADS_PER_GROUP * HEAD_DIM
A_OUT = A_GROUP_COLS
B_Q_HEADS = 8
B_KV_HEADS = 2
B_GROUP = B_Q_HEADS // B_KV_HEADS
B_OUT = B_Q_HEADS * HEAD_DIM
C_HEADS = 4
C_OUT = C_HEADS * 2 * HEAD_DIM
N_BRANCH = 3
N_EXPERTS = 16
N_EXPERT_GROUPS = 4
EXPERTS_PER_GROUP = N_EXPERTS // N_EXPERT_GROUPS
TOP_K = 2
DEEPNORM_ALPHA = (2 * DEPTH) ** 0.25
LN_EPS = 1e-5
RMS_EPS = 1e-6
NEG_INF = -1e30
ATTN_SCALE = HEAD_DIM ** -0.5
A_RADIUS = 64

HPG = A_HEADS_PER_GROUP
B_Q0, B_K0, B_V0 = 0, B_Q_HEADS, B_Q_HEADS + B_KV_HEADS
B_COLS = (B_Q_HEADS + 2 * B_KV_HEADS) * HEAD_DIM
C_COLS = 6 * C_HEADS * HEAD_DIM
A_COLS = 3 * A_HEADS * HEAD_DIM
K_PLAIN, K_ROPE, K_ROPE_Q, K_AXIAL_Q, K_AXIAL_K = range(5)

LANE = 128
QKV_TM = 1024
A_TN, B_TN, C_TN = A_GROUP_COLS, 4 * LANE, 8 * LANE
PROJ_CHUNK_HEADS = 2
GATE_TM, GATE_TN = 1024, 1024
PROJ_TM, PROJ_TN = 1024, 512
OUT_TM = 512
ROUTER_TM = 512
MOE_BLOCK = 256
COMBINE_TM = 256
DISPATCH_TM = 256
SLOT_TM = 2048
DMA_UNROLL = 8
ATTN_TQ = 256
A_UNITS_PER_STEP = 2
C_HEADS_PER_STEP = 2
MERGE_TM = 1024
VMEM_LIMIT = 56 * 1024 * 1024
ROW_E, ROW_RANK, ROW_GATE = 0, 2, 4


def _params(sem, vmem=VMEM_LIMIT):
    return pltpu.CompilerParams(dimension_semantics=sem, vmem_limit_bytes=vmem)


def _tile(n, t):
    t = min(n, t)
    assert n % t == 0, (n, t)
    return t


def _rope_head(xh, c1_ref, s1_ref):
    return xh * c1_ref[...] + pltpu.roll(xh, 64, 1) * s1_ref[...]


def _head_epilogue(kind, xh, c1_ref, s1_ref, cax_ref, sa_ref, sb_ref, g_ref):
    if kind == K_PLAIN:
        return xh
    if kind in (K_ROPE, K_ROPE_Q):
        y = _rope_head(xh, c1_ref, s1_ref)
    else:
        g = g_ref[0:1, :] if kind == K_AXIAL_Q else g_ref[1:2, :]
        r = xh * lax.rsqrt(jnp.mean(xh * xh, axis=-1, keepdims=True) + RMS_EPS) * g
        y = r * cax_ref[...] + pltpu.roll(r, 96, 1) * sa_ref[...] + pltpu.roll(r, 32, 1) * sb_ref[...]
    return y * ATTN_SCALE if kind in (K_ROPE_Q, K_AXIAL_Q) else y


def _project_kernel(x_ref, w_ref, c1_ref, s1_ref, cax_ref, sa_ref, sb_ref, g_ref, o_ref, *scratch,
                    tile_kinds, dilation):
    j = pl.program_id(1)
    patterns = {}
    for idx, pattern in enumerate(tile_kinds):
        patterns.setdefault(pattern, []).append(idx)

    def emit(pattern):
        for c0 in range(0, len(pattern), PROJ_CHUNK_HEADS):
            acc = jnp.dot(x_ref[...], w_ref[:, c0 * LANE:(c0 + PROJ_CHUNK_HEADS) * LANE], preferred_element_type=F32)
            for hh in range(PROJ_CHUNK_HEADS):
                h = c0 + hh
                cols = slice(h * LANE, (h + 1) * LANE)
                y = _head_epilogue(pattern[h], acc[:, hh * LANE:(hh + 1) * LANE],
                                   c1_ref, s1_ref, cax_ref, sa_ref, sb_ref, g_ref)
                if dilation == 1:
                    o_ref[0, 0, :, cols] = y.astype(o_ref.dtype)
                else:
                    scratch[0][h] = y

    for pattern, idxs in patterns.items():
        cond = functools.reduce(jnp.logical_or, [j == i for i in idxs])
        pl.when(cond)(functools.partial(emit, pattern))

    if dilation > 1:
        y_ref = scratch[0]
        rows = y_ref.shape[1] // dilation
        for c in range(dilation):
            for h in range(y_ref.shape[0]):
                o_ref[0, c, :, h * LANE:(h + 1) * LANE] = (
                    y_ref[h, pl.ds(c, rows, stride=dilation), :].astype(o_ref.dtype))


def _project(xb, w, tables, gains, seq, tn, tile_kinds, dilation, name):
    t, d = xb.shape
    n = w.shape[1]
    assert n == tn * len(tile_kinds) and all(len(p) * LANE == tn for p in tile_kinds)
    tm = _tile(seq, QKV_TM)
    nsb = seq // tm
    tab_spec = pl.BlockSpec((tm, LANE), lambda i, j: (i % nsb, 0))
    scratch = [] if dilation == 1 else [pltpu.VMEM((tn // LANE, tm, LANE), F32)]
    return pl.pallas_call(
        functools.partial(_project_kernel, tile_kinds=tuple(tile_kinds), dilation=dilation),
        out_shape=jax.ShapeDtypeStruct((t // seq, dilation, seq // dilation, n), BF16),
        grid=(t // tm, n // tn),
        in_specs=[pl.BlockSpec((tm, d), lambda i, j: (i, 0)),
                  pl.BlockSpec((d, tn), lambda i, j: (0, j)),
                  tab_spec, tab_spec, tab_spec, tab_spec, tab_spec,
                  pl.BlockSpec((2, LANE), lambda i, j: (0, 0))],
        out_specs=pl.BlockSpec((1, dilation, tm // dilation, tn), lambda i, j: (i // nsb, 0, i % nsb, j)),
        scratch_shapes=scratch,
        compiler_params=_params(("parallel", "arbitrary")),
        name=name,
    )(xb, w, *tables, gains)


A_TILE_KINDS = ((K_ROPE_Q,) * HPG, (K_ROPE,) * HPG, (K_PLAIN,) * HPG)
B_TILE_KINDS = ((K_AXIAL_Q,) * 4, (K_AXIAL_Q,) * 4, (K_AXIAL_K,) * B_KV_HEADS + (K_PLAIN,) * B_KV_HEADS)
C_TILE_KINDS = ((K_ROPE_Q,) * 8, (K_ROPE,) * 8, (K_PLAIN,) * 8)


def _gate_kernel(x_ref, w_ref, b_ref, o_ref):
    acc = jnp.dot(x_ref[...], w_ref[...], preferred_element_type=F32) + b_ref[...]
    o_ref[...] = jax.nn.sigmoid(acc).astype(o_ref.dtype)


def _gate_proj(xb, w, b):
    t, d = xb.shape
    n = w.shape[1]
    tm, tn = _tile(t, GATE_TM), _tile(n, GATE_TN)
    return pl.pallas_call(
        _gate_kernel,
        out_shape=jax.ShapeDtypeStruct((t, n), BF16),
        grid=(t // tm, n // tn),
        in_specs=[pl.BlockSpec((tm, d), lambda i, j: (i, 0)),
                  pl.BlockSpec((d, tn), lambda i, j: (0, j)),
                  pl.BlockSpec((1, tn), lambda i, j: (0, j))],
        out_specs=pl.BlockSpec((tm, tn), lambda i, j: (i, j)),
        compiler_params=_params(("parallel", "arbitrary")),
        name="gate_proj",
    )(xb, w, b)


def _attn_a_kernel(q_ref, k_ref, v_ref, o_ref, lse_ref, *, tq, win, sub_len):
    n_res, n_qb = q_ref.shape[1], q_ref.shape[2] // tq
    lane = lax.broadcasted_iota(I32, (tq, LANE), 1)
    for qb in range(n_qb):
        q0 = (pl.program_id(2) * n_qb + qb) * tq
        start = jnp.clip(q0 - A_RADIUS, 0, sub_len - win)
        if win % 16 == 0 and tq % 64 == 0:
            start = pl.multiple_of(start, 16)
        q_pos = q0 + lax.broadcasted_iota(I32, (tq, win), 0)
        k_pos = start + lax.broadcasted_iota(I32, (tq, win), 1)
        bias = jnp.where(jnp.abs(k_pos - q_pos) <= A_RADIUS, 0.0, NEG_INF).astype(F32)
        rows = slice(qb * tq, (qb + 1) * tq)
        for c in range(n_res):
            lse_all = jnp.zeros((tq, LANE), F32)
            for h in range(HPG):
                cols = slice(h * LANE, (h + 1) * LANE)
                q = q_ref[0, c, rows, cols]
                k = k_ref[0, c, pl.ds(start, win), cols]
                v = v_ref[0, c, pl.ds(start, win), cols]
                s = lax.dot_general(q, k, (((1,), (1,)), ((), ())), preferred_element_type=F32) + bias
                m = jnp.max(s, axis=-1, keepdims=True)
                p = jnp.exp(s - m)
                den = jnp.sum(p, axis=-1, keepdims=True)
                o = jnp.dot(p.astype(BF16), v, preferred_element_type=F32) / den
                o_ref[0, c, rows, cols] = o.astype(o_ref.dtype)
                lse_all = jnp.where(lane == h, m + jnp.log(den), lse_all)
            lse_ref[0, c, rows, :] = lse_all


def _attn_a(qkv4):
    b, r, sub_len, _ = qkv4.shape
    tq = _tile(sub_len, ATTN_TQ)
    win = min(tq + 2 * A_RADIUS, sub_len)
    n_qb = min(A_UNITS_PER_STEP, sub_len // tq)
    n_res = min(max(A_UNITS_PER_STEP // n_qb, 1), r)
    assert (sub_len // tq) % n_qb == 0 and r % n_res == 0
    return pl.pallas_call(
        functools.partial(_attn_a_kernel, tq=tq, win=win, sub_len=sub_len),
        out_shape=(jax.ShapeDtypeStruct((b, r, sub_len, A_GROUP_COLS), BF16),
                   jax.ShapeDtypeStruct((b, r, sub_len, LANE), F32)),
        grid=(b, r // n_res, sub_len // (tq * n_qb)),
        in_specs=[pl.BlockSpec((1, n_res, tq * n_qb, A_GROUP_COLS), lambda bi, c, qi: (bi, c, qi, 0)),
                  pl.BlockSpec((1, n_res, sub_len, A_GROUP_COLS), lambda bi, c, qi: (bi, c, 0, 1)),
                  pl.BlockSpec((1, n_res, sub_len, A_GROUP_COLS), lambda bi, c, qi: (bi, c, 0, 2))],
        out_specs=(pl.BlockSpec((1, n_res, tq * n_qb, A_GROUP_COLS), lambda bi, c, qi: (bi, c, qi, 0)),
                   pl.BlockSpec((1, n_res, tq * n_qb, LANE), lambda bi, c, qi: (bi, c, qi, 0))),
        compiler_params=_params(("parallel", "parallel", "arbitrary")),
        name=f"attn_a_r{r}",
    )(qkv4, qkv4, qkv4)


def _merge_a_kernel(o0_ref, l0_ref, o1_ref, l1_ref, o2_ref, l2_ref, o_ref, so1, sl1, so2, sl2):
    def to_token_order(src_ref, dst_ref):
        r, rows = src_ref.shape[1], src_ref.shape[2]
        for c in range(r):
            for h in range(dst_ref.shape[0]):
                dst_ref[h, pl.ds(c, rows, stride=r), :] = src_ref[0, c, :, h * LANE:(h + 1) * LANE].astype(F32)

    to_token_order(o1_ref, so1)
    to_token_order(l1_ref, sl1)
    to_token_order(o2_ref, so2)
    to_token_order(l2_ref, sl2)
    l0, l1, l2 = l0_ref[...], sl1[0], sl2[0]
    m = jnp.maximum(jnp.maximum(l0, l1), l2)
    e0, e1, e2 = jnp.exp(l0 - m), jnp.exp(l1 - m), jnp.exp(l2 - m)
    den = e0 + e1 + e2
    w0, w1, w2 = e0 / den, e1 / den, e2 / den
    for h in range(HPG):
        cols = slice(h * LANE, (h + 1) * LANE)
        acc = (o0_ref[:, cols].astype(F32) * w0[:, h:h + 1]
               + so1[h] * w1[:, h:h + 1]
               + so2[h] * w2[:, h:h + 1])
        o_ref[:, cols] = acc.astype(o_ref.dtype)


def _merge_a(o0, l0, o1, l1, o2, l2, seq):
    t = o0.shape[0]
    tm = _tile(seq, MERGE_TM)
    nsb = seq // tm

    def res_spec(arr):
        r, n = arr.shape[1], arr.shape[3]
        return pl.BlockSpec((1, r, tm // r, n), lambda i: (i // nsb, 0, i % nsb, 0))

    return pl.pallas_call(
        _merge_a_kernel,
        out_shape=jax.ShapeDtypeStruct((t, A_OUT), BF16),
        grid=(t // tm,),
        in_specs=[pl.BlockSpec((tm, A_GROUP_COLS), lambda i: (i, 0)), pl.BlockSpec((tm, LANE), lambda i: (i, 0)),
                  res_spec(o1), res_spec(l1), res_spec(o2), res_spec(l2)],
        out_specs=pl.BlockSpec((tm, A_OUT), lambda i: (i, 0)),
        scratch_shapes=[pltpu.VMEM((HPG, tm, LANE), F32), pltpu.VMEM((1, tm, LANE), F32),
                        pltpu.VMEM((HPG, tm, LANE), F32), pltpu.VMEM((1, tm, LANE), F32)],
        compiler_params=_params(("parallel",)),
        name="merge_a",
    )(o0, l0, o1, l1, o2, l2)


def _softmax_pv(q, k, v):
    s = lax.dot_general(q, k, (((1,), (1,)), ((), ())), preferred_element_type=F32)
    m = jnp.max(s, axis=-1, keepdims=True)
    p = jnp.exp(s - m)
    den = jnp.sum(p, axis=-1, keepdims=True)
    return jnp.dot(p.astype(BF16), v, preferred_element_type=F32) / den


def _attn_b_kernel(q_ref, k_ref, v_ref, o_ref):
    for h in range(B_KV_HEADS):
        k = k_ref[0, :, h * LANE:(h + 1) * LANE]
        v = v_ref[0, :, h * LANE:(h + 1) * LANE]
        for g in range(B_GROUP):
            cols = slice((h * B_GROUP + g) * LANE, (h * B_GROUP + g + 1) * LANE)
            o_ref[0, :, cols] = _softmax_pv(q_ref[0, :, cols], k, v).astype(o_ref.dtype)


def _attn_b(qkv3):
    b, s, _ = qkv3.shape
    tq = _tile(s, ATTN_TQ)
    kv_cols = B_KV_HEADS * LANE
    o = pl.pallas_call(
        _attn_b_kernel,
        out_shape=jax.ShapeDtypeStruct((b, s, B_OUT), BF16),
        grid=(b, s // tq),
        in_specs=[pl.BlockSpec((1, tq, B_OUT), lambda bi, qi: (bi, qi, 0)),
                  pl.BlockSpec((1, s, kv_cols), lambda bi, qi: (bi, 0, B_K0 * LANE // kv_cols)),
                  pl.BlockSpec((1, s, kv_cols), lambda bi, qi: (bi, 0, B_V0 * LANE // kv_cols))],
        out_specs=pl.BlockSpec((1, tq, B_OUT), lambda bi, qi: (bi, qi, 0)),
        compiler_params=_params(("parallel", "arbitrary")),
        name="attn_b",
    )(qkv3, qkv3, qkv3)
    return o.reshape(b * s, B_OUT)


def _attn_c_kernel(q_ref, k_ref, v_ref, lam_ref, g_ref, o_ref, *, lambda_init):
    lam_p = lam_ref[...].astype(F32)
    lam = (jnp.exp(jnp.sum(lam_p[0:1] * lam_p[1:2], axis=-1, keepdims=True))
           - jnp.exp(jnp.sum(lam_p[2:3] * lam_p[3:4], axis=-1, keepdims=True)) + lambda_init)
    for hh in range(C_HEADS_PER_STEP):
        c0 = hh * 2 * LANE
        first, second = slice(c0, c0 + LANE), slice(c0 + LANE, c0 + 2 * LANE)
        v = v_ref[0, :, c0:c0 + 2 * LANE]
        o = (_softmax_pv(q_ref[0, :, first], k_ref[0, :, first], v)
             - lam * _softmax_pv(q_ref[0, :, second], k_ref[0, :, second], v))
        o = o * lax.rsqrt(jnp.mean(o * o, axis=-1, keepdims=True) + RMS_EPS) * g_ref[...].astype(F32)
        o_ref[0, :, c0:c0 + 2 * LANE] = (o * (1.0 - lambda_init)).astype(o_ref.dtype)


def _attn_c(qkv3, lam_rows, subln_g, lambda_init):
    b, s, _ = qkv3.shape
    tq = _tile(s, ATTN_TQ)
    cols = C_HEADS_PER_STEP * 2 * LANE
    steps = C_HEADS // C_HEADS_PER_STEP
    o = pl.pallas_call(
        functools.partial(_attn_c_kernel, lambda_init=lambda_init),
        out_shape=jax.ShapeDtypeStruct((b, s, C_OUT), BF16),
        grid=(b, steps, s // tq),
        in_specs=[pl.BlockSpec((1, tq, cols), lambda bi, h, qi: (bi, qi, h)),
                  pl.BlockSpec((1, s, cols), lambda bi, h, qi: (bi, 0, steps + h)),
                  pl.BlockSpec((1, s, cols), lambda bi, h, qi: (bi, 0, 2 * steps + h)),
                  pl.BlockSpec((4, LANE), lambda bi, h, qi: (0, 0)),
                  pl.BlockSpec((1, 2 * LANE), lambda bi, h, qi: (0, 0))],
        out_specs=pl.BlockSpec((1, tq, cols), lambda bi, h, qi: (bi, qi, h)),
        compiler_params=_params(("parallel", "parallel", "arbitrary")),
        name="attn_c",
    )(qkv3, qkv3, qkv3, lam_rows, subln_g)
    return o.reshape(b * s, C_OUT)


def _branch_proj_kernel(oa_ref, ob_ref, oc_ref, g0_ref, g1_ref, g2_ref, wa_ref, wb_ref, wc_ref, o_ref):
    acc = g0_ref[...].astype(F32) * jnp.dot(oa_ref[...], wa_ref[...], preferred_element_type=F32)
    acc = acc + g1_ref[...].astype(F32) * jnp.dot(ob_ref[...], wb_ref[...], preferred_element_type=F32)
    acc = acc + g2_ref[...].astype(F32) * jnp.dot(oc_ref[...], wc_ref[...], preferred_element_type=F32)
    o_ref[...] = acc.astype(o_ref.dtype)


def _branch_proj(oa, ob, oc, gates, wa, wb, wc):
    t = oa.shape[0]
    d = wa.shape[1]
    tm, tn = _tile(t, PROJ_TM), _tile(d, PROJ_TN)
    nb = d // tn
    gspecs = [pl.BlockSpec((tm, tn), functools.partial(lambda i, j, br: (i, br * nb + j), br=br)) for br in range(N_BRANCH)]
    return pl.pallas_call(
        _branch_proj_kernel,
        out_shape=jax.ShapeDtypeStruct((t, d), BF16),
        grid=(t // tm, nb),
        in_specs=[pl.BlockSpec((tm, A_OUT), lambda i, j: (i, 0)),
                  pl.BlockSpec((tm, B_OUT), lambda i, j: (i, 0)),
                  pl.BlockSpec((tm, C_OUT), lambda i, j: (i, 0))] + gspecs +
                 [pl.BlockSpec((A_OUT, tn), lambda i, j: (0, j)),
                  pl.BlockSpec((B_OUT, tn), lambda i, j: (0, j)),
                  pl.BlockSpec((C_OUT, tn), lambda i, j: (0, j))],
        out_specs=pl.BlockSpec((tm, tn), lambda i, j: (i, j)),
        compiler_params=_params(("parallel", "arbitrary")),
        name="branch_proj",
    )(oa, ob, oc, gates, gates, gates, wa, wb, wc)


def _layer_norm(z, g, b):
    mu = jnp.mean(z, axis=-1, keepdims=True)
    zc = z - mu
    var = jnp.mean(zc * zc, axis=-1, keepdims=True)
    return zc * lax.rsqrt(var + LN_EPS) * g + b


def _out_ln_kernel(m_ref, w_ref, x_ref, g_ref, b_ref, o_ref):
    mix = jnp.dot(m_ref[...], w_ref[...], preferred_element_type=F32)
    o_ref[...] = _layer_norm(DEEPNORM_ALPHA * x_ref[...] + mix, g_ref[...], b_ref[...])


def _out_proj_ln(merged, w_out, x, g, b):
    t, d = x.shape
    tm = _tile(t, OUT_TM)
    row = pl.BlockSpec((tm, d), lambda i: (i, 0))
    vec = pl.BlockSpec((1, d), lambda i: (0, 0))
    return pl.pallas_call(
        _out_ln_kernel,
        out_shape=jax.ShapeDtypeStruct((t, d), F32),
        grid=(t // tm,),
        in_specs=[row, pl.BlockSpec((d, d), lambda i: (0, 0)), row, vec, vec],
        out_specs=row,
        compiler_params=_params(("parallel",)),
        name="out_proj_ln",
    )(merged, w_out, x, g, b)


def _router_kernel(x_ref, whi_ref, wlo_ref, bias_ref, er_ref, gt_ref, cnt_ref, carry_ref, *, tm):
    @pl.when(pl.program_id(0) == 0)
    def _():
        carry_ref[...] = jnp.zeros_like(carry_ref)

    x = x_ref[...]
    x_hi = x.astype(BF16)
    x_lo = (x - x_hi.astype(F32)).astype(BF16)
    nt = (((1,), (1,)), ((), ()))
    logits = (lax.dot_general(whi_ref[...], x_hi, nt, preferred_element_type=F32)
              + lax.dot_general(wlo_ref[...], x_hi, nt, preferred_element_type=F32)
              + lax.dot_general(whi_ref[...], x_lo, nt, preferred_element_type=F32))
    scores = jax.nn.sigmoid(logits)
    biased = scores + bias_ref[...]

    def row(a, e):
        return a[e:e + 1, :]

    gscore = []
    for g in range(N_EXPERT_GROUPS):
        a, b, c, d = (row(biased, g * EXPERTS_PER_GROUP + i) for i in range(EXPERTS_PER_GROUP))
        hi1, lo1, hi2, lo2 = jnp.maximum(a, b), jnp.minimum(a, b), jnp.maximum(c, d), jnp.minimum(c, d)
        gscore.append(jnp.maximum(hi1, hi2) + jnp.maximum(jnp.minimum(hi1, hi2), jnp.maximum(lo1, lo2)))
    gsel = jnp.zeros((1, tm), I32)
    best = gscore[0]
    for g in range(1, N_EXPERT_GROUPS):
        better = gscore[g] > best
        gsel = jnp.where(better, g, gsel)
        best = jnp.where(better, gscore[g], best)

    def pick(a, i):
        out = row(a, i)
        for g in range(1, N_EXPERT_GROUPS):
            out = jnp.where(gsel == g, row(a, g * EXPERTS_PER_GROUP + i), out)
        return out

    bv = [pick(biased, i) for i in range(EXPERTS_PER_GROUP)]
    sv = [pick(scores, i) for i in range(EXPERTS_PER_GROUP)]
    i0 = jnp.zeros((1, tm), I32)
    b0 = bv[0]
    for i in range(1, EXPERTS_PER_GROUP):
        better = bv[i] > b0
        i0 = jnp.where(better, i, i0)
        b0 = jnp.where(better, bv[i], b0)
    i1 = jnp.full((1, tm), -1, I32)
    b1 = jnp.full((1, tm), -jnp.inf, F32)
    for i in range(EXPERTS_PER_GROUP):
        better = (i0 != i) & ((i1 < 0) | (bv[i] > b1))
        i1 = jnp.where(better, i, i1)
        b1 = jnp.where(better, bv[i], b1)
    s0 = jnp.zeros((1, tm), F32)
    s1 = jnp.zeros((1, tm), F32)
    for i in range(EXPERTS_PER_GROUP):
        s0 = jnp.where(i0 == i, sv[i], s0)
        s1 = jnp.where(i1 == i, sv[i], s1)
    e0 = gsel * EXPERTS_PER_GROUP + i0
    e1 = gsel * EXPERTS_PER_GROUP + i1
    den = s0 + s1
    g0, g1 = s0 / den, s1 / den

    erow = lax.broadcasted_iota(I32, (N_EXPERTS, tm), 0)
    member = ((erow == e0) | (erow == e1))
    tri = (lax.broadcasted_iota(I32, (tm, tm), 0) < lax.broadcasted_iota(I32, (tm, tm), 1))
    prefix = jnp.dot(member.astype(BF16), tri.astype(BF16), preferred_element_type=F32) + carry_ref[:, 0:1]
    r0 = jnp.sum(jnp.where(erow == e0, prefix, 0.0), axis=0, keepdims=True)
    r1 = jnp.sum(jnp.where(erow == e1, prefix, 0.0), axis=0, keepdims=True)
    carry_ref[...] = carry_ref[...] + jnp.sum(member.astype(F32), axis=1, keepdims=True)
    cnt_ref[...] = carry_ref[...]

    zi = jnp.zeros((1, tm), I32)
    er_ref[...] = jnp.concatenate([e0, e1, r0.astype(I32), r1.astype(I32), zi, zi, zi, zi], axis=0)
    zf = jnp.zeros((1, tm), F32)
    rec = jnp.concatenate([zf, zf, zf, zf, g0, g1, zf, zf, jnp.zeros((LANE - 8, tm), F32)], axis=0)
    gt_ref[...] = rec.T


def _router(x, w_hi_t, w_lo_t, bias_col):
    t, d = x.shape
    tm = _tile(t, ROUTER_TM)
    return pl.pallas_call(
        functools.partial(_router_kernel, tm=tm),
        out_shape=(jax.ShapeDtypeStruct((8, t), I32), jax.ShapeDtypeStruct((t, LANE), F32),
                   jax.ShapeDtypeStruct((N_EXPERTS, LANE), F32)),
        grid=(t // tm,),
        in_specs=[pl.BlockSpec((tm, d), lambda i: (i, 0)),
                  pl.BlockSpec((N_EXPERTS, d), lambda i: (0, 0)),
                  pl.BlockSpec((N_EXPERTS, d), lambda i: (0, 0)),
                  pl.BlockSpec((N_EXPERTS, 1), lambda i: (0, 0))],
        out_specs=(pl.BlockSpec((8, tm), lambda i: (0, i)),
                   pl.BlockSpec((tm, LANE), lambda i: (i, 0)),
                   pl.BlockSpec((N_EXPERTS, LANE), lambda i: (0, 0))),
        scratch_shapes=[pltpu.VMEM((N_EXPERTS, LANE), F32)],
        compiler_params=_params(("arbitrary",)),
        name="router",
    )(x, w_hi_t, w_lo_t, bias_col)


def _slot_rows_kernel(pad_start_ref, er_ref, o_ref):
    e = er_ref[ROW_E:ROW_E + TOP_K, :]
    base = jnp.zeros_like(e)
    for x in range(N_EXPERTS):
        base = jnp.where(e == x, pad_start_ref[x], base)
    rows = base + er_ref[ROW_RANK:ROW_RANK + TOP_K, :]
    o_ref[...] = jnp.concatenate([rows, jnp.zeros((8 - TOP_K, rows.shape[1]), I32)], axis=0)


def _slot_rows(pad_start, er):
    t = er.shape[1]
    tm = _tile(t, SLOT_TM)
    grid_spec = pltpu.PrefetchScalarGridSpec(
        num_scalar_prefetch=1,
        grid=(t // tm,),
        in_specs=[pl.BlockSpec((8, tm), lambda i, ps: (0, i))],
        out_specs=pl.BlockSpec((8, tm), lambda i, ps: (0, i)),
    )
    return pl.pallas_call(
        _slot_rows_kernel,
        out_shape=jax.ShapeDtypeStruct((8, t), I32),
        grid_spec=grid_spec,
        compiler_params=_params(("arbitrary",)),
        name="moe_slot_rows",
    )(pad_start, er)


def _dispatch_kernel(fill_end_ref, pad_end_ref, rows_ref, x_ref, xs_hbm, zblk, sem, zsem, *, tm):
    def row_copy(r, dst_row):
        return pltpu.make_async_copy(x_ref.at[pl.ds(r, 1), :], xs_hbm.at[pl.ds(dst_row, 1), :], sem)

    def issue(r, c):
        row_copy(r, rows_ref[0, r]).start()
        row_copy(r, rows_ref[1, r]).start()
        return c

    lax.fori_loop(0, tm, issue, 0, unroll=DMA_UNROLL)

    def zero_row(dst_row):
        return pltpu.make_async_copy(zblk.at[pl.ds(0, 1), :], xs_hbm.at[pl.ds(dst_row, 1), :], zsem)

    def zero_block(blk):
        dst = xs_hbm.at[pl.ds(pl.multiple_of(blk * MOE_BLOCK, MOE_BLOCK), MOE_BLOCK), :]
        return pltpu.make_async_copy(zblk, dst, zsem)

    @pl.when(pl.program_id(0) == 0)
    def _():
        zblk[...] = jnp.zeros_like(zblk)
        for e in range(N_EXPERTS):
            lo, hi = fill_end_ref[e], pad_end_ref[e]
            lax.fori_loop(lo, hi, lambda r, c: (zero_row(r).start(), c)[1], 0)
            lax.fori_loop(lo, hi, lambda r, c: (zero_row(0).wait(), c)[1], 0)
        first, last = pad_end_ref[N_EXPERTS - 1] // MOE_BLOCK, xs_hbm.shape[0] // MOE_BLOCK
        lax.fori_loop(first, last, lambda blk, c: (zero_block(blk).start(), c)[1], 0)
        lax.fori_loop(first, last, lambda blk, c: (zero_block(0).wait(), c)[1], 0)

    def drain(r, c):
        row_copy(0, 0).wait()
        row_copy(0, 0).wait()
        return c

    lax.fori_loop(0, tm, drain, 0, unroll=DMA_UNROLL)


def _dispatch(x, rows, fill_end, pad_end, n_rows):
    t, d = x.shape
    tm = _tile(t, DISPATCH_TM)
    grid_spec = pltpu.PrefetchScalarGridSpec(
        num_scalar_prefetch=2,
        grid=(t // tm,),
        in_specs=[pl.BlockSpec((8, tm), lambda i, fe, pe: (0, i), memory_space=pltpu.SMEM),
                  pl.BlockSpec((tm, d), lambda i, fe, pe: (i, 0))],
        out_specs=pl.BlockSpec(memory_space=pl.ANY),
        scratch_shapes=[pltpu.VMEM((MOE_BLOCK, d), F32), pltpu.SemaphoreType.DMA, pltpu.SemaphoreType.DMA],
    )
    return pl.pallas_call(
        functools.partial(_dispatch_kernel, tm=tm),
        out_shape=jax.ShapeDtypeStruct((n_rows, d), F32),
        grid_spec=grid_spec,
        compiler_params=_params(("arbitrary",)),
        name="moe_dispatch",
    )(fill_end, pad_end, rows, x)


def _expert_kernel(blk_e_ref, nact_ref, x_ref, w1_ref, w3_ref, w2_ref, o_ref):
    del blk_e_ref

    @pl.when(pl.program_id(0) < nact_ref[0])
    def _():
        x = x_ref[...].astype(BF16)
        h1 = jnp.dot(x, w1_ref[0], preferred_element_type=F32)
        h3 = jnp.dot(x, w3_ref[0], preferred_element_type=F32)
        h = (h1 * jax.nn.sigmoid(h1) * h3).astype(BF16)
        o_ref[...] = jnp.dot(h, w2_ref[0], preferred_element_type=F32)

    @pl.when(pl.program_id(0) >= nact_ref[0])
    def _():
        o_ref[...] = jnp.zeros_like(o_ref)


def _experts(xs, blk_e, n_active, w1, w3, w2):
    p, d = xs.shape
    f = w1.shape[2]
    nb = p // MOE_BLOCK
    grid_spec = pltpu.PrefetchScalarGridSpec(
        num_scalar_prefetch=2,
        grid=(nb,),
        in_specs=[pl.BlockSpec((MOE_BLOCK, d), lambda i, be, na: (jnp.minimum(i, na[0] - 1), 0)),
                  pl.BlockSpec((1, d, f), lambda i, be, na: (be[i], 0, 0)),
                  pl.BlockSpec((1, d, f), lambda i, be, na: (be[i], 0, 0)),
                  pl.BlockSpec((1, f, d), lambda i, be, na: (be[i], 0, 0))],
        out_specs=pl.BlockSpec((MOE_BLOCK, d), lambda i, be, na: (i, 0)),
    )
    return pl.pallas_call(
        _expert_kernel,
        out_shape=jax.ShapeDtypeStruct((p, d), F32),
        grid_spec=grid_spec,
        compiler_params=_params(("arbitrary",)),
        name="moe_experts",
    )(blk_e, n_active, xs, w1, w3, w2)


def _combine_kernel(rows_ref, next_rows_ref, x_ref, gt_ref, g_ref, b_ref, ys_hbm, o_ref, ob_ref, buf, sems, *, tm):
    i = pl.program_id(0)
    n = pl.num_programs(0)

    def row_copy(par, r, slot, src_row):
        return pltpu.make_async_copy(ys_hbm.at[pl.ds(src_row, 1), :], buf.at[par, slot, pl.ds(r, 1), :], sems.at[par])

    def gather(par, idx_ref):
        def issue(r, c):
            row_copy(par, r, 0, idx_ref[0, r]).start()
            row_copy(par, r, 1, idx_ref[1, r]).start()
            return c
        lax.fori_loop(0, tm, issue, 0, unroll=DMA_UNROLL)

    @pl.when(i == 0)
    def _():
        gather(0, rows_ref)

    @pl.when(i + 1 < n)
    def _():
        gather((i + 1) % 2, next_rows_ref)

    par = i % 2

    def drain(r, c):
        row_copy(par, 0, 0, 0).wait()
        row_copy(par, 0, 0, 0).wait()
        return c

    lax.fori_loop(0, tm, drain, 0, unroll=DMA_UNROLL)
    gt = gt_ref[...]
    ffn = gt[:, ROW_GATE:ROW_GATE + 1] * buf[par, 0] + gt[:, ROW_GATE + 1:ROW_GATE + 2] * buf[par, 1]
    y = _layer_norm(DEEPNORM_ALPHA * x_ref[...] + ffn, g_ref[...], b_ref[...])
    o_ref[...] = y
    ob_ref[...] = y.astype(ob_ref.dtype)


def _combine_ln(x, ys, rows, gt, g, b):
    t, d = x.shape
    tm = _tile(t, COMBINE_TM)
    n = t // tm
    row = pl.BlockSpec((tm, d), lambda i: (i, 0))
    vec = pl.BlockSpec((1, d), lambda i: (0, 0))
    return pl.pallas_call(
        functools.partial(_combine_kernel, tm=tm),
        out_shape=(jax.ShapeDtypeStruct((t, d), F32), jax.ShapeDtypeStruct((t, d), BF16)),
        grid=(n,),
        in_specs=[pl.BlockSpec((8, tm), lambda i: (0, i), memory_space=pltpu.SMEM),
                  pl.BlockSpec((8, tm), lambda i: (0, jnp.minimum(i + 1, n - 1)), memory_space=pltpu.SMEM),
                  row,
                  pl.BlockSpec((tm, LANE), lambda i: (i, 0)),
                  vec, vec,
                  pl.BlockSpec(memory_space=pl.ANY)],
        out_specs=(row, row),
        scratch_shapes=[pltpu.VMEM((2, TOP_K, tm, d), F32), pltpu.SemaphoreType.DMA((2,))],
        compiler_params=_params(("arbitrary",)),
        name="moe_combine_ln",
    )(rows, rows, x, gt, g, b, ys)


def _rope_tables(seq):
    def tab(pos, dim):
        inv = ROPE_THETA ** (-jnp.arange(0, dim, 2, dtype=F32) / dim)
        ang = pos.astype(F32)[:, None] * inv[None, :]
        return jnp.cos(ang), jnp.sin(ang)

    pos = jnp.arange(seq)
    cos1, sin1 = tab(pos, HEAD_DIM)
    cos_r, sin_r = tab(pos // GRID_W, HEAD_DIM // 2)
    cos_c, sin_c = tab(pos % GRID_W, HEAD_DIM // 2)
    z = jnp.zeros_like(sin_r)
    c1 = jnp.concatenate([cos1, cos1], axis=1)
    s1 = jnp.concatenate([-sin1, sin1], axis=1)
    cax = jnp.concatenate([cos_r, cos_r, cos_c, cos_c], axis=1)
    sa = jnp.concatenate([-sin_r, z, -sin_c, z], axis=1)
    sb = jnp.concatenate([z, sin_r, z, sin_c], axis=1)
    return c1, s1, cax, sa, sb


def _moe(x1, lw, shared):
    t, d = x1.shape
    er, gt, cnt = _router(x1, shared["wr_hi"], shared["wr_lo"], shared["r_bias"])
    counts = cnt[:, 0].astype(I32)
    padded = (counts + MOE_BLOCK - 1) // MOE_BLOCK * MOE_BLOCK
    pad_end = jnp.cumsum(padded)
    pad_start = (pad_end - padded).astype(I32)
    n_rows = t * TOP_K + N_EXPERTS * MOE_BLOCK
    nb = n_rows // MOE_BLOCK
    blk_e = jnp.minimum(jnp.searchsorted(pad_end, jnp.arange(nb, dtype=I32) * MOE_BLOCK, side="right"),
                        N_EXPERTS - 1).astype(I32)
    n_active = (pad_end[-1:] // MOE_BLOCK).astype(I32)
    rows = _slot_rows(pad_start, er)
    xs = _dispatch(x1, rows, (pad_start + counts).astype(I32), pad_end.astype(I32), n_rows)
    ys = _experts(xs, blk_e, n_active, lw["w1"], lw["w3"], lw["w2"])
    return _combine_ln(x1, ys, rows, gt, lw["ln2_g"], lw["ln2_b"])


def _trunk(x, layers, shared):
    b, s, d = x.shape
    t = b * s
    tables = _rope_tables(s)
    xf = x.reshape(t, d)
    xb = xf.astype(BF16)
    for l, lw in enumerate(layers):
        gains = lw["qk_gains"]
        gates = _gate_proj(xb, lw["w_gate"], lw["b_gate"])
        a_parts = [_attn_a(_project(xb, lw["w_a"][g], tables, gains, s, A_TN, A_TILE_KINDS, dil, f"proj_a{g}"))
                   for g, (_, dil) in enumerate(A_CONFIGS)]
        (o0, l0), (o1, l1), (o2, l2) = a_parts
        oa = _merge_a(o0.reshape(t, A_GROUP_COLS), l0.reshape(t, LANE), o1, l1, o2, l2, s)
        qkv_b = _project(xb, lw["w_b"], tables, gains, s, B_TN, B_TILE_KINDS, 1, "proj_b")
        ob = _attn_b(qkv_b.reshape(b, s, B_COLS))
        lambda_init = 0.8 - 0.6 * float(np.exp(-0.3 * l))
        qkv_c = _project(xb, lw["w_c"], tables, gains, s, C_TN, C_TILE_KINDS, 1, "proj_c")
        oc = _attn_c(qkv_c.reshape(b, s, C_COLS), lw["lam_rows"], lw["subln_g"], lambda_init)
        merged = _branch_proj(oa, ob, oc, gates, lw["w_proj_a"], lw["w_proj_b"], lw["w_proj_c"])
        x1 = _out_proj_ln(merged, lw["w_out"], xf, lw["ln1_g"], lw["ln1_b"])
        xf, xb = _moe(x1, lw, shared)
    return xf.reshape(b, s, d)


def kernel(x_prompt, x_sample, w_in, q_norm_g, k_norm_g, lambda_q1, lambda_k1, lambda_q2, lambda_k2, subln_g,
           w_gate, b_gate, w_proj_a, w_proj_b, w_proj_c, w_out, ln1_g, ln1_b, w_router, router_bias,
           w1, w3, w2, ln2_g, ln2_b):
    assert [dil for _, dil in A_CONFIGS] == [1, 4, 16] and all(w // (2 * dil) == A_RADIUS for w, dil in A_CONFIGS)
    depth = w_in.shape[0]
    layers = []
    part = A_HEADS * HEAD_DIM
    for l in range(depth):
        w_l = w_in[l].astype(BF16)
        w_a = [jnp.concatenate([w_l[:, p * part + g * A_GROUP_COLS: p * part + (g + 1) * A_GROUP_COLS] for p in range(3)],
                               axis=1) for g in range(len(A_CONFIGS))]
        layers.append(dict(
            w_a=w_a, w_b=w_l[:, A_COLS:A_COLS + B_COLS], w_c=w_l[:, A_COLS + B_COLS:],
            qk_gains=jnp.stack([q_norm_g[l], k_norm_g[l]]).astype(F32),
            lam_rows=jnp.stack([lambda_q1[l], lambda_k1[l], lambda_q2[l], lambda_k2[l]]).astype(F32),
            subln_g=subln_g[l].reshape(1, -1).astype(F32),
            w_gate=w_gate[l].astype(BF16),
            b_gate=b_gate[l].reshape(1, -1).astype(F32),
            w_proj_a=w_proj_a[l].astype(BF16),
            w_proj_b=w_proj_b[l].astype(BF16),
            w_proj_c=w_proj_c[l].astype(BF16),
            w_out=w_out[l].astype(BF16),
            ln1_g=ln1_g[l].reshape(1, -1).astype(F32),
            ln1_b=ln1_b[l].reshape(1, -1).astype(F32),
            w1=w1[l].astype(BF16), w3=w3[l].astype(BF16), w2=w2[l].astype(BF16),
            ln2_g=ln2_g[l].reshape(1, -1).astype(F32),
            ln2_b=ln2_b[l].reshape(1, -1).astype(F32),
        ))
    wr_t = w_router.astype(F32).T
    wr_hi = wr_t.astype(BF16)
    wr_lo = (wr_t - wr_hi.astype(F32)).astype(BF16)
    shared = dict(wr_hi=wr_hi, wr_lo=wr_lo, r_bias=router_bias.astype(F32).reshape(-1, 1))
    return (_trunk(x_prompt, layers, shared), _trunk(x_sample, layers, shared))
```

```python
import functools

import numpy as np
import jax
import jax.numpy as jnp
from jax import lax
from jax.experimental import pallas as pl
from jax.experimental.pallas import tpu as pltpu

F32 = jnp.float32
BF16 = jnp.bfloat16
I32 = jnp.int32

DEPTH = 2
HEAD_DIM = 128
GRID_W = 64
ROPE_THETA = 10000.0
A_CONFIGS = ((128, 1), (512, 4), (2048, 16))
A_HEADS_PER_GROUP = 6
A_HEADS = A_HEADS_PER_GROUP * len(A_CONFIGS)
A_GROUP_COLS = A_HEADS_PER_GROUP * HEAD_DIM
A_OUT = A_GROUP_COLS
B_Q_HEADS = 8
B_KV_HEADS = 2
B_GROUP = B_Q_HEADS // B_KV_HEADS
B_OUT = B_Q_HEADS * HEAD_DIM
C_HEADS = 4
C_OUT = C_HEADS * 2 * HEAD_DIM
N_BRANCH = 3
N_EXPERTS = 16
N_EXPERT_GROUPS = 4
EXPERTS_PER_GROUP = N_EXPERTS // N_EXPERT_GROUPS
TOP_K = 2
DEEPNORM_ALPHA = (2 * DEPTH) ** 0.25
LN_EPS = 1e-5
RMS_EPS = 1e-6
NEG_INF = -1e30
LOG2E = 1.4426950408889634
LN2 = 0.6931471805599453
ATTN_SCALE = HEAD_DIM ** -0.5 * LOG2E
A_RADIUS = 64

HPG = A_HEADS_PER_GROUP
B_Q0, B_K0, B_V0 = 0, B_Q_HEADS, B_Q_HEADS + B_KV_HEADS
B_COLS = (B_Q_HEADS + 2 * B_KV_HEADS) * HEAD_DIM
C_COLS = 6 * C_HEADS * HEAD_DIM
A_COLS = 3 * A_HEADS * HEAD_DIM
K_PLAIN, K_ROPE, K_ROPE_Q, K_AXIAL_Q, K_AXIAL_K = range(5)

LANE = 128
QKV_TM = 1024
A_TN, B_TN, C_TN = A_GROUP_COLS, 4 * LANE, 8 * LANE
PROJ_CHUNK_HEADS = 2
GATE_TM, GATE_TN = 1024, 1024
PROJ_TM, PROJ_TN = 1024, 512
OUT_TM = 512
ROUTER_TM = 512
MOE_BLOCK = 256
COMBINE_TM = 256
DISPATCH_TM = 256
SLOT_TM = 2048
DMA_UNROLL = 8
ATTN_TQ = 256
A_UNITS_PER_STEP = 2
C_HEADS_PER_STEP = 2
MERGE_TM = 1024
VMEM_LIMIT = 56 * 1024 * 1024
ROW_E, ROW_RANK, ROW_GATE = 0, 2, 4


def _params(sem, vmem=VMEM_LIMIT):
    return pltpu.CompilerParams(dimension_semantics=sem, vmem_limit_bytes=vmem)


def _tile(n, t):
    t = min(n, t)
    assert n % t == 0, (n, t)
    return t


def _rope_head(xh, c1_ref, s1_ref):
    return xh * c1_ref[...] + pltpu.roll(xh, 64, 1) * s1_ref[...]


def _head_epilogue(kind, xh, c1_ref, s1_ref, cax_ref, sa_ref, sb_ref, g_ref):
    if kind == K_PLAIN:
        return xh
    if kind in (K_ROPE, K_ROPE_Q):
        y = _rope_head(xh, c1_ref, s1_ref)
    else:
        g = g_ref[0:1, :] if kind == K_AXIAL_Q else g_ref[1:2, :]
        r = xh * lax.rsqrt(jnp.mean(xh * xh, axis=-1, keepdims=True) + RMS_EPS) * g
        y = r * cax_ref[...] + pltpu.roll(r, 96, 1) * sa_ref[...] + pltpu.roll(r, 32, 1) * sb_ref[...]
    return y * ATTN_SCALE if kind in (K_ROPE_Q, K_AXIAL_Q) else y


def _project_kernel(x_ref, w_ref, c1_ref, s1_ref, cax_ref, sa_ref, sb_ref, g_ref, o_ref, *scratch,
                    tile_kinds, dilation):
    j = pl.program_id(1)
    patterns = {}
    for idx, pattern in enumerate(tile_kinds):
        patterns.setdefault(pattern, []).append(idx)

    def emit(pattern):
        axial = any(kind in (K_AXIAL_Q, K_AXIAL_K) for kind in pattern)
        chunk = len(pattern) if axial else PROJ_CHUNK_HEADS
        for c0 in range(0, len(pattern), chunk):
            acc = jnp.dot(x_ref[...], w_ref[:, c0 * LANE:(c0 + chunk) * LANE], preferred_element_type=F32)
            for hh in range(chunk):
                h = c0 + hh
                cols = slice(h * LANE, (h + 1) * LANE)
                y = _head_epilogue(pattern[h], acc[:, hh * LANE:(hh + 1) * LANE],
                                   c1_ref, s1_ref, cax_ref, sa_ref, sb_ref, g_ref)
                if dilation == 1:
                    o_ref[0, 0, :, cols] = y.astype(o_ref.dtype)
                else:
                    scratch[0][h] = y

    for pattern, idxs in patterns.items():
        cond = functools.reduce(jnp.logical_or, [j == i for i in idxs])
        pl.when(cond)(functools.partial(emit, pattern))

    if dilation > 1:
        y_ref = scratch[0]
        rows = y_ref.shape[1] // dilation
        for c in range(dilation):
            for h in range(y_ref.shape[0]):
                o_ref[0, c, :, h * LANE:(h + 1) * LANE] = (
                    y_ref[h, pl.ds(c, rows, stride=dilation), :].astype(o_ref.dtype))


def _project(xb, w, tables, gains, seq, tn, tile_kinds, dilation, name):
    t, d = xb.shape
    n = w.shape[1]
    assert n == tn * len(tile_kinds) and all(len(p) * LANE == tn for p in tile_kinds)
    tm = _tile(seq, QKV_TM)
    nsb = seq // tm
    tab_spec = pl.BlockSpec((tm, LANE), lambda i, j: (i % nsb, 0))
    scratch = [] if dilation == 1 else [pltpu.VMEM((tn // LANE, tm, LANE), F32)]
    return pl.pallas_call(
        functools.partial(_project_kernel, tile_kinds=tuple(tile_kinds), dilation=dilation),
        out_shape=jax.ShapeDtypeStruct((t // seq, dilation, seq // dilation, n), BF16),
        grid=(t // tm, n // tn),
        in_specs=[pl.BlockSpec((tm, d), lambda i, j: (i, 0)),
                  pl.BlockSpec((d, tn), lambda i, j: (0, j)),
                  tab_spec, tab_spec, tab_spec, tab_spec, tab_spec,
                  pl.BlockSpec((2, LANE), lambda i, j: (0, 0))],
        out_specs=pl.BlockSpec((1, dilation, tm // dilation, tn), lambda i, j: (i // nsb, 0, i % nsb, j)),
        scratch_shapes=scratch,
        compiler_params=_params(("parallel", "arbitrary")),
        name=name,
    )(xb, w, *tables, gains)


A_TILE_KINDS = ((K_ROPE_Q,) * HPG, (K_ROPE,) * HPG, (K_PLAIN,) * HPG)
B_TILE_KINDS = ((K_AXIAL_Q,) * 4, (K_AXIAL_Q,) * 4, (K_AXIAL_K,) * B_KV_HEADS + (K_PLAIN,) * B_KV_HEADS)
C_TILE_KINDS = ((K_ROPE_Q,) * 8, (K_ROPE,) * 8, (K_PLAIN,) * 8)


def _gate_kernel(x_ref, w_ref, b_ref, o_ref):
    acc = jnp.dot(x_ref[...], w_ref[...], preferred_element_type=F32) + b_ref[...]
    o_ref[...] = jax.nn.sigmoid(acc).astype(o_ref.dtype)


def _gate_proj(xb, w, b):
    t, d = xb.shape
    n = w.shape[1]
    tm, tn = _tile(t, GATE_TM), _tile(n, GATE_TN)
    return pl.pallas_call(
        _gate_kernel,
        out_shape=jax.ShapeDtypeStruct((t, n), BF16),
        grid=(t // tm, n // tn),
        in_specs=[pl.BlockSpec((tm, d), lambda i, j: (i, 0)),
                  pl.BlockSpec((d, tn), lambda i, j: (0, j)),
                  pl.BlockSpec((1, tn), lambda i, j: (0, j))],
        out_specs=pl.BlockSpec((tm, tn), lambda i, j: (i, j)),
        compiler_params=_params(("parallel", "arbitrary")),
        name="gate_proj",
    )(xb, w, b)


def _attn_a_kernel(q_ref, k_ref, v_ref, o_ref, lse_ref, *, tq, win, sub_len):
    n_res, n_qb = q_ref.shape[1], q_ref.shape[2] // tq
    lane = lax.broadcasted_iota(I32, (tq, LANE), 1)
    for qb in range(n_qb):
        q0 = (pl.program_id(2) * n_qb + qb) * tq
        start = jnp.clip(q0 - A_RADIUS, 0, sub_len - win)
        if win % 16 == 0 and tq % 64 == 0:
            start = pl.multiple_of(start, 16)
        q_pos = q0 + lax.broadcasted_iota(I32, (tq, win), 0)
        k_pos = start + lax.broadcasted_iota(I32, (tq, win), 1)
        bias = jnp.where(jnp.abs(k_pos - q_pos) <= A_RADIUS, 0.0, NEG_INF).astype(F32)
        rows = slice(qb * tq, (qb + 1) * tq)
        for c in range(n_res):
            lse_all = jnp.zeros((tq, LANE), F32)
            for h in range(HPG):
                cols = slice(h * LANE, (h + 1) * LANE)
                q = q_ref[0, c, rows, cols]
                k = k_ref[0, c, pl.ds(start, win), cols]
                v = v_ref[0, c, pl.ds(start, win), cols]
                s = lax.dot_general(q, k, (((1,), (1,)), ((), ())), preferred_element_type=F32) + bias
                m = jnp.max(s, axis=-1, keepdims=True)
                p = jnp.exp2(s - m)
                den = jnp.sum(p, axis=-1, keepdims=True)
                o = jnp.dot(p.astype(BF16), v, preferred_element_type=F32) / den
                o_ref[0, c, rows, cols] = o.astype(o_ref.dtype)
                lse_all = jnp.where(lane == h, (m + jnp.log2(den)) * LN2, lse_all)
            lse_ref[0, c, rows, :] = lse_all


def _attn_a(qkv4):
    b, r, sub_len, _ = qkv4.shape
    tq = _tile(sub_len, ATTN_TQ)
    win = min(tq + 2 * A_RADIUS, sub_len)
    n_qb = min(A_UNITS_PER_STEP, sub_len // tq)
    n_res = min(max(A_UNITS_PER_STEP // n_qb, 1), r)
    assert (sub_len // tq) % n_qb == 0 and r % n_res == 0
    return pl.pallas_call(
        functools.partial(_attn_a_kernel, tq=tq, win=win, sub_len=sub_len),
        out_shape=(jax.ShapeDtypeStruct((b, r, sub_len, A_GROUP_COLS), BF16),
                   jax.ShapeDtypeStruct((b, r, sub_len, LANE), F32)),
        grid=(b, r // n_res, sub_len // (tq * n_qb)),
        in_specs=[pl.BlockSpec((1, n_res, tq * n_qb, A_GROUP_COLS), lambda bi, c, qi: (bi, c, qi, 0)),
                  pl.BlockSpec((1, n_res, sub_len, A_GROUP_COLS), lambda bi, c, qi: (bi, c, 0, 1)),
                  pl.BlockSpec((1, n_res, sub_len, A_GROUP_COLS), lambda bi, c, qi: (bi, c, 0, 2))],
        out_specs=(pl.BlockSpec((1, n_res, tq * n_qb, A_GROUP_COLS), lambda bi, c, qi: (bi, c, qi, 0)),
                   pl.BlockSpec((1, n_res, tq * n_qb, LANE), lambda bi, c, qi: (bi, c, qi, 0))),
        compiler_params=_params(("parallel", "parallel", "arbitrary")),
        name=f"attn_a_r{r}",
    )(qkv4, qkv4, qkv4)


def _merge_a_kernel(o0_ref, l0_ref, o1_ref, l1_ref, o2_ref, l2_ref, o_ref, so1, sl1, so2, sl2):
    def to_token_order(src_ref, dst_ref):
        r, rows = src_ref.shape[1], src_ref.shape[2]
        for c in range(r):
            for h in range(dst_ref.shape[0]):
                dst_ref[h, pl.ds(c, rows, stride=r), :] = src_ref[0, c, :, h * LANE:(h + 1) * LANE].astype(F32)

    to_token_order(o1_ref, so1)
    to_token_order(l1_ref, sl1)
    to_token_order(o2_ref, so2)
    to_token_order(l2_ref, sl2)
    l0, l1, l2 = l0_ref[...], sl1[0], sl2[0]
    m = jnp.maximum(jnp.maximum(l0, l1), l2)
    e0, e1, e2 = jnp.exp(l0 - m), jnp.exp(l1 - m), jnp.exp(l2 - m)
    den = e0 + e1 + e2
    w0, w1, w2 = e0 / den, e1 / den, e2 / den
    for h in range(HPG):
        cols = slice(h * LANE, (h + 1) * LANE)
        acc = (o0_ref[:, cols].astype(F32) * w0[:, h:h + 1]
               + so1[h] * w1[:, h:h + 1]
               + so2[h] * w2[:, h:h + 1])
        o_ref[:, cols] = acc.astype(o_ref.dtype)


def _merge_a(o0, l0, o1, l1, o2, l2, seq):
    t = o0.shape[0]
    tm = _tile(seq, MERGE_TM)
    nsb = seq // tm

    def res_spec(arr):
        r, n = arr.shape[1], arr.shape[3]
        return pl.BlockSpec((1, r, tm // r, n), lambda i: (i // nsb, 0, i % nsb, 0))

    return pl.pallas_call(
        _merge_a_kernel,
        out_shape=jax.ShapeDtypeStruct((t, A_OUT), BF16),
        grid=(t // tm,),
        in_specs=[pl.BlockSpec((tm, A_GROUP_COLS), lambda i: (i, 0)), pl.BlockSpec((tm, LANE), lambda i: (i, 0)),
                  res_spec(o1), res_spec(l1), res_spec(o2), res_spec(l2)],
        out_specs=pl.BlockSpec((tm, A_OUT), lambda i: (i, 0)),
        scratch_shapes=[pltpu.VMEM((HPG, tm, LANE), F32), pltpu.VMEM((1, tm, LANE), F32),
                        pltpu.VMEM((HPG, tm, LANE), F32), pltpu.VMEM((1, tm, LANE), F32)],
        compiler_params=_params(("parallel",)),
        name="merge_a",
    )(o0, l0, o1, l1, o2, l2)


def _softmax_pv(q, k, v):
    s = lax.dot_general(q, k, (((1,), (1,)), ((), ())), preferred_element_type=F32)
    m = jnp.max(s, axis=-1, keepdims=True)
    p = jnp.exp2(s - m)
    den = jnp.sum(p, axis=-1, keepdims=True)
    return jnp.dot(p.astype(BF16), v, preferred_element_type=F32) / den


def _attn_b_kernel(q_ref, k_ref, v_ref, o_ref):
    for h in range(B_KV_HEADS):
        k = k_ref[0, :, h * LANE:(h + 1) * LANE]
        v = v_ref[0, :, h * LANE:(h + 1) * LANE]
        for g in range(B_GROUP):
            cols = slice((h * B_GROUP + g) * LANE, (h * B_GROUP + g + 1) * LANE)
            o_ref[0, :, cols] = _softmax_pv(q_ref[0, :, cols], k, v).astype(o_ref.dtype)


def _attn_b(qkv3):
    b, s, _ = qkv3.shape
    tq = _tile(s, ATTN_TQ)
    kv_cols = B_KV_HEADS * LANE
    o = pl.pallas_call(
        _attn_b_kernel,
        out_shape=jax.ShapeDtypeStruct((b, s, B_OUT), BF16),
        grid=(b, s // tq),
        in_specs=[pl.BlockSpec((1, tq, B_OUT), lambda bi, qi: (bi, qi, 0)),
                  pl.BlockSpec((1, s, kv_cols), lambda bi, qi: (bi, 0, B_K0 * LANE // kv_cols)),
                  pl.BlockSpec((1, s, kv_cols), lambda bi, qi: (bi, 0, B_V0 * LANE // kv_cols))],
        out_specs=pl.BlockSpec((1, tq, B_OUT), lambda bi, qi: (bi, qi, 0)),
        compiler_params=_params(("parallel", "arbitrary")),
        name="attn_b",
    )(qkv3, qkv3, qkv3)
    return o.reshape(b * s, B_OUT)


def _attn_c_kernel(q_ref, k_ref, v_ref, lam_ref, g_ref, o_ref, *, lambda_init):
    lam_p = lam_ref[...].astype(F32)
    lam = (jnp.exp(jnp.sum(lam_p[0:1] * lam_p[1:2], axis=-1, keepdims=True))
           - jnp.exp(jnp.sum(lam_p[2:3] * lam_p[3:4], axis=-1, keepdims=True)) + lambda_init)
    for hh in range(C_HEADS_PER_STEP):
        c0 = hh * 2 * LANE
        first, second = slice(c0, c0 + LANE), slice(c0 + LANE, c0 + 2 * LANE)
        v = v_ref[0, :, c0:c0 + 2 * LANE]
        o = (_softmax_pv(q_ref[0, :, first], k_ref[0, :, first], v)
             - lam * _softmax_pv(q_ref[0, :, second], k_ref[0, :, second], v))
        o = o * lax.rsqrt(jnp.mean(o * o, axis=-1, keepdims=True) + RMS_EPS) * g_ref[...].astype(F32)
        o_ref[0, :, c0:c0 + 2 * LANE] = (o * (1.0 - lambda_init)).astype(o_ref.dtype)


def _attn_c(qkv3, lam_rows, subln_g, lambda_init):
    b, s, _ = qkv3.shape
    tq = _tile(s, ATTN_TQ)
    cols = C_HEADS_PER_STEP * 2 * LANE
    steps = C_HEADS // C_HEADS_PER_STEP
    o = pl.pallas_call(
        functools.partial(_attn_c_kernel, lambda_init=lambda_init),
        out_shape=jax.ShapeDtypeStruct((b, s, C_OUT), BF16),
        grid=(b, steps, s // tq),
        in_specs=[pl.BlockSpec((1, tq, cols), lambda bi, h, qi: (bi, qi, h)),
                  pl.BlockSpec((1, s, cols), lambda bi, h, qi: (bi, 0, steps + h)),
                  pl.BlockSpec((1, s, cols), lambda bi, h, qi: (bi, 0, 2 * steps + h)),
                  pl.BlockSpec((4, LANE), lambda bi, h, qi: (0, 0)),
                  pl.BlockSpec((1, 2 * LANE), lambda bi, h, qi: (0, 0))],
        out_specs=pl.BlockSpec((1, tq, cols), lambda bi, h, qi: (bi, qi, h)),
        compiler_params=_params(("parallel", "parallel", "arbitrary")),
        name="attn_c",
    )(qkv3, qkv3, qkv3, lam_rows, subln_g)
    return o.reshape(b * s, C_OUT)


def _branch_proj_kernel(oa_ref, ob_ref, oc_ref, g0_ref, g1_ref, g2_ref, wa_ref, wb_ref, wc_ref, o_ref):
    acc = g0_ref[...].astype(F32) * jnp.dot(oa_ref[...], wa_ref[...], preferred_element_type=F32)
    acc = acc + g1_ref[...].astype(F32) * jnp.dot(ob_ref[...], wb_ref[...], preferred_element_type=F32)
    acc = acc + g2_ref[...].astype(F32) * jnp.dot(oc_ref[...], wc_ref[...], preferred_element_type=F32)
    o_ref[...] = acc.astype(o_ref.dtype)


def _branch_proj(oa, ob, oc, gates, wa, wb, wc):
    t = oa.shape[0]
    d = wa.shape[1]
    tm, tn = _tile(t, PROJ_TM), _tile(d, PROJ_TN)
    nb = d // tn
    gspecs = [pl.BlockSpec((tm, tn), functools.partial(lambda i, j, br: (i, br * nb + j), br=br)) for br in range(N_BRANCH)]
    return pl.pallas_call(
        _branch_proj_kernel,
        out_shape=jax.ShapeDtypeStruct((t, d), BF16),
        grid=(t // tm, nb),
        in_specs=[pl.BlockSpec((tm, A_OUT), lambda i, j: (i, 0)),
                  pl.BlockSpec((tm, B_OUT), lambda i, j: (i, 0)),
                  pl.BlockSpec((tm, C_OUT), lambda i, j: (i, 0))] + gspecs +
                 [pl.BlockSpec((A_OUT, tn), lambda i, j: (0, j)),
                  pl.BlockSpec((B_OUT, tn), lambda i, j: (0, j)),
                  pl.BlockSpec((C_OUT, tn), lambda i, j: (0, j))],
        out_specs=pl.BlockSpec((tm, tn), lambda i, j: (i, j)),
        compiler_params=_params(("parallel", "arbitrary")),
        name="branch_proj",
    )(oa, ob, oc, gates, gates, gates, wa, wb, wc)


def _layer_norm(z, g, b):
    mu = jnp.mean(z, axis=-1, keepdims=True)
    zc = z - mu
    var = jnp.mean(zc * zc, axis=-1, keepdims=True)
    return zc * lax.rsqrt(var + LN_EPS) * g + b


def _out_ln_kernel(m_ref, w_ref, x_ref, g_ref, b_ref, o_ref):
    mix = jnp.dot(m_ref[...], w_ref[...], preferred_element_type=F32)
    o_ref[...] = _layer_norm(DEEPNORM_ALPHA * x_ref[...] + mix, g_ref[...], b_ref[...])


def _out_proj_ln(merged, w_out, x, g, b):
    t, d = x.shape
    tm = _tile(t, OUT_TM)
    row = pl.BlockSpec((tm, d), lambda i: (i, 0))
    vec = pl.BlockSpec((1, d), lambda i: (0, 0))
    return pl.pallas_call(
        _out_ln_kernel,
        out_shape=jax.ShapeDtypeStruct((t, d), F32),
        grid=(t // tm,),
        in_specs=[row, pl.BlockSpec((d, d), lambda i: (0, 0)), row, vec, vec],
        out_specs=row,
        compiler_params=_params(("parallel",)),
        name="out_proj_ln",
    )(merged, w_out, x, g, b)


def _router_kernel(x_ref, whi_ref, wlo_ref, bias_ref, er_ref, gt_ref, cnt_ref, carry_ref, *, tm):
    @pl.when(pl.program_id(0) == 0)
    def _():
        carry_ref[...] = jnp.zeros_like(carry_ref)

    x = x_ref[...]
    x_hi = x.astype(BF16)
    x_lo = (x - x_hi.astype(F32)).astype(BF16)
    nt = (((1,), (1,)), ((), ()))
    logits = (lax.dot_general(whi_ref[...], x_hi, nt, preferred_element_type=F32)
              + lax.dot_general(wlo_ref[...], x_hi, nt, preferred_element_type=F32)
              + lax.dot_general(whi_ref[...], x_lo, nt, preferred_element_type=F32))
    scores = jax.nn.sigmoid(logits)
    biased = scores + bias_ref[...]

    def row(a, e):
        return a[e:e + 1, :]

    gscore = []
    for g in range(N_EXPERT_GROUPS):
        a, b, c, d = (row(biased, g * EXPERTS_PER_GROUP + i) for i in range(EXPERTS_PER_GROUP))
        hi1, lo1, hi2, lo2 = jnp.maximum(a, b), jnp.minimum(a, b), jnp.maximum(c, d), jnp.minimum(c, d)
        gscore.append(jnp.maximum(hi1, hi2) + jnp.maximum(jnp.minimum(hi1, hi2), jnp.maximum(lo1, lo2)))
    gsel = jnp.zeros((1, tm), I32)
    best = gscore[0]
    for g in range(1, N_EXPERT_GROUPS):
        better = gscore[g] > best
        gsel = jnp.where(better, g, gsel)
        best = jnp.where(better, gscore[g], best)

    def pick(a, i):
        out = row(a, i)
        for g in range(1, N_EXPERT_GROUPS):
            out = jnp.where(gsel == g, row(a, g * EXPERTS_PER_GROUP + i), out)
        return out

    bv = [pick(biased, i) for i in range(EXPERTS_PER_GROUP)]
    sv = [pick(scores, i) for i in range(EXPERTS_PER_GROUP)]
    i0 = jnp.zeros((1, tm), I32)
    b0 = bv[0]
    for i in range(1, EXPERTS_PER_GROUP):
        better = bv[i] > b0
        i0 = jnp.where(better, i, i0)
        b0 = jnp.where(better, bv[i], b0)
    i1 = jnp.full((1, tm), -1, I32)
    b1 = jnp.full((1, tm), -jnp.inf, F32)
    for i in range(EXPERTS_PER_GROUP):
        better = (i0 != i) & ((i1 < 0) | (bv[i] > b1))
        i1 = jnp.where(better, i, i1)
        b1 = jnp.where(better, bv[i], b1)
    s0 = jnp.zeros((1, tm), F32)
    s1 = jnp.zeros((1, tm), F32)
    for i in range(EXPERTS_PER_GROUP):
        s0 = jnp.where(i0 == i, sv[i], s0)
        s1 = jnp.where(i1 == i, sv[i], s1)
    e0 = gsel * EXPERTS_PER_GROUP + i0
    e1 = gsel * EXPERTS_PER_GROUP + i1
    den = s0 + s1
    g0, g1 = s0 / den, s1 / den

    erow = lax.broadcasted_iota(I32, (N_EXPERTS, tm), 0)
    member = ((erow == e0) | (erow == e1))
    tri = (lax.broadcasted_iota(I32, (tm, tm), 0) < lax.broadcasted_iota(I32, (tm, tm), 1))
    prefix = jnp.dot(member.astype(BF16), tri.astype(BF16), preferred_element_type=F32) + carry_ref[:, 0:1]
    r0 = jnp.sum(jnp.where(erow == e0, prefix, 0.0), axis=0, keepdims=True)
    r1 = jnp.sum(jnp.where(erow == e1, prefix, 0.0), axis=0, keepdims=True)
    carry_ref[...] = carry_ref[...] + jnp.sum(member.astype(F32), axis=1, keepdims=True)
    cnt_ref[...] = carry_ref[...]

    zi = jnp.zeros((1, tm), I32)
    er_ref[...] = jnp.concatenate([e0, e1, r0.astype(I32), r1.astype(I32), zi, zi, zi, zi], axis=0)
    zf = jnp.zeros((1, tm), F32)
    rec = jnp.concatenate([zf, zf, zf, zf, g0, g1, zf, zf, jnp.zeros((LANE - 8, tm), F32)], axis=0)
    gt_ref[...] = rec.T


def _router(x, w_hi_t, w_lo_t, bias_col):
    t, d = x.shape
    tm = _tile(t, ROUTER_TM)
    return pl.pallas_call(
        functools.partial(_router_kernel, tm=tm),
        out_shape=(jax.ShapeDtypeStruct((8, t), I32), jax.ShapeDtypeStruct((t, LANE), F32),
                   jax.ShapeDtypeStruct((N_EXPERTS, LANE), F32)),
        grid=(t // tm,),
        in_specs=[pl.BlockSpec((tm, d), lambda i: (i, 0)),
                  pl.BlockSpec((N_EXPERTS, d), lambda i: (0, 0)),
                  pl.BlockSpec((N_EXPERTS, d), lambda i: (0, 0)),
                  pl.BlockSpec((N_EXPERTS, 1), lambda i: (0, 0))],
        out_specs=(pl.BlockSpec((8, tm), lambda i: (0, i)),
                   pl.BlockSpec((tm, LANE), lambda i: (i, 0)),
                   pl.BlockSpec((N_EXPERTS, LANE), lambda i: (0, 0))),
        scratch_shapes=[pltpu.VMEM((N_EXPERTS, LANE), F32)],
        compiler_params=_params(("arbitrary",)),
        name="router",
    )(x, w_hi_t, w_lo_t, bias_col)


def _slot_rows_kernel(pad_start_ref, er_ref, o_ref):
    e = er_ref[ROW_E:ROW_E + TOP_K, :]
    base = jnp.zeros_like(e)
    for x in range(N_EXPERTS):
        base = jnp.where(e == x, pad_start_ref[x], base)
    rows = base + er_ref[ROW_RANK:ROW_RANK + TOP_K, :]
    o_ref[...] = jnp.concatenate([rows, jnp.zeros((8 - TOP_K, rows.shape[1]), I32)], axis=0)


def _slot_rows(pad_start, er):
    t = er.shape[1]
    tm = _tile(t, SLOT_TM)
    grid_spec = pltpu.PrefetchScalarGridSpec(
        num_scalar_prefetch=1,
        grid=(t // tm,),
        in_specs=[pl.BlockSpec((8, tm), lambda i, ps: (0, i))],
        out_specs=pl.BlockSpec((8, tm), lambda i, ps: (0, i)),
    )
    return pl.pallas_call(
        _slot_rows_kernel,
        out_shape=jax.ShapeDtypeStruct((8, t), I32),
        grid_spec=grid_spec,
        compiler_params=_params(("arbitrary",)),
        name="moe_slot_rows",
    )(pad_start, er)


def _dispatch_kernel(fill_end_ref, pad_end_ref, rows_ref, x_ref, xs_hbm, zblk, sem, zsem, *, tm):
    def row_copy(r, dst_row):
        return pltpu.make_async_copy(x_ref.at[pl.ds(r, 1), :], xs_hbm.at[pl.ds(dst_row, 1), :], sem)

    def issue(r, c):
        row_copy(r, rows_ref[0, r]).start()
        row_copy(r, rows_ref[1, r]).start()
        return c

    lax.fori_loop(0, tm, issue, 0, unroll=DMA_UNROLL)

    def zero_row(dst_row):
        return pltpu.make_async_copy(zblk.at[pl.ds(0, 1), :], xs_hbm.at[pl.ds(dst_row, 1), :], zsem)

    def zero_block(blk):
        dst = xs_hbm.at[pl.ds(pl.multiple_of(blk * MOE_BLOCK, MOE_BLOCK), MOE_BLOCK), :]
        return pltpu.make_async_copy(zblk, dst, zsem)

    @pl.when(pl.program_id(0) == 0)
    def _():
        zblk[...] = jnp.zeros_like(zblk)
        for e in range(N_EXPERTS):
            lo, hi = fill_end_ref[e], pad_end_ref[e]
            lax.fori_loop(lo, hi, lambda r, c: (zero_row(r).start(), c)[1], 0)
            lax.fori_loop(lo, hi, lambda r, c: (zero_row(0).wait(), c)[1], 0)
        first, last = pad_end_ref[N_EXPERTS - 1] // MOE_BLOCK, xs_hbm.shape[0] // MOE_BLOCK
        lax.fori_loop(first, last, lambda blk, c: (zero_block(blk).start(), c)[1], 0)
        lax.fori_loop(first, last, lambda blk, c: (zero_block(0).wait(), c)[1], 0)

    def drain(r, c):
        row_copy(0, 0).wait()
        row_copy(0, 0).wait()
        return c

    lax.fori_loop(0, tm, drain, 0, unroll=DMA_UNROLL)


def _dispatch(x, rows, fill_end, pad_end, n_rows):
    t, d = x.shape
    tm = _tile(t, DISPATCH_TM)
    grid_spec = pltpu.PrefetchScalarGridSpec(
        num_scalar_prefetch=2,
        grid=(t // tm,),
        in_specs=[pl.BlockSpec((8, tm), lambda i, fe, pe: (0, i), memory_space=pltpu.SMEM),
                  pl.BlockSpec((tm, d), lambda i, fe, pe: (i, 0))],
        out_specs=pl.BlockSpec(memory_space=pl.ANY),
        scratch_shapes=[pltpu.VMEM((MOE_BLOCK, d), F32), pltpu.SemaphoreType.DMA, pltpu.SemaphoreType.DMA],
    )
    return pl.pallas_call(
        functools.partial(_dispatch_kernel, tm=tm),
        out_shape=jax.ShapeDtypeStruct((n_rows, d), F32),
        grid_spec=grid_spec,
        compiler_params=_params(("arbitrary",)),
        name="moe_dispatch",
    )(fill_end, pad_end, rows, x)


def _expert_kernel(blk_e_ref, nact_ref, x_ref, w1_ref, w3_ref, w2_ref, o_ref):
    del blk_e_ref

    @pl.when(pl.program_id(0) < nact_ref[0])
    def _():
        x = x_ref[...].astype(BF16)
        h1 = jnp.dot(x, w1_ref[0], preferred_element_type=F32)
        h3 = jnp.dot(x, w3_ref[0], preferred_element_type=F32)
        h = (h1 * jax.nn.sigmoid(h1) * h3).astype(BF16)
        o_ref[...] = jnp.dot(h, w2_ref[0], preferred_element_type=F32)

    @pl.when(pl.program_id(0) >= nact_ref[0])
    def _():
        o_ref[...] = jnp.zeros_like(o_ref)


def _experts(xs, blk_e, n_active, w1, w3, w2):
    p, d = xs.shape
    f = w1.shape[2]
    nb = p // MOE_BLOCK
    grid_spec = pltpu.PrefetchScalarGridSpec(
        num_scalar_prefetch=2,
        grid=(nb,),
        in_specs=[pl.BlockSpec((MOE_BLOCK, d), lambda i, be, na: (jnp.minimum(i, na[0] - 1), 0)),
                  pl.BlockSpec((1, d, f), lambda i, be, na: (be[i], 0, 0)),
                  pl.BlockSpec((1, d, f), lambda i, be, na: (be[i], 0, 0)),
                  pl.BlockSpec((1, f, d), lambda i, be, na: (be[i], 0, 0))],
        out_specs=pl.BlockSpec((MOE_BLOCK, d), lambda i, be, na: (i, 0)),
    )
    return pl.pallas_call(
        _expert_kernel,
        out_shape=jax.ShapeDtypeStruct((p, d), F32),
        grid_spec=grid_spec,
        compiler_params=_params(("arbitrary",)),
        name="moe_experts",
    )(blk_e, n_active, xs, w1, w3, w2)


def _combine_kernel(rows_ref, next_rows_ref, x_ref, gt_ref, g_ref, b_ref, ys_hbm, o_ref, ob_ref, buf, sems, *, tm):
    i = pl.program_id(0)
    n = pl.num_programs(0)

    def row_copy(par, r, slot, src_row):
        return pltpu.make_async_copy(ys_hbm.at[pl.ds(src_row, 1), :], buf.at[par, slot, pl.ds(r, 1), :], sems.at[par])

    def gather(par, idx_ref):
        def issue(r, c):
            row_copy(par, r, 0, idx_ref[0, r]).start()
            row_copy(par, r, 1, idx_ref[1, r]).start()
            return c
        lax.fori_loop(0, tm, issue, 0, unroll=DMA_UNROLL)

    @pl.when(i == 0)
    def _():
        gather(0, rows_ref)

    @pl.when(i + 1 < n)
    def _():
        gather((i + 1) % 2, next_rows_ref)

    par = i % 2

    def drain(r, c):
        row_copy(par, 0, 0, 0).wait()
        row_copy(par, 0, 0, 0).wait()
        return c

    lax.fori_loop(0, tm, drain, 0, unroll=DMA_UNROLL)
    gt = gt_ref[...]
    ffn = gt[:, ROW_GATE:ROW_GATE + 1] * buf[par, 0] + gt[:, ROW_GATE + 1:ROW_GATE + 2] * buf[par, 1]
    y = _layer_norm(DEEPNORM_ALPHA * x_ref[...] + ffn, g_ref[...], b_ref[...])
    o_ref[...] = y
    ob_ref[...] = y.astype(ob_ref.dtype)


def _combine_ln(x, ys, rows, gt, g, b):
    t, d = x.shape
    tm = _tile(t, COMBINE_TM)
    n = t // tm
    row = pl.BlockSpec((tm, d), lambda i: (i, 0))
    vec = pl.BlockSpec((1, d), lambda i: (0, 0))
    return pl.pallas_call(
        functools.partial(_combine_kernel, tm=tm),
        out_shape=(jax.ShapeDtypeStruct((t, d), F32), jax.ShapeDtypeStruct((t, d), BF16)),
        grid=(n,),
        in_specs=[pl.BlockSpec((8, tm), lambda i: (0, i), memory_space=pltpu.SMEM),
                  pl.BlockSpec((8, tm), lambda i: (0, jnp.minimum(i + 1, n - 1)), memory_space=pltpu.SMEM),
                  row,
                  pl.BlockSpec((tm, LANE), lambda i: (i, 0)),
                  vec, vec,
                  pl.BlockSpec(memory_space=pl.ANY)],
        out_specs=(row, row),
        scratch_shapes=[pltpu.VMEM((2, TOP_K, tm, d), F32), pltpu.SemaphoreType.DMA((2,))],
        compiler_params=_params(("arbitrary",)),
        name="moe_combine_ln",
    )(rows, rows, x, gt, g, b, ys)


def _rope_tables(seq):
    def tab(pos, dim):
        inv = ROPE_THETA ** (-jnp.arange(0, dim, 2, dtype=F32) / dim)
        ang = pos.astype(F32)[:, None] * inv[None, :]
        return jnp.cos(ang), jnp.sin(ang)

    pos = jnp.arange(seq)
    cos1, sin1 = tab(pos, HEAD_DIM)
    cos_r, sin_r = tab(pos // GRID_W, HEAD_DIM // 2)
    cos_c, sin_c = tab(pos % GRID_W, HEAD_DIM // 2)
    z = jnp.zeros_like(sin_r)
    c1 = jnp.concatenate([cos1, cos1], axis=1)
    s1 = jnp.concatenate([-sin1, sin1], axis=1)
    cax = jnp.concatenate([cos_r, cos_r, cos_c, cos_c], axis=1)
    sa = jnp.concatenate([-sin_r, z, -sin_c, z], axis=1)
    sb = jnp.concatenate([z, sin_r, z, sin_c], axis=1)
    return c1, s1, cax, sa, sb


def _moe(x1, lw, shared):
    t, d = x1.shape
    er, gt, cnt = _router(x1, shared["wr_hi"], shared["wr_lo"], shared["r_bias"])
    counts = cnt[:, 0].astype(I32)
    padded = (counts + MOE_BLOCK - 1) // MOE_BLOCK * MOE_BLOCK
    pad_end = jnp.cumsum(padded)
    pad_start = (pad_end - padded).astype(I32)
    n_rows = t * TOP_K + N_EXPERTS * MOE_BLOCK
    nb = n_rows // MOE_BLOCK
    blk_row = jnp.arange(nb, dtype=I32)[:, None] * MOE_BLOCK
    blk_e = jnp.minimum(jnp.sum((pad_end[None, :] <= blk_row).astype(I32), axis=1), N_EXPERTS - 1)
    blk_e = blk_e + lw["layer"] * N_EXPERTS
    n_active = (pad_end[-1:] // MOE_BLOCK).astype(I32)
    rows = _slot_rows(pad_start, er)
    xs = _dispatch(x1, rows, (pad_start + counts).astype(I32), pad_end.astype(I32), n_rows)
    ys = _experts(xs, blk_e, n_active, shared["w1"], shared["w3"], shared["w2"])
    return _combine_ln(x1, ys, rows, gt, lw["ln2_g"], lw["ln2_b"])


def _trunk(x, layers, shared):
    b, s, d = x.shape
    t = b * s
    tables = _rope_tables(s)
    xf = x.reshape(t, d)
    xb = xf.astype(BF16)
    for l, lw in enumerate(layers):
        gains = lw["qk_gains"]
        gates = _gate_proj(xb, lw["w_gate"], lw["b_gate"])
        a_parts = [_attn_a(_project(xb, lw["w_a"][g], tables, gains, s, A_TN, A_TILE_KINDS, dil, f"proj_a{g}"))
                   for g, (_, dil) in enumerate(A_CONFIGS)]
        (o0, l0), (o1, l1), (o2, l2) = a_parts
        oa = _merge_a(o0.reshape(t, A_GROUP_COLS), l0.reshape(t, LANE), o1, l1, o2, l2, s)
        qkv_b = _project(xb, lw["w_b"], tables, gains, s, B_TN, B_TILE_KINDS, 1, "proj_b")
        ob = _attn_b(qkv_b.reshape(b, s, B_COLS))
        lambda_init = 0.8 - 0.6 * float(np.exp(-0.3 * l))
        qkv_c = _project(xb, lw["w_c"], tables, gains, s, C_TN, C_TILE_KINDS, 1, "proj_c")
        oc = _attn_c(qkv_c.reshape(b, s, C_COLS), lw["lam_rows"], lw["subln_g"], lambda_init)
        merged = _branch_proj(oa, ob, oc, gates, lw["w_proj_a"], lw["w_proj_b"], lw["w_proj_c"])
        x1 = _out_proj_ln(merged, lw["w_out"], xf, lw["ln1_g"], lw["ln1_b"])
        xf, xb = _moe(x1, lw, shared)
    return xf.reshape(b, s, d)


def kernel(x_prompt, x_sample, w_in, q_norm_g, k_norm_g, lambda_q1, lambda_k1, lambda_q2, lambda_k2, subln_g,
           w_gate, b_gate, w_proj_a, w_proj_b, w_proj_c, w_out, ln1_g, ln1_b, w_router, router_bias,
           w1, w3, w2, ln2_g, ln2_b):
    assert [dil for _, dil in A_CONFIGS] == [1, 4, 16] and all(w // (2 * dil) == A_RADIUS for w, dil in A_CONFIGS)
    depth = w_in.shape[0]
    layers = []
    part = A_HEADS * HEAD_DIM
    for l in range(depth):
        w_l = w_in[l].astype(BF16)
        w_a = [jnp.concatenate([w_l[:, p * part + g * A_GROUP_COLS: p * part + (g + 1) * A_GROUP_COLS] for p in range(3)],
                               axis=1) for g in range(len(A_CONFIGS))]
        layers.append(dict(
            w_a=w_a, w_b=w_l[:, A_COLS:A_COLS + B_COLS], w_c=w_l[:, A_COLS + B_COLS:],
            qk_gains=jnp.stack([q_norm_g[l], k_norm_g[l]]).astype(F32),
            lam_rows=jnp.stack([lambda_q1[l], lambda_k1[l], lambda_q2[l], lambda_k2[l]]).astype(F32),
            subln_g=subln_g[l].reshape(1, -1).astype(F32),
            w_gate=w_gate[l].astype(BF16),
            b_gate=b_gate[l].reshape(1, -1).astype(F32),
            w_proj_a=w_proj_a[l].astype(BF16),
            w_proj_b=w_proj_b[l].astype(BF16),
            w_proj_c=w_proj_c[l].astype(BF16),
            w_out=w_out[l].astype(BF16),
            ln1_g=ln1_g[l].reshape(1, -1).astype(F32),
            ln1_b=ln1_b[l].reshape(1, -1).astype(F32),
            layer=l,
            ln2_g=ln2_g[l].reshape(1, -1).astype(F32),
            ln2_b=ln2_b[l].reshape(1, -1).astype(F32),
        ))
    wr_t = w_router.astype(F32).T
    wr_hi = wr_t.astype(BF16)
    wr_lo = (wr_t - wr_hi.astype(F32)).astype(BF16)
    def stacked(w):
        return w.astype(BF16).reshape((depth * N_EXPERTS,) + w.shape[2:])

    shared = dict(wr_hi=wr_hi, wr_lo=wr_lo, r_bias=router_bias.astype(F32).reshape(-1, 1),
                  w1=stacked(w1), w3=stacked(w3), w2=stacked(w2))
    return (_trunk(x_prompt, layers, shared), _trunk(x_sample, layers, shared))
```

```python
import functools

import numpy as np
import jax
import jax.numpy as jnp
from jax import lax
from jax.experimental import pallas as pl
from jax.experimental.pallas import tpu as pltpu

F32 = jnp.float32
BF16 = jnp.bfloat16
I32 = jnp.int32

DEPTH = 2
HEAD_DIM = 128
GRID_W = 64
ROPE_THETA = 10000.0
A_CONFIGS = ((128, 1), (512, 4), (2048, 16))
A_HEADS_PER_GROUP = 6
A_HEADS = A_HEADS_PER_GROUP * len(A_CONFIGS)
A_GROUP_COLS = A_HEADS_PER_GROUP * HEAD_DIM
A_OUT = A_GROUP_COLS
B_Q_HEADS = 8
B_KV_HEADS = 2
B_GROUP = B_Q_HEADS // B_KV_HEADS
B_OUT = B_Q_HEADS * HEAD_DIM
C_HEADS = 4
C_OUT = C_HEADS * 2 * HEAD_DIM
N_BRANCH = 3
N_EXPERTS = 16
N_EXPERT_GROUPS = 4
EXPERTS_PER_GROUP = N_EXPERTS // N_EXPERT_GROUPS
TOP_K = 2
DEEPNORM_ALPHA = (2 * DEPTH) ** 0.25
LN_EPS = 1e-5
RMS_EPS = 1e-6
NEG_INF = -1e30
LOG2E = 1.4426950408889634
LN2 = 0.6931471805599453
ATTN_SCALE = HEAD_DIM ** -0.5 * LOG2E
A_RADIUS = 64

HPG = A_HEADS_PER_GROUP
B_Q0, B_K0, B_V0 = 0, B_Q_HEADS, B_Q_HEADS + B_KV_HEADS
B_COLS = (B_Q_HEADS + 2 * B_KV_HEADS) * HEAD_DIM
C_COLS = 6 * C_HEADS * HEAD_DIM
A_COLS = 3 * A_HEADS * HEAD_DIM
K_PLAIN, K_ROPE, K_ROPE_Q, K_AXIAL_Q, K_AXIAL_K = range(5)

LANE = 128
QKV_TM = 1024
A_TN, B_TN, C_TN = A_GROUP_COLS, 4 * LANE, 8 * LANE
PROJ_CHUNK_HEADS = 2
GATE_TM, GATE_TN = 1024, 1024
PROJ_TM, PROJ_TN = 1024, 512
OUT_TM = 512
ROUTER_TM = 512
MOE_BLOCK = 256
COMBINE_TM = 256
DISPATCH_TM = 256
SLOT_TM = 2048
DMA_UNROLL = 8
ATTN_TQ = 256
A_UNITS_PER_STEP = 2
C_HEADS_PER_STEP = 4
MERGE_TM = 1024
VMEM_LIMIT = 56 * 1024 * 1024
ROW_E, ROW_RANK, ROW_GATE = 0, 2, 4


def _params(sem, vmem=VMEM_LIMIT):
    return pltpu.CompilerParams(dimension_semantics=sem, vmem_limit_bytes=vmem)


def _tile(n, t):
    t = min(n, t)
    assert n % t == 0, (n, t)
    return t


def _rope_head(xh, c1_ref, s1_ref):
    return xh * c1_ref[...] + pltpu.roll(xh, 64, 1) * s1_ref[...]


def _head_epilogue(kind, xh, c1_ref, s1_ref, cax_ref, sa_ref, sb_ref, g_ref):
    if kind == K_PLAIN:
        return xh
    if kind in (K_ROPE, K_ROPE_Q):
        y = _rope_head(xh, c1_ref, s1_ref)
    else:
        g = g_ref[0:1, :] if kind == K_AXIAL_Q else g_ref[1:2, :]
        sq = xh * xh
        sq_hi = sq.astype(BF16)
        sq_lo = (sq - sq_hi.astype(F32)).astype(BF16)
        ones = jnp.ones((LANE, LANE), BF16)
        ssum = jnp.dot(sq_hi, ones, preferred_element_type=F32) + jnp.dot(sq_lo, ones, preferred_element_type=F32)
        r = xh * lax.rsqrt(ssum * (1.0 / LANE) + RMS_EPS) * g
        y = r * cax_ref[...] + pltpu.roll(r, 96, 1) * sa_ref[...] + pltpu.roll(r, 32, 1) * sb_ref[...]
    return y * ATTN_SCALE if kind in (K_ROPE_Q, K_AXIAL_Q) else y


def _project_kernel(x_ref, w_ref, c1_ref, s1_ref, cax_ref, sa_ref, sb_ref, g_ref, o_ref, *scratch,
                    tile_kinds, dilation):
    j = pl.program_id(1)
    patterns = {}
    for idx, pattern in enumerate(tile_kinds):
        patterns.setdefault(pattern, []).append(idx)

    def emit(pattern):
        axial = any(kind in (K_AXIAL_Q, K_AXIAL_K) for kind in pattern)
        chunk = len(pattern) if axial else PROJ_CHUNK_HEADS
        for c0 in range(0, len(pattern), chunk):
            acc = jnp.dot(x_ref[...], w_ref[:, c0 * LANE:(c0 + chunk) * LANE], preferred_element_type=F32)
            for hh in range(chunk):
                h = c0 + hh
                cols = slice(h * LANE, (h + 1) * LANE)
                y = _head_epilogue(pattern[h], acc[:, hh * LANE:(hh + 1) * LANE],
                                   c1_ref, s1_ref, cax_ref, sa_ref, sb_ref, g_ref)
                if dilation == 1:
                    o_ref[0, 0, :, cols] = y.astype(o_ref.dtype)
                else:
                    scratch[0][h] = y

    for pattern, idxs in patterns.items():
        cond = functools.reduce(jnp.logical_or, [j == i for i in idxs])
        pl.when(cond)(functools.partial(emit, pattern))

    if dilation > 1:
        y_ref = scratch[0]
        rows = y_ref.shape[1] // dilation
        for c in range(dilation):
            for h in range(y_ref.shape[0]):
                o_ref[0, c, :, h * LANE:(h + 1) * LANE] = (
                    y_ref[h, pl.ds(c, rows, stride=dilation), :].astype(o_ref.dtype))


def _project(xb, w, tables, gains, seq, tn, tile_kinds, dilation, name):
    t, d = xb.shape
    n = w.shape[1]
    assert n == tn * len(tile_kinds) and all(len(p) * LANE == tn for p in tile_kinds)
    tm = _tile(seq, QKV_TM)
    nsb = seq // tm
    tab_spec = pl.BlockSpec((tm, LANE), lambda i, j: (i % nsb, 0))
    scratch = [] if dilation == 1 else [pltpu.VMEM((tn // LANE, tm, LANE), F32)]
    return pl.pallas_call(
        functools.partial(_project_kernel, tile_kinds=tuple(tile_kinds), dilation=dilation),
        out_shape=jax.ShapeDtypeStruct((t // seq, dilation, seq // dilation, n), BF16),
        grid=(t // tm, n // tn),
        in_specs=[pl.BlockSpec((tm, d), lambda i, j: (i, 0)),
                  pl.BlockSpec((d, tn), lambda i, j: (0, j)),
                  tab_spec, tab_spec, tab_spec, tab_spec, tab_spec,
                  pl.BlockSpec((2, LANE), lambda i, j: (0, 0))],
        out_specs=pl.BlockSpec((1, dilation, tm // dilation, tn), lambda i, j: (i // nsb, 0, i % nsb, j)),
        scratch_shapes=scratch,
        compiler_params=_params(("parallel", "arbitrary")),
        name=name,
    )(xb, w, *tables, gains)


A_TILE_KINDS = ((K_ROPE_Q,) * HPG, (K_ROPE,) * HPG, (K_PLAIN,) * HPG)
B_TILE_KINDS = ((K_AXIAL_Q,) * 4, (K_AXIAL_Q,) * 4, (K_AXIAL_K,) * B_KV_HEADS + (K_PLAIN,) * B_KV_HEADS)
C_TILE_KINDS = ((K_ROPE_Q,) * 8, (K_ROPE,) * 8, (K_PLAIN,) * 8)


def _gate_kernel(x_ref, w_ref, b_ref, o_ref):
    acc = jnp.dot(x_ref[...], w_ref[...], preferred_element_type=F32) + b_ref[...]
    o_ref[...] = jax.nn.sigmoid(acc).astype(o_ref.dtype)


def _gate_proj(xb, w, b):
    t, d = xb.shape
    n = w.shape[1]
    tm, tn = _tile(t, GATE_TM), _tile(n, GATE_TN)
    return pl.pallas_call(
        _gate_kernel,
        out_shape=jax.ShapeDtypeStruct((t, n), BF16),
        grid=(t // tm, n // tn),
        in_specs=[pl.BlockSpec((tm, d), lambda i, j: (i, 0)),
                  pl.BlockSpec((d, tn), lambda i, j: (0, j)),
                  pl.BlockSpec((1, tn), lambda i, j: (0, j))],
        out_specs=pl.BlockSpec((tm, tn), lambda i, j: (i, j)),
        compiler_params=_params(("parallel", "arbitrary")),
        name="gate_proj",
    )(xb, w, b)


def _attn_a_kernel(q_ref, k_ref, v_ref, o_ref, lse_ref, *, tq, win, sub_len):
    n_res, n_qb = q_ref.shape[1], q_ref.shape[2] // tq
    lane = lax.broadcasted_iota(I32, (tq, LANE), 1)
    for qb in range(n_qb):
        q0 = (pl.program_id(2) * n_qb + qb) * tq
        start = jnp.clip(q0 - A_RADIUS, 0, sub_len - win)
        if win % 16 == 0 and tq % 64 == 0:
            start = pl.multiple_of(start, 16)
        q_pos = q0 + lax.broadcasted_iota(I32, (tq, win), 0)
        k_pos = start + lax.broadcasted_iota(I32, (tq, win), 1)
        bias = jnp.where(jnp.abs(k_pos - q_pos) <= A_RADIUS, 0.0, NEG_INF).astype(F32)
        rows = slice(qb * tq, (qb + 1) * tq)
        for c in range(n_res):
            lse_all = jnp.zeros((tq, LANE), F32)
            for h in range(HPG):
                cols = slice(h * LANE, (h + 1) * LANE)
                q = q_ref[0, c, rows, cols]
                k = k_ref[0, c, pl.ds(start, win), cols]
                v = v_ref[0, c, pl.ds(start, win), cols]
                s = lax.dot_general(q, k, (((1,), (1,)), ((), ())), preferred_element_type=F32) + bias
                m = jnp.max(s, axis=-1, keepdims=True)
                p = jnp.exp2(s - m).astype(BF16)
                ov = jnp.dot(p, jnp.concatenate([v, jnp.ones_like(v)], axis=1), preferred_element_type=F32)
                den = ov[:, LANE:]
                o_ref[0, c, rows, cols] = (ov[:, :LANE] / den).astype(o_ref.dtype)
                lse_all = jnp.where(lane == h, (m + jnp.log2(den)) * LN2, lse_all)
            lse_ref[0, c, rows, :] = lse_all


def _attn_a(qkv4):
    b, r, sub_len, _ = qkv4.shape
    tq = _tile(sub_len, ATTN_TQ)
    win = min(tq + 2 * A_RADIUS, sub_len)
    n_qb = min(A_UNITS_PER_STEP, sub_len // tq)
    n_res = min(max(A_UNITS_PER_STEP // n_qb, 1), r)
    assert (sub_len // tq) % n_qb == 0 and r % n_res == 0
    return pl.pallas_call(
        functools.partial(_attn_a_kernel, tq=tq, win=win, sub_len=sub_len),
        out_shape=(jax.ShapeDtypeStruct((b, r, sub_len, A_GROUP_COLS), BF16),
                   jax.ShapeDtypeStruct((b, r, sub_len, LANE), F32)),
        grid=(b, r // n_res, sub_len // (tq * n_qb)),
        in_specs=[pl.BlockSpec((1, n_res, tq * n_qb, A_GROUP_COLS), lambda bi, c, qi: (bi, c, qi, 0)),
                  pl.BlockSpec((1, n_res, sub_len, A_GROUP_COLS), lambda bi, c, qi: (bi, c, 0, 1)),
                  pl.BlockSpec((1, n_res, sub_len, A_GROUP_COLS), lambda bi, c, qi: (bi, c, 0, 2))],
        out_specs=(pl.BlockSpec((1, n_res, tq * n_qb, A_GROUP_COLS), lambda bi, c, qi: (bi, c, qi, 0)),
                   pl.BlockSpec((1, n_res, tq * n_qb, LANE), lambda bi, c, qi: (bi, c, qi, 0))),
        compiler_params=_params(("parallel", "parallel", "arbitrary")),
        name=f"attn_a_r{r}",
    )(qkv4, qkv4, qkv4)


def _merge_a_kernel(o0_ref, l0_ref, o1_ref, l1_ref, o2_ref, l2_ref, o_ref, so1, sl1, so2, sl2):
    def to_token_order(src_ref, dst_ref):
        r, rows = src_ref.shape[1], src_ref.shape[2]
        for c in range(r):
            for h in range(dst_ref.shape[0]):
                dst_ref[h, pl.ds(c, rows, stride=r), :] = src_ref[0, c, :, h * LANE:(h + 1) * LANE].astype(F32)

    to_token_order(o1_ref, so1)
    to_token_order(l1_ref, sl1)
    to_token_order(o2_ref, so2)
    to_token_order(l2_ref, sl2)
    l0, l1, l2 = l0_ref[...], sl1[0], sl2[0]
    m = jnp.maximum(jnp.maximum(l0, l1), l2)
    e0, e1, e2 = jnp.exp(l0 - m), jnp.exp(l1 - m), jnp.exp(l2 - m)
    den = e0 + e1 + e2
    w0, w1, w2 = e0 / den, e1 / den, e2 / den
    for h in range(HPG):
        cols = slice(h * LANE, (h + 1) * LANE)
        acc = (o0_ref[:, cols].astype(F32) * w0[:, h:h + 1]
               + so1[h] * w1[:, h:h + 1]
               + so2[h] * w2[:, h:h + 1])
        o_ref[:, cols] = acc.astype(o_ref.dtype)


def _merge_a(o0, l0, o1, l1, o2, l2, seq):
    t = o0.shape[0]
    tm = _tile(seq, MERGE_TM)
    nsb = seq // tm

    def res_spec(arr):
        r, n = arr.shape[1], arr.shape[3]
        return pl.BlockSpec((1, r, tm // r, n), lambda i: (i // nsb, 0, i % nsb, 0))

    return pl.pallas_call(
        _merge_a_kernel,
        out_shape=jax.ShapeDtypeStruct((t, A_OUT), BF16),
        grid=(t // tm,),
        in_specs=[pl.BlockSpec((tm, A_GROUP_COLS), lambda i: (i, 0)), pl.BlockSpec((tm, LANE), lambda i: (i, 0)),
                  res_spec(o1), res_spec(l1), res_spec(o2), res_spec(l2)],
        out_specs=pl.BlockSpec((tm, A_OUT), lambda i: (i, 0)),
        scratch_shapes=[pltpu.VMEM((HPG, tm, LANE), F32), pltpu.VMEM((1, tm, LANE), F32),
                        pltpu.VMEM((HPG, tm, LANE), F32), pltpu.VMEM((1, tm, LANE), F32)],
        compiler_params=_params(("parallel",)),
        name="merge_a",
    )(o0, l0, o1, l1, o2, l2)


def _softmax_pv(q, k, v):
    s = lax.dot_general(q, k, (((1,), (1,)), ((), ())), preferred_element_type=F32)
    m = jnp.max(s, axis=-1, keepdims=True)
    p = jnp.exp2(s - m)
    den = jnp.sum(p, axis=-1, keepdims=True)
    return jnp.dot(p.astype(BF16), v, preferred_element_type=F32) / den


def _attn_b_kernel(q_ref, k_ref, v_ref, o_ref):
    for h in range(B_KV_HEADS):
        k = k_ref[0, :, h * LANE:(h + 1) * LANE]
        v = v_ref[0, :, h * LANE:(h + 1) * LANE]
        v1 = jnp.concatenate([v, jnp.ones_like(v)], axis=1)
        for g in range(B_GROUP):
            cols = slice((h * B_GROUP + g) * LANE, (h * B_GROUP + g + 1) * LANE)
            s = lax.dot_general(q_ref[0, :, cols], k, (((1,), (1,)), ((), ())), preferred_element_type=F32)
            p = jnp.exp2(s - jnp.max(s, axis=-1, keepdims=True)).astype(BF16)
            ov = jnp.dot(p, v1, preferred_element_type=F32)
            o_ref[0, :, cols] = (ov[:, :LANE] / ov[:, LANE:]).astype(o_ref.dtype)


def _attn_b(qkv3):
    b, s, _ = qkv3.shape
    tq = _tile(s, ATTN_TQ)
    kv_cols = B_KV_HEADS * LANE
    o = pl.pallas_call(
        _attn_b_kernel,
        out_shape=jax.ShapeDtypeStruct((b, s, B_OUT), BF16),
        grid=(b, s // tq),
        in_specs=[pl.BlockSpec((1, tq, B_OUT), lambda bi, qi: (bi, qi, 0)),
                  pl.BlockSpec((1, s, kv_cols), lambda bi, qi: (bi, 0, B_K0 * LANE // kv_cols)),
                  pl.BlockSpec((1, s, kv_cols), lambda bi, qi: (bi, 0, B_V0 * LANE // kv_cols))],
        out_specs=pl.BlockSpec((1, tq, B_OUT), lambda bi, qi: (bi, qi, 0)),
        compiler_params=_params(("parallel", "arbitrary")),
        name="attn_b",
    )(qkv3, qkv3, qkv3)
    return o.reshape(b * s, B_OUT)


def _attn_c_kernel(q_ref, k_ref, v_ref, lam_ref, g_ref, o_ref, *, lambda_init):
    lam_p = lam_ref[...].astype(F32)
    lam = (jnp.exp(jnp.sum(lam_p[0:1] * lam_p[1:2], axis=-1, keepdims=True))
           - jnp.exp(jnp.sum(lam_p[2:3] * lam_p[3:4], axis=-1, keepdims=True)) + lambda_init)
    for hh in range(C_HEADS_PER_STEP):
        c0 = hh * 2 * LANE
        first, second = slice(c0, c0 + LANE), slice(c0 + LANE, c0 + 2 * LANE)
        v = v_ref[0, :, c0:c0 + 2 * LANE]
        o = (_softmax_pv(q_ref[0, :, first], k_ref[0, :, first], v)
             - lam * _softmax_pv(q_ref[0, :, second], k_ref[0, :, second], v))
        o = o * lax.rsqrt(jnp.mean(o * o, axis=-1, keepdims=True) + RMS_EPS) * g_ref[...].astype(F32)
        o_ref[0, :, c0:c0 + 2 * LANE] = (o * (1.0 - lambda_init)).astype(o_ref.dtype)


def _attn_c(qkv3, lam_rows, subln_g, lambda_init):
    b, s, _ = qkv3.shape
    tq = _tile(s, ATTN_TQ)
    cols = C_HEADS_PER_STEP * 2 * LANE
    steps = C_HEADS // C_HEADS_PER_STEP
    o = pl.pallas_call(
        functools.partial(_attn_c_kernel, lambda_init=lambda_init),
        out_shape=jax.ShapeDtypeStruct((b, s, C_OUT), BF16),
        grid=(b, steps, s // tq),
        in_specs=[pl.BlockSpec((1, tq, cols), lambda bi, h, qi: (bi, qi, h)),
                  pl.BlockSpec((1, s, cols), lambda bi, h, qi: (bi, 0, steps + h)),
                  pl.BlockSpec((1, s, cols), lambda bi, h, qi: (bi, 0, 2 * steps + h)),
                  pl.BlockSpec((4, LANE), lambda bi, h, qi: (0, 0)),
                  pl.BlockSpec((1, 2 * LANE), lambda bi, h, qi: (0, 0))],
        out_specs=pl.BlockSpec((1, tq, cols), lambda bi, h, qi: (bi, qi, h)),
        compiler_params=_params(("parallel", "parallel", "arbitrary")),
        name="attn_c",
    )(qkv3, qkv3, qkv3, lam_rows, subln_g)
    return o.reshape(b * s, C_OUT)


def _branch_proj_kernel(oa_ref, ob_ref, oc_ref, g0_ref, g1_ref, g2_ref, wa_ref, wb_ref, wc_ref, o_ref):
    acc = g0_ref[...].astype(F32) * jnp.dot(oa_ref[...], wa_ref[...], preferred_element_type=F32)
    acc = acc + g1_ref[...].astype(F32) * jnp.dot(ob_ref[...], wb_ref[...], preferred_element_type=F32)
    acc = acc + g2_ref[...].astype(F32) * jnp.dot(oc_ref[...], wc_ref[...], preferred_element_type=F32)
    o_ref[...] = acc.astype(o_ref.dtype)


def _branch_proj(oa, ob, oc, gates, wa, wb, wc):
    t = oa.shape[0]
    d = wa.shape[1]
    tm, tn = _tile(t, PROJ_TM), _tile(d, PROJ_TN)
    nb = d // tn
    gspecs = [pl.BlockSpec((tm, tn), functools.partial(lambda i, j, br: (i, br * nb + j), br=br)) for br in range(N_BRANCH)]
    return pl.pallas_call(
        _branch_proj_kernel,
        out_shape=jax.ShapeDtypeStruct((t, d), BF16),
        grid=(t // tm, nb),
        in_specs=[pl.BlockSpec((tm, A_OUT), lambda i, j: (i, 0)),
                  pl.BlockSpec((tm, B_OUT), lambda i, j: (i, 0)),
                  pl.BlockSpec((tm, C_OUT), lambda i, j: (i, 0))] + gspecs +
                 [pl.BlockSpec((A_OUT, tn), lambda i, j: (0, j)),
                  pl.BlockSpec((B_OUT, tn), lambda i, j: (0, j)),
                  pl.BlockSpec((C_OUT, tn), lambda i, j: (0, j))],
        out_specs=pl.BlockSpec((tm, tn), lambda i, j: (i, j)),
        compiler_params=_params(("parallel", "arbitrary")),
        name="branch_proj",
    )(oa, ob, oc, gates, gates, gates, wa, wb, wc)


def _layer_norm(z, g, b):
    mu = jnp.mean(z, axis=-1, keepdims=True)
    zc = z - mu
    var = jnp.mean(zc * zc, axis=-1, keepdims=True)
    return zc * lax.rsqrt(var + LN_EPS) * g + b


def _out_ln_kernel(m_ref, w_ref, x_ref, g_ref, b_ref, o_ref):
    mix = jnp.dot(m_ref[...], w_ref[...], preferred_element_type=F32)
    o_ref[...] = _layer_norm(DEEPNORM_ALPHA * x_ref[...] + mix, g_ref[...], b_ref[...])


def _out_proj_ln(merged, w_out, x, g, b):
    t, d = x.shape
    tm = _tile(t, OUT_TM)
    row = pl.BlockSpec((tm, d), lambda i: (i, 0))
    vec = pl.BlockSpec((1, d), lambda i: (0, 0))
    return pl.pallas_call(
        _out_ln_kernel,
        out_shape=jax.ShapeDtypeStruct((t, d), F32),
        grid=(t // tm,),
        in_specs=[row, pl.BlockSpec((d, d), lambda i: (0, 0)), row, vec, vec],
        out_specs=row,
        compiler_params=_params(("parallel",)),
        name="out_proj_ln",
    )(merged, w_out, x, g, b)


def _router_kernel(x_ref, whi_ref, wlo_ref, bias_ref, er_ref, gt_ref, cnt_ref, carry_ref, *, tm):
    @pl.when(pl.program_id(0) == 0)
    def _():
        carry_ref[...] = jnp.zeros_like(carry_ref)

    x = x_ref[...]
    x_hi = x.astype(BF16)
    x_lo = (x - x_hi.astype(F32)).astype(BF16)
    nt = (((1,), (1,)), ((), ()))
    logits = (lax.dot_general(whi_ref[...], x_hi, nt, preferred_element_type=F32)
              + lax.dot_general(wlo_ref[...], x_hi, nt, preferred_element_type=F32)
              + lax.dot_general(whi_ref[...], x_lo, nt, preferred_element_type=F32))
    scores = jax.nn.sigmoid(logits)
    biased = scores + bias_ref[...]

    def row(a, e):
        return a[e:e + 1, :]

    gscore = []
    for g in range(N_EXPERT_GROUPS):
        a, b, c, d = (row(biased, g * EXPERTS_PER_GROUP + i) for i in range(EXPERTS_PER_GROUP))
        hi1, lo1, hi2, lo2 = jnp.maximum(a, b), jnp.minimum(a, b), jnp.maximum(c, d), jnp.minimum(c, d)
        gscore.append(jnp.maximum(hi1, hi2) + jnp.maximum(jnp.minimum(hi1, hi2), jnp.maximum(lo1, lo2)))
    gsel = jnp.zeros((1, tm), I32)
    best = gscore[0]
    for g in range(1, N_EXPERT_GROUPS):
        better = gscore[g] > best
        gsel = jnp.where(better, g, gsel)
        best = jnp.where(better, gscore[g], best)

    def pick(a, i):
        out = row(a, i)
        for g in range(1, N_EXPERT_GROUPS):
            out = jnp.where(gsel == g, row(a, g * EXPERTS_PER_GROUP + i), out)
        return out

    bv = [pick(biased, i) for i in range(EXPERTS_PER_GROUP)]
    sv = [pick(scores, i) for i in range(EXPERTS_PER_GROUP)]
    i0 = jnp.zeros((1, tm), I32)
    b0 = bv[0]
    for i in range(1, EXPERTS_PER_GROUP):
        better = bv[i] > b0
        i0 = jnp.where(better, i, i0)
        b0 = jnp.where(better, bv[i], b0)
    i1 = jnp.full((1, tm), -1, I32)
    b1 = jnp.full((1, tm), -jnp.inf, F32)
    for i in range(EXPERTS_PER_GROUP):
        better = (i0 != i) & ((i1 < 0) | (bv[i] > b1))
        i1 = jnp.where(better, i, i1)
        b1 = jnp.where(better, bv[i], b1)
    s0 = jnp.zeros((1, tm), F32)
    s1 = jnp.zeros((1, tm), F32)
    for i in range(EXPERTS_PER_GROUP):
        s0 = jnp.where(i0 == i, sv[i], s0)
        s1 = jnp.where(i1 == i, sv[i], s1)
    e0 = gsel * EXPERTS_PER_GROUP + i0
    e1 = gsel * EXPERTS_PER_GROUP + i1
    den = s0 + s1
    g0, g1 = s0 / den, s1 / den

    erow = lax.broadcasted_iota(I32, (N_EXPERTS, tm), 0)
    member = ((erow == e0) | (erow == e1))
    tri = (lax.broadcasted_iota(I32, (tm, tm), 0) < lax.broadcasted_iota(I32, (tm, tm), 1))
    prefix = jnp.dot(member.astype(BF16), tri.astype(BF16), preferred_element_type=F32) + carry_ref[:, 0:1]
    r0 = jnp.sum(jnp.where(erow == e0, prefix, 0.0), axis=0, keepdims=True)
    r1 = jnp.sum(jnp.where(erow == e1, prefix, 0.0), axis=0, keepdims=True)
    carry_ref[...] = carry_ref[...] + jnp.sum(member.astype(F32), axis=1, keepdims=True)
    cnt_ref[...] = carry_ref[...]

    zi = jnp.zeros((1, tm), I32)
    er_ref[...] = jnp.concatenate([e0, e1, r0.astype(I32), r1.astype(I32), zi, zi, zi, zi], axis=0)
    zf = jnp.zeros((1, tm), F32)
    rec = jnp.concatenate([zf, zf, zf, zf, g0, g1, zf, zf, jnp.zeros((LANE - 8, tm), F32)], axis=0)
    gt_ref[...] = rec.T


def _router(x, w_hi_t, w_lo_t, bias_col):
    t, d = x.shape
    tm = _tile(t, ROUTER_TM)
    return pl.pallas_call(
        functools.partial(_router_kernel, tm=tm),
        out_shape=(jax.ShapeDtypeStruct((8, t), I32), jax.ShapeDtypeStruct((t, LANE), F32),
                   jax.ShapeDtypeStruct((N_EXPERTS, LANE), F32)),
        grid=(t // tm,),
        in_specs=[pl.BlockSpec((tm, d), lambda i: (i, 0)),
                  pl.BlockSpec((N_EXPERTS, d), lambda i: (0, 0)),
                  pl.BlockSpec((N_EXPERTS, d), lambda i: (0, 0)),
                  pl.BlockSpec((N_EXPERTS, 1), lambda i: (0, 0))],
        out_specs=(pl.BlockSpec((8, tm), lambda i: (0, i)),
                   pl.BlockSpec((tm, LANE), lambda i: (i, 0)),
                   pl.BlockSpec((N_EXPERTS, LANE), lambda i: (0, 0))),
        scratch_shapes=[pltpu.VMEM((N_EXPERTS, LANE), F32)],
        compiler_params=_params(("arbitrary",)),
        name="router",
    )(x, w_hi_t, w_lo_t, bias_col)


def _slot_rows_kernel(pad_start_ref, er_ref, o_ref):
    e = er_ref[ROW_E:ROW_E + TOP_K, :]
    base = jnp.zeros_like(e)
    for x in range(N_EXPERTS):
        base = jnp.where(e == x, pad_start_ref[x], base)
    rows = base + er_ref[ROW_RANK:ROW_RANK + TOP_K, :]
    o_ref[...] = jnp.concatenate([rows, jnp.zeros((8 - TOP_K, rows.shape[1]), I32)], axis=0)


def _slot_rows(pad_start, er):
    t = er.shape[1]
    tm = _tile(t, SLOT_TM)
    grid_spec = pltpu.PrefetchScalarGridSpec(
        num_scalar_prefetch=1,
        grid=(t // tm,),
        in_specs=[pl.BlockSpec((8, tm), lambda i, ps: (0, i))],
        out_specs=pl.BlockSpec((8, tm), lambda i, ps: (0, i)),
    )
    return pl.pallas_call(
        _slot_rows_kernel,
        out_shape=jax.ShapeDtypeStruct((8, t), I32),
        grid_spec=grid_spec,
        compiler_params=_params(("arbitrary",)),
        name="moe_slot_rows",
    )(pad_start, er)


def _dispatch_kernel(fill_end_ref, pad_end_ref, rows_ref, x_ref, xs_hbm, zblk, sem, zsem, *, tm):
    def row_copy(r, dst_row):
        return pltpu.make_async_copy(x_ref.at[pl.ds(r, 1), :], xs_hbm.at[pl.ds(dst_row, 1), :], sem)

    def issue(r, c):
        row_copy(r, rows_ref[0, r]).start()
        row_copy(r, rows_ref[1, r]).start()
        return c

    lax.fori_loop(0, tm, issue, 0, unroll=DMA_UNROLL)

    def zero_row(dst_row):
        return pltpu.make_async_copy(zblk.at[pl.ds(0, 1), :], xs_hbm.at[pl.ds(dst_row, 1), :], zsem)

    def zero_block(blk):
        dst = xs_hbm.at[pl.ds(pl.multiple_of(blk * MOE_BLOCK, MOE_BLOCK), MOE_BLOCK), :]
        return pltpu.make_async_copy(zblk, dst, zsem)

    @pl.when(pl.program_id(0) == 0)
    def _():
        zblk[...] = jnp.zeros_like(zblk)
        for e in range(N_EXPERTS):
            lo, hi = fill_end_ref[e], pad_end_ref[e]
            lax.fori_loop(lo, hi, lambda r, c: (zero_row(r).start(), c)[1], 0)
            lax.fori_loop(lo, hi, lambda r, c: (zero_row(0).wait(), c)[1], 0)
        first, last = pad_end_ref[N_EXPERTS - 1] // MOE_BLOCK, xs_hbm.shape[0] // MOE_BLOCK
        lax.fori_loop(first, last, lambda blk, c: (zero_block(blk).start(), c)[1], 0)
        lax.fori_loop(first, last, lambda blk, c: (zero_block(0).wait(), c)[1], 0)

    def drain(r, c):
        row_copy(0, 0).wait()
        row_copy(0, 0).wait()
        return c

    lax.fori_loop(0, tm, drain, 0, unroll=DMA_UNROLL)


def _dispatch(x, rows, fill_end, pad_end, n_rows):
    t, d = x.shape
    tm = _tile(t, DISPATCH_TM)
    grid_spec = pltpu.PrefetchScalarGridSpec(
        num_scalar_prefetch=2,
        grid=(t // tm,),
        in_specs=[pl.BlockSpec((8, tm), lambda i, fe, pe: (0, i), memory_space=pltpu.SMEM),
                  pl.BlockSpec((tm, d), lambda i, fe, pe: (i, 0))],
        out_specs=pl.BlockSpec(memory_space=pl.ANY),
        scratch_shapes=[pltpu.VMEM((MOE_BLOCK, d), F32), pltpu.SemaphoreType.DMA, pltpu.SemaphoreType.DMA],
    )
    return pl.pallas_call(
        functools.partial(_dispatch_kernel, tm=tm),
        out_shape=jax.ShapeDtypeStruct((n_rows, d), F32),
        grid_spec=grid_spec,
        compiler_params=_params(("arbitrary",)),
        name="moe_dispatch",
    )(fill_end, pad_end, rows, x)


def _expert_kernel(blk_e_ref, nact_ref, x_ref, w1_ref, w3_ref, w2_ref, o_ref):
    del blk_e_ref

    @pl.when(pl.program_id(0) < nact_ref[0])
    def _():
        x = x_ref[...].astype(BF16)
        h1 = jnp.dot(x, w1_ref[0], preferred_element_type=F32)
        h3 = jnp.dot(x, w3_ref[0], preferred_element_type=F32)
        h = (h1 * jax.nn.sigmoid(h1) * h3).astype(BF16)
        o_ref[...] = jnp.dot(h, w2_ref[0], preferred_element_type=F32)

    @pl.when(pl.program_id(0) >= nact_ref[0])
    def _():
        o_ref[...] = jnp.zeros_like(o_ref)


def _experts(xs, blk_e, n_active, w1, w3, w2):
    p, d = xs.shape
    f = w1.shape[2]
    nb = p // MOE_BLOCK
    grid_spec = pltpu.PrefetchScalarGridSpec(
        num_scalar_prefetch=2,
        grid=(nb,),
        in_specs=[pl.BlockSpec((MOE_BLOCK, d), lambda i, be, na: (jnp.minimum(i, na[0] - 1), 0)),
                  pl.BlockSpec((1, d, f), lambda i, be, na: (be[i], 0, 0)),
                  pl.BlockSpec((1, d, f), lambda i, be, na: (be[i], 0, 0)),
                  pl.BlockSpec((1, f, d), lambda i, be, na: (be[i], 0, 0))],
        out_specs=pl.BlockSpec((MOE_BLOCK, d), lambda i, be, na: (i, 0)),
    )
    return pl.pallas_call(
        _expert_kernel,
        out_shape=jax.ShapeDtypeStruct((p, d), F32),
        grid_spec=grid_spec,
        compiler_params=_params(("arbitrary",)),
        name="moe_experts",
    )(blk_e, n_active, xs, w1, w3, w2)


def _combine_kernel(rows_ref, next_rows_ref, x_ref, gt_ref, g_ref, b_ref, ys_hbm, o_ref, ob_ref, buf, sems, *, tm):
    i = pl.program_id(0)
    n = pl.num_programs(0)

    def row_copy(par, r, slot, src_row):
        return pltpu.make_async_copy(ys_hbm.at[pl.ds(src_row, 1), :], buf.at[par, slot, pl.ds(r, 1), :], sems.at[par])

    def gather(par, idx_ref):
        def issue(r, c):
            row_copy(par, r, 0, idx_ref[0, r]).start()
            row_copy(par, r, 1, idx_ref[1, r]).start()
            return c
        lax.fori_loop(0, tm, issue, 0, unroll=DMA_UNROLL)

    @pl.when(i == 0)
    def _():
        gather(0, rows_ref)

    @pl.when(i + 1 < n)
    def _():
        gather((i + 1) % 2, next_rows_ref)

    par = i % 2

    def drain(r, c):
        row_copy(par, 0, 0, 0).wait()
        row_copy(par, 0, 0, 0).wait()
        return c

    lax.fori_loop(0, tm, drain, 0, unroll=DMA_UNROLL)
    gt = gt_ref[...]
    ffn = gt[:, ROW_GATE:ROW_GATE + 1] * buf[par, 0] + gt[:, ROW_GATE + 1:ROW_GATE + 2] * buf[par, 1]
    y = _layer_norm(DEEPNORM_ALPHA * x_ref[...] + ffn, g_ref[...], b_ref[...])
    o_ref[...] = y
    ob_ref[...] = y.astype(ob_ref.dtype)


def _combine_ln(x, ys, rows, gt, g, b):
    t, d = x.shape
    tm = _tile(t, COMBINE_TM)
    n = t // tm
    row = pl.BlockSpec((tm, d), lambda i: (i, 0))
    vec = pl.BlockSpec((1, d), lambda i: (0, 0))
    return pl.pallas_call(
        functools.partial(_combine_kernel, tm=tm),
        out_shape=(jax.ShapeDtypeStruct((t, d), F32), jax.ShapeDtypeStruct((t, d), BF16)),
        grid=(n,),
        in_specs=[pl.BlockSpec((8, tm), lambda i: (0, i), memory_space=pltpu.SMEM),
                  pl.BlockSpec((8, tm), lambda i: (0, jnp.minimum(i + 1, n - 1)), memory_space=pltpu.SMEM),
                  row,
                  pl.BlockSpec((tm, LANE), lambda i: (i, 0)),
                  vec, vec,
                  pl.BlockSpec(memory_space=pl.ANY)],
        out_specs=(row, row),
        scratch_shapes=[pltpu.VMEM((2, TOP_K, tm, d), F32), pltpu.SemaphoreType.DMA((2,))],
        compiler_params=_params(("arbitrary",)),
        name="moe_combine_ln",
    )(rows, rows, x, gt, g, b, ys)


def _rope_tables(seq):
    def tab(pos, dim):
        inv = ROPE_THETA ** (-jnp.arange(0, dim, 2, dtype=F32) / dim)
        ang = pos.astype(F32)[:, None] * inv[None, :]
        return jnp.cos(ang), jnp.sin(ang)

    pos = jnp.arange(seq)
    cos1, sin1 = tab(pos, HEAD_DIM)
    cos_r, sin_r = tab(pos // GRID_W, HEAD_DIM // 2)
    cos_c, sin_c = tab(pos % GRID_W, HEAD_DIM // 2)
    z = jnp.zeros_like(sin_r)
    c1 = jnp.concatenate([cos1, cos1], axis=1)
    s1 = jnp.concatenate([-sin1, sin1], axis=1)
    cax = jnp.concatenate([cos_r, cos_r, cos_c, cos_c], axis=1)
    sa = jnp.concatenate([-sin_r, z, -sin_c, z], axis=1)
    sb = jnp.concatenate([z, sin_r, z, sin_c], axis=1)
    return c1, s1, cax, sa, sb


def _moe(x1, lw, shared):
    t, d = x1.shape
    er, gt, cnt = _router(x1, shared["wr_hi"], shared["wr_lo"], shared["r_bias"])
    counts = cnt[:, 0].astype(I32)
    padded = (counts + MOE_BLOCK - 1) // MOE_BLOCK * MOE_BLOCK
    pad_end = jnp.cumsum(padded)
    pad_start = (pad_end - padded).astype(I32)
    n_rows = t * TOP_K + N_EXPERTS * MOE_BLOCK
    nb = n_rows // MOE_BLOCK
    blk_row = jnp.arange(nb, dtype=I32)[:, None] * MOE_BLOCK
    blk_e = jnp.minimum(jnp.sum((pad_end[None, :] <= blk_row).astype(I32), axis=1), N_EXPERTS - 1)
    blk_e = blk_e + lw["layer"] * N_EXPERTS
    n_active = (pad_end[-1:] // MOE_BLOCK).astype(I32)
    rows = _slot_rows(pad_start, er)
    xs = _dispatch(x1, rows, (pad_start + counts).astype(I32), pad_end.astype(I32), n_rows)
    ys = _experts(xs, blk_e, n_active, shared["w1"], shared["w3"], shared["w2"])
    return _combine_ln(x1, ys, rows, gt, lw["ln2_g"], lw["ln2_b"])


def _trunk(x, layers, shared):
    b, s, d = x.shape
    t = b * s
    tables = _rope_tables(s)
    xf = x.reshape(t, d)
    xb = xf.astype(BF16)
    for l, lw in enumerate(layers):
        gains = lw["qk_gains"]
        gates = _gate_proj(xb, lw["w_gate"], lw["b_gate"])
        a_parts = [_attn_a(_project(xb, lw["w_a"][g], tables, gains, s, A_TN, A_TILE_KINDS, dil, f"proj_a{g}"))
                   for g, (_, dil) in enumerate(A_CONFIGS)]
        (o0, l0), (o1, l1), (o2, l2) = a_parts
        oa = _merge_a(o0.reshape(t, A_GROUP_COLS), l0.reshape(t, LANE), o1, l1, o2, l2, s)
        qkv_b = _project(xb, lw["w_b"], tables, gains, s, B_TN, B_TILE_KINDS, 1, "proj_b")
        ob = _attn_b(qkv_b.reshape(b, s, B_COLS))
        lambda_init = 0.8 - 0.6 * float(np.exp(-0.3 * l))
        qkv_c = _project(xb, lw["w_c"], tables, gains, s, C_TN, C_TILE_KINDS, 1, "proj_c")
        oc = _attn_c(qkv_c.reshape(b, s, C_COLS), lw["lam_rows"], lw["subln_g"], lambda_init)
        merged = _branch_proj(oa, ob, oc, gates, lw["w_proj_a"], lw["w_proj_b"], lw["w_proj_c"])
        x1 = _out_proj_ln(merged, lw["w_out"], xf, lw["ln1_g"], lw["ln1_b"])
        xf, xb = _moe(x1, lw, shared)
    return xf.reshape(b, s, d)


def kernel(x_prompt, x_sample, w_in, q_norm_g, k_norm_g, lambda_q1, lambda_k1, lambda_q2, lambda_k2, subln_g,
           w_gate, b_gate, w_proj_a, w_proj_b, w_proj_c, w_out, ln1_g, ln1_b, w_router, router_bias,
           w1, w3, w2, ln2_g, ln2_b):
    assert [dil for _, dil in A_CONFIGS] == [1, 4, 16] and all(w // (2 * dil) == A_RADIUS for w, dil in A_CONFIGS)
    depth = w_in.shape[0]
    layers = []
    part = A_HEADS * HEAD_DIM
    for l in range(depth):
        w_l = w_in[l].astype(BF16)
        w_a = [jnp.concatenate([w_l[:, p * part + g * A_GROUP_COLS: p * part + (g + 1) * A_GROUP_COLS] for p in range(3)],
                               axis=1) for g in range(len(A_CONFIGS))]
        layers.append(dict(
            w_a=w_a, w_b=w_l[:, A_COLS:A_COLS + B_COLS], w_c=w_l[:, A_COLS + B_COLS:],
            qk_gains=jnp.stack([q_norm_g[l], k_norm_g[l]]).astype(F32),
            lam_rows=jnp.stack([lambda_q1[l], lambda_k1[l], lambda_q2[l], lambda_k2[l]]).astype(F32),
            subln_g=subln_g[l].reshape(1, -1).astype(F32),
            w_gate=w_gate[l].astype(BF16),
            b_gate=b_gate[l].reshape(1, -1).astype(F32),
            w_proj_a=w_proj_a[l].astype(BF16),
            w_proj_b=w_proj_b[l].astype(BF16),
            w_proj_c=w_proj_c[l].astype(BF16),
            w_out=w_out[l].astype(BF16),
            ln1_g=ln1_g[l].reshape(1, -1).astype(F32),
            ln1_b=ln1_b[l].reshape(1, -1).astype(F32),
            layer=l,
            ln2_g=ln2_g[l].reshape(1, -1).astype(F32),
            ln2_b=ln2_b[l].reshape(1, -1).astype(F32),
        ))
    wr_t = w_router.astype(F32).T
    wr_hi = wr_t.astype(BF16)
    wr_lo = (wr_t - wr_hi.astype(F32)).astype(BF16)
    def stacked(w):
        return w.astype(BF16).reshape((depth * N_EXPERTS,) + w.shape[2:])

    shared = dict(wr_hi=wr_hi, wr_lo=wr_lo, r_bias=router_bias.astype(F32).reshape(-1, 1),
                  w1=stacked(w1), w3=stacked(w3), w2=stacked(w2))
    return (_trunk(x_prompt, layers, shared), _trunk(x_sample, layers, shared))
```

```python
import functools

import numpy as np
import jax
import jax.numpy as jnp
from jax import lax
from jax.experimental import pallas as pl
from jax.experimental.pallas import tpu as pltpu

F32 = jnp.float32
BF16 = jnp.bfloat16
I32 = jnp.int32

DEPTH = 2
HEAD_DIM = 128
GRID_W = 64
ROPE_THETA = 10000.0
A_CONFIGS = ((128, 1), (512, 4), (2048, 16))
A_HEADS_PER_GROUP = 6
A_HEADS = A_HEADS_PER_GROUP * len(A_CONFIGS)
A_GROUP_COLS = A_HEADS_PER_GROUP * HEAD_DIM
A_OUT = A_GROUP_COLS
B_Q_HEADS = 8
B_KV_HEADS = 2
B_GROUP = B_Q_HEADS // B_KV_HEADS
B_OUT = B_Q_HEADS * HEAD_DIM
C_HEADS = 4
C_OUT = C_HEADS * 2 * HEAD_DIM
N_BRANCH = 3
N_EXPERTS = 16
N_EXPERT_GROUPS = 4
EXPERTS_PER_GROUP = N_EXPERTS // N_EXPERT_GROUPS
TOP_K = 2
DEEPNORM_ALPHA = (2 * DEPTH) ** 0.25
LN_EPS = 1e-5
RMS_EPS = 1e-6
NEG_INF = -1e30
LOG2E = 1.4426950408889634
LN2 = 0.6931471805599453
ATTN_SCALE = HEAD_DIM ** -0.5 * LOG2E
A_RADIUS = 64

HPG = A_HEADS_PER_GROUP
B_Q0, B_K0, B_V0 = 0, B_Q_HEADS, B_Q_HEADS + B_KV_HEADS
B_COLS = (B_Q_HEADS + 2 * B_KV_HEADS) * HEAD_DIM
C_COLS = 6 * C_HEADS * HEAD_DIM
A_COLS = 3 * A_HEADS * HEAD_DIM
K_PLAIN, K_ROPE, K_ROPE_Q, K_AXIAL_Q, K_AXIAL_K = range(5)

LANE = 128
QKV_TM = 1024
A_TN, B_TN, C_TN = A_GROUP_COLS, 4 * LANE, 8 * LANE
PROJ_CHUNK_HEADS = 2
GATE_TM, GATE_TN = 1024, 1024
PROJ_TM, PROJ_TN = 1024, 512
OUT_TM = 512
OUT_SPLIT = 2
ROUTER_TM = 512
MOE_BLOCK = 256
COMBINE_TM = 512
DISPATCH_TM = 1024
B_TQ = 512
SLOT_TM = 2048
DMA_UNROLL = 8
ATTN_TQ = 256
A_UNITS_PER_STEP = 2
C_HEADS_PER_STEP = 4
MERGE_TM = 1024
VMEM_LIMIT = 56 * 1024 * 1024
ROW_E, ROW_RANK, ROW_GATE = 0, 2, 4


def _params(sem, vmem=VMEM_LIMIT):
    return pltpu.CompilerParams(dimension_semantics=sem, vmem_limit_bytes=vmem)


def _tile(n, t):
    t = min(n, t)
    assert n % t == 0, (n, t)
    return t


def _rope_head(xh, c1_ref, s1_ref):
    return xh * c1_ref[...] + pltpu.roll(xh, 64, 1) * s1_ref[...]


def _head_epilogue(kind, xh, c1_ref, s1_ref, cax_ref, sa_ref, sb_ref, g_ref):
    if kind == K_PLAIN:
        return xh
    if kind in (K_ROPE, K_ROPE_Q):
        y = _rope_head(xh, c1_ref, s1_ref)
    else:
        g = g_ref[0:1, :] if kind == K_AXIAL_Q else g_ref[1:2, :]
        sq = xh * xh
        sq_hi = sq.astype(BF16)
        sq_lo = (sq - sq_hi.astype(F32)).astype(BF16)
        ones = jnp.ones((LANE, LANE), BF16)
        ssum = jnp.dot(sq_hi, ones, preferred_element_type=F32) + jnp.dot(sq_lo, ones, preferred_element_type=F32)
        r = xh * lax.rsqrt(ssum * (1.0 / LANE) + RMS_EPS) * g
        y = r * cax_ref[...] + pltpu.roll(r, 96, 1) * sa_ref[...] + pltpu.roll(r, 32, 1) * sb_ref[...]
    return y * ATTN_SCALE if kind in (K_ROPE_Q, K_AXIAL_Q) else y


def _project_kernel(x_ref, w_ref, c1_ref, s1_ref, cax_ref, sa_ref, sb_ref, g_ref, o_ref, *scratch,
                    tile_kinds, dilation):
    j = pl.program_id(1)
    patterns = {}
    for idx, pattern in enumerate(tile_kinds):
        patterns.setdefault(pattern, []).append(idx)

    def emit(pattern):
        axial = any(kind in (K_AXIAL_Q, K_AXIAL_K) for kind in pattern)
        chunk = len(pattern) if axial else PROJ_CHUNK_HEADS
        for c0 in range(0, len(pattern), chunk):
            acc = jnp.dot(x_ref[...], w_ref[:, c0 * LANE:(c0 + chunk) * LANE], preferred_element_type=F32)
            for hh in range(chunk):
                h = c0 + hh
                cols = slice(h * LANE, (h + 1) * LANE)
                y = _head_epilogue(pattern[h], acc[:, hh * LANE:(hh + 1) * LANE],
                                   c1_ref, s1_ref, cax_ref, sa_ref, sb_ref, g_ref)
                if dilation == 1:
                    o_ref[0, 0, :, cols] = y.astype(o_ref.dtype)
                else:
                    scratch[0][h] = y

    for pattern, idxs in patterns.items():
        cond = functools.reduce(jnp.logical_or, [j == i for i in idxs])
        pl.when(cond)(functools.partial(emit, pattern))

    if dilation > 1:
        y_ref = scratch[0]
        rows = y_ref.shape[1] // dilation
        for c in range(dilation):
            for h in range(y_ref.shape[0]):
                o_ref[0, c, :, h * LANE:(h + 1) * LANE] = (
                    y_ref[h, pl.ds(c, rows, stride=dilation), :].astype(o_ref.dtype))


def _project(xb, w, tables, gains, seq, tn, tile_kinds, dilation, name):
    t, d = xb.shape
    n = w.shape[1]
    assert n == tn * len(tile_kinds) and all(len(p) * LANE == tn for p in tile_kinds)
    tm = _tile(seq, QKV_TM)
    nsb = seq // tm
    tab_spec = pl.BlockSpec((tm, LANE), lambda i, j: (i % nsb, 0))
    scratch = [] if dilation == 1 else [pltpu.VMEM((tn // LANE, tm, LANE), F32)]
    return pl.pallas_call(
        functools.partial(_project_kernel, tile_kinds=tuple(tile_kinds), dilation=dilation),
        out_shape=jax.ShapeDtypeStruct((t // seq, dilation, seq // dilation, n), BF16),
        grid=(t // tm, n // tn),
        in_specs=[pl.BlockSpec((tm, d), lambda i, j: (i, 0)),
                  pl.BlockSpec((d, tn), lambda i, j: (0, j)),
                  tab_spec, tab_spec, tab_spec, tab_spec, tab_spec,
                  pl.BlockSpec((2, LANE), lambda i, j: (0, 0))],
        out_specs=pl.BlockSpec((1, dilation, tm // dilation, tn), lambda i, j: (i // nsb, 0, i % nsb, j)),
        scratch_shapes=scratch,
        compiler_params=_params(("parallel", "arbitrary")),
        name=name,
    )(xb, w, *tables, gains)


A_TILE_KINDS = ((K_ROPE_Q,) * HPG, (K_ROPE,) * HPG, (K_PLAIN,) * HPG)
B_TILE_KINDS = ((K_AXIAL_Q,) * 4, (K_AXIAL_Q,) * 4, (K_AXIAL_K,) * B_KV_HEADS + (K_PLAIN,) * B_KV_HEADS)
C_TILE_KINDS = ((K_ROPE_Q,) * 8, (K_ROPE,) * 8, (K_PLAIN,) * 8)


def _gate_kernel(x_ref, w_ref, b_ref, o_ref):
    acc = jnp.dot(x_ref[...], w_ref[...], preferred_element_type=F32) + b_ref[...]
    o_ref[...] = jax.nn.sigmoid(acc).astype(o_ref.dtype)


def _gate_proj(xb, w, b):
    t, d = xb.shape
    n = w.shape[1]
    tm, tn = _tile(t, GATE_TM), _tile(n, GATE_TN)
    return pl.pallas_call(
        _gate_kernel,
        out_shape=jax.ShapeDtypeStruct((t, n), BF16),
        grid=(t // tm, n // tn),
        in_specs=[pl.BlockSpec((tm, d), lambda i, j: (i, 0)),
                  pl.BlockSpec((d, tn), lambda i, j: (0, j)),
                  pl.BlockSpec((1, tn), lambda i, j: (0, j))],
        out_specs=pl.BlockSpec((tm, tn), lambda i, j: (i, j)),
        compiler_params=_params(("parallel", "arbitrary")),
        name="gate_proj",
    )(xb, w, b)


def _attn_a_kernel(q_ref, k_ref, v_ref, o_ref, lse_ref, *, tq, win, sub_len):
    n_res, n_qb = q_ref.shape[1], q_ref.shape[2] // tq
    lane = lax.broadcasted_iota(I32, (tq, LANE), 1)
    for qb in range(n_qb):
        q0 = (pl.program_id(2) * n_qb + qb) * tq
        start = jnp.clip(q0 - A_RADIUS, 0, sub_len - win)
        if win % 16 == 0 and tq % 64 == 0:
            start = pl.multiple_of(start, 16)
        q_pos = q0 + lax.broadcasted_iota(I32, (tq, win), 0)
        k_pos = start + lax.broadcasted_iota(I32, (tq, win), 1)
        bias = jnp.where(jnp.abs(k_pos - q_pos) <= A_RADIUS, 0.0, NEG_INF).astype(F32)
        rows = slice(qb * tq, (qb + 1) * tq)
        for c in range(n_res):
            lse_all = jnp.zeros((tq, LANE), F32)
            for h in range(HPG):
                cols = slice(h * LANE, (h + 1) * LANE)
                q = q_ref[0, c, rows, cols]
                k = k_ref[0, c, pl.ds(start, win), cols]
                v = v_ref[0, c, pl.ds(start, win), cols]
                s = lax.dot_general(q, k, (((1,), (1,)), ((), ())), preferred_element_type=F32) + bias
                m = jnp.max(s, axis=-1, keepdims=True)
                p = jnp.exp2(s - m).astype(BF16)
                ov = jnp.dot(p, jnp.concatenate([v, jnp.ones_like(v)], axis=1), preferred_element_type=F32)
                den = ov[:, LANE:]
                o_ref[0, c, rows, cols] = (ov[:, :LANE] / den).astype(o_ref.dtype)
                lse_all = jnp.where(lane == h, (m + jnp.log2(den)) * LN2, lse_all)
            lse_ref[0, c, rows, :] = lse_all


def _attn_a(qkv4):
    b, r, sub_len, _ = qkv4.shape
    tq = _tile(sub_len, ATTN_TQ)
    win = min(tq + 2 * A_RADIUS, sub_len)
    units = A_UNITS_PER_STEP * (ATTN_TQ // tq)
    n_qb = min(units, sub_len // tq)
    n_res = min(max(units // n_qb, 1), r)
    assert (sub_len // tq) % n_qb == 0 and r % n_res == 0
    return pl.pallas_call(
        functools.partial(_attn_a_kernel, tq=tq, win=win, sub_len=sub_len),
        out_shape=(jax.ShapeDtypeStruct((b, r, sub_len, A_GROUP_COLS), BF16),
                   jax.ShapeDtypeStruct((b, r, sub_len, LANE), F32)),
        grid=(b, r // n_res, sub_len // (tq * n_qb)),
        in_specs=[pl.BlockSpec((1, n_res, tq * n_qb, A_GROUP_COLS), lambda bi, c, qi: (bi, c, qi, 0)),
                  pl.BlockSpec((1, n_res, sub_len, A_GROUP_COLS), lambda bi, c, qi: (bi, c, 0, 1)),
                  pl.BlockSpec((1, n_res, sub_len, A_GROUP_COLS), lambda bi, c, qi: (bi, c, 0, 2))],
        out_specs=(pl.BlockSpec((1, n_res, tq * n_qb, A_GROUP_COLS), lambda bi, c, qi: (bi, c, qi, 0)),
                   pl.BlockSpec((1, n_res, tq * n_qb, LANE), lambda bi, c, qi: (bi, c, qi, 0))),
        compiler_params=_params(("parallel", "parallel", "arbitrary")),
        name=f"attn_a_r{r}",
    )(qkv4, qkv4, qkv4)


def _merge_a_kernel(o0_ref, l0_ref, o1_ref, l1_ref, o2_ref, l2_ref, o_ref, so1, sl1, so2, sl2):
    def to_token_order(src_ref, dst_ref):
        r, rows = src_ref.shape[1], src_ref.shape[2]
        for c in range(r):
            for h in range(dst_ref.shape[0]):
                dst_ref[h, pl.ds(c, rows, stride=r), :] = src_ref[0, c, :, h * LANE:(h + 1) * LANE].astype(F32)

    to_token_order(o1_ref, so1)
    to_token_order(l1_ref, sl1)
    to_token_order(o2_ref, so2)
    to_token_order(l2_ref, sl2)
    l0, l1, l2 = l0_ref[...], sl1[0], sl2[0]
    m = jnp.maximum(jnp.maximum(l0, l1), l2)
    e0, e1, e2 = jnp.exp(l0 - m), jnp.exp(l1 - m), jnp.exp(l2 - m)
    den = e0 + e1 + e2
    w0, w1, w2 = e0 / den, e1 / den, e2 / den
    for h in range(HPG):
        cols = slice(h * LANE, (h + 1) * LANE)
        acc = (o0_ref[:, cols].astype(F32) * w0[:, h:h + 1]
               + so1[h] * w1[:, h:h + 1]
               + so2[h] * w2[:, h:h + 1])
        o_ref[:, cols] = acc.astype(o_ref.dtype)


def _merge_a(o0, l0, o1, l1, o2, l2, seq):
    t = o0.shape[0]
    tm = _tile(seq, MERGE_TM)
    nsb = seq // tm

    def res_spec(arr):
        r, n = arr.shape[1], arr.shape[3]
        return pl.BlockSpec((1, r, tm // r, n), lambda i: (i // nsb, 0, i % nsb, 0))

    return pl.pallas_call(
        _merge_a_kernel,
        out_shape=jax.ShapeDtypeStruct((t, A_OUT), BF16),
        grid=(t // tm,),
        in_specs=[pl.BlockSpec((tm, A_GROUP_COLS), lambda i: (i, 0)), pl.BlockSpec((tm, LANE), lambda i: (i, 0)),
                  res_spec(o1), res_spec(l1), res_spec(o2), res_spec(l2)],
        out_specs=pl.BlockSpec((tm, A_OUT), lambda i: (i, 0)),
        scratch_shapes=[pltpu.VMEM((HPG, tm, LANE), F32), pltpu.VMEM((1, tm, LANE), F32),
                        pltpu.VMEM((HPG, tm, LANE), F32), pltpu.VMEM((1, tm, LANE), F32)],
        compiler_params=_params(("parallel",)),
        name="merge_a",
    )(o0, l0, o1, l1, o2, l2)


def _softmax_pv(q, k, v):
    s = lax.dot_general(q, k, (((1,), (1,)), ((), ())), preferred_element_type=F32)
    m = jnp.max(s, axis=-1, keepdims=True)
    p = jnp.exp2(s - m)
    den = jnp.sum(p, axis=-1, keepdims=True)
    return jnp.dot(p.astype(BF16), v, preferred_element_type=F32) / den


def _attn_b_kernel(q_ref, k_ref, v_ref, o_ref):
    for h in range(B_KV_HEADS):
        k = k_ref[0, :, h * LANE:(h + 1) * LANE]
        v = v_ref[0, :, h * LANE:(h + 1) * LANE]
        v1 = jnp.concatenate([v, jnp.ones_like(v)], axis=1)
        for g in range(B_GROUP):
            cols = slice((h * B_GROUP + g) * LANE, (h * B_GROUP + g + 1) * LANE)
            s = lax.dot_general(q_ref[0, :, cols], k, (((1,), (1,)), ((), ())), preferred_element_type=F32)
            p = jnp.exp2(s - jnp.max(s, axis=-1, keepdims=True)).astype(BF16)
            ov = jnp.dot(p, v1, preferred_element_type=F32)
            o_ref[0, :, cols] = (ov[:, :LANE] / ov[:, LANE:]).astype(o_ref.dtype)


def _attn_b(qkv3):
    b, s, _ = qkv3.shape
    tq = _tile(s, B_TQ)
    kv_cols = B_KV_HEADS * LANE
    o = pl.pallas_call(
        _attn_b_kernel,
        out_shape=jax.ShapeDtypeStruct((b, s, B_OUT), BF16),
        grid=(b, s // tq),
        in_specs=[pl.BlockSpec((1, tq, B_OUT), lambda bi, qi: (bi, qi, 0)),
                  pl.BlockSpec((1, s, kv_cols), lambda bi, qi: (bi, 0, B_K0 * LANE // kv_cols)),
                  pl.BlockSpec((1, s, kv_cols), lambda bi, qi: (bi, 0, B_V0 * LANE // kv_cols))],
        out_specs=pl.BlockSpec((1, tq, B_OUT), lambda bi, qi: (bi, qi, 0)),
        compiler_params=_params(("parallel", "arbitrary")),
        name="attn_b",
    )(qkv3, qkv3, qkv3)
    return o.reshape(b * s, B_OUT)


def _attn_c_kernel(q_ref, k_ref, v_ref, lam_ref, g_ref, o_ref, *, lambda_init):
    lam_p = lam_ref[...].astype(F32)
    lam = (jnp.exp(jnp.sum(lam_p[0:1] * lam_p[1:2], axis=-1, keepdims=True))
           - jnp.exp(jnp.sum(lam_p[2:3] * lam_p[3:4], axis=-1, keepdims=True)) + lambda_init)
    for hh in range(C_HEADS_PER_STEP):
        c0 = hh * 2 * LANE
        first, second = slice(c0, c0 + LANE), slice(c0 + LANE, c0 + 2 * LANE)
        v = v_ref[0, :, c0:c0 + 2 * LANE]
        o = (_softmax_pv(q_ref[0, :, first], k_ref[0, :, first], v)
             - lam * _softmax_pv(q_ref[0, :, second], k_ref[0, :, second], v))
        o = o * lax.rsqrt(jnp.mean(o * o, axis=-1, keepdims=True) + RMS_EPS) * g_ref[...].astype(F32)
        o_ref[0, :, c0:c0 + 2 * LANE] = (o * (1.0 - lambda_init)).astype(o_ref.dtype)


def _attn_c(qkv3, lam_rows, subln_g, lambda_init):
    b, s, _ = qkv3.shape
    tq = _tile(s, ATTN_TQ)
    cols = C_HEADS_PER_STEP * 2 * LANE
    steps = C_HEADS // C_HEADS_PER_STEP
    o = pl.pallas_call(
        functools.partial(_attn_c_kernel, lambda_init=lambda_init),
        out_shape=jax.ShapeDtypeStruct((b, s, C_OUT), BF16),
        grid=(b, steps, s // tq),
        in_specs=[pl.BlockSpec((1, tq, cols), lambda bi, h, qi: (bi, qi, h)),
                  pl.BlockSpec((1, s, cols), lambda bi, h, qi: (bi, 0, steps + h)),
                  pl.BlockSpec((1, s, cols), lambda bi, h, qi: (bi, 0, 2 * steps + h)),
                  pl.BlockSpec((4, LANE), lambda bi, h, qi: (0, 0)),
                  pl.BlockSpec((1, 2 * LANE), lambda bi, h, qi: (0, 0))],
        out_specs=pl.BlockSpec((1, tq, cols), lambda bi, h, qi: (bi, qi, h)),
        compiler_params=_params(("parallel", "parallel", "arbitrary")),
        name="attn_c",
    )(qkv3, qkv3, qkv3, lam_rows, subln_g)
    return o.reshape(b * s, C_OUT)


def _branch_proj_kernel(oa_ref, ob_ref, oc_ref, g0_ref, g1_ref, g2_ref, wa_ref, wb_ref, wc_ref, o_ref):
    acc = g0_ref[...].astype(F32) * jnp.dot(oa_ref[...], wa_ref[...], preferred_element_type=F32)
    acc = acc + g1_ref[...].astype(F32) * jnp.dot(ob_ref[...], wb_ref[...], preferred_element_type=F32)
    acc = acc + g2_ref[...].astype(F32) * jnp.dot(oc_ref[...], wc_ref[...], preferred_element_type=F32)
    o_ref[...] = acc.astype(o_ref.dtype)


def _branch_proj(oa, ob, oc, gates, wa, wb, wc):
    t = oa.shape[0]
    d = wa.shape[1]
    tm, tn = _tile(t, PROJ_TM), _tile(d, PROJ_TN)
    nb = d // tn
    gspecs = [pl.BlockSpec((tm, tn), functools.partial(lambda i, j, br: (i, br * nb + j), br=br)) for br in range(N_BRANCH)]
    return pl.pallas_call(
        _branch_proj_kernel,
        out_shape=jax.ShapeDtypeStruct((t, d), BF16),
        grid=(t // tm, nb),
        in_specs=[pl.BlockSpec((tm, A_OUT), lambda i, j: (i, 0)),
                  pl.BlockSpec((tm, B_OUT), lambda i, j: (i, 0)),
                  pl.BlockSpec((tm, C_OUT), lambda i, j: (i, 0))] + gspecs +
                 [pl.BlockSpec((A_OUT, tn), lambda i, j: (0, j)),
                  pl.BlockSpec((B_OUT, tn), lambda i, j: (0, j)),
                  pl.BlockSpec((C_OUT, tn), lambda i, j: (0, j))],
        out_specs=pl.BlockSpec((tm, tn), lambda i, j: (i, j)),
        compiler_params=_params(("parallel", "arbitrary")),
        name="branch_proj",
    )(oa, ob, oc, gates, gates, gates, wa, wb, wc)


def _layer_norm(z, g, b):
    mu = jnp.mean(z, axis=-1, keepdims=True)
    zc = z - mu
    var = jnp.mean(zc * zc, axis=-1, keepdims=True)
    return zc * lax.rsqrt(var + LN_EPS) * g + b


def _out_ln_kernel(m_ref, w_ref, x_ref, g_ref, b_ref, o_ref):
    part = m_ref.shape[0] // OUT_SPLIT
    for r in range(OUT_SPLIT):
        rows = slice(r * part, (r + 1) * part)
        mix = jnp.dot(m_ref[rows, :], w_ref[...], preferred_element_type=F32)
        o_ref[rows, :] = _layer_norm(DEEPNORM_ALPHA * x_ref[rows, :] + mix, g_ref[...], b_ref[...])


def _out_proj_ln(merged, w_out, x, g, b):
    t, d = x.shape
    tm = _tile(t, OUT_TM)
    row = pl.BlockSpec((tm, d), lambda i: (i, 0))
    vec = pl.BlockSpec((1, d), lambda i: (0, 0))
    return pl.pallas_call(
        _out_ln_kernel,
        out_shape=jax.ShapeDtypeStruct((t, d), F32),
        grid=(t // tm,),
        in_specs=[row, pl.BlockSpec((d, d), lambda i: (0, 0)), row, vec, vec],
        out_specs=row,
        compiler_params=_params(("parallel",)),
        name="out_proj_ln",
    )(merged, w_out, x, g, b)


def _router_kernel(x_ref, whi_ref, wlo_ref, bias_ref, er_ref, gt_ref, cnt_ref, carry_ref, *, tm):
    @pl.when(pl.program_id(0) == 0)
    def _():
        carry_ref[...] = jnp.zeros_like(carry_ref)

    x = x_ref[...]
    x_hi = x.astype(BF16)
    x_lo = (x - x_hi.astype(F32)).astype(BF16)
    nt = (((1,), (1,)), ((), ()))
    logits = (lax.dot_general(whi_ref[...], x_hi, nt, preferred_element_type=F32)
              + lax.dot_general(wlo_ref[...], x_hi, nt, preferred_element_type=F32)
              + lax.dot_general(whi_ref[...], x_lo, nt, preferred_element_type=F32))
    scores = jax.nn.sigmoid(logits)
    biased = scores + bias_ref[...]

    def row(a, e):
        return a[e:e + 1, :]

    gscore = []
    for g in range(N_EXPERT_GROUPS):
        a, b, c, d = (row(biased, g * EXPERTS_PER_GROUP + i) for i in range(EXPERTS_PER_GROUP))
        hi1, lo1, hi2, lo2 = jnp.maximum(a, b), jnp.minimum(a, b), jnp.maximum(c, d), jnp.minimum(c, d)
        gscore.append(jnp.maximum(hi1, hi2) + jnp.maximum(jnp.minimum(hi1, hi2), jnp.maximum(lo1, lo2)))
    gsel = jnp.zeros((1, tm), I32)
    best = gscore[0]
    for g in range(1, N_EXPERT_GROUPS):
        better = gscore[g] > best
        gsel = jnp.where(better, g, gsel)
        best = jnp.where(better, gscore[g], best)

    def pick(a, i):
        out = row(a, i)
        for g in range(1, N_EXPERT_GROUPS):
            out = jnp.where(gsel == g, row(a, g * EXPERTS_PER_GROUP + i), out)
        return out

    bv = [pick(biased, i) for i in range(EXPERTS_PER_GROUP)]
    sv = [pick(scores, i) for i in range(EXPERTS_PER_GROUP)]
    i0 = jnp.zeros((1, tm), I32)
    b0 = bv[0]
    for i in range(1, EXPERTS_PER_GROUP):
        better = bv[i] > b0
        i0 = jnp.where(better, i, i0)
        b0 = jnp.where(better, bv[i], b0)
    i1 = jnp.full((1, tm), -1, I32)
    b1 = jnp.full((1, tm), -jnp.inf, F32)
    for i in range(EXPERTS_PER_GROUP):
        better = (i0 != i) & ((i1 < 0) | (bv[i] > b1))
        i1 = jnp.where(better, i, i1)
        b1 = jnp.where(better, bv[i], b1)
    s0 = jnp.zeros((1, tm), F32)
    s1 = jnp.zeros((1, tm), F32)
    for i in range(EXPERTS_PER_GROUP):
        s0 = jnp.where(i0 == i, sv[i], s0)
        s1 = jnp.where(i1 == i, sv[i], s1)
    e0 = gsel * EXPERTS_PER_GROUP + i0
    e1 = gsel * EXPERTS_PER_GROUP + i1
    den = s0 + s1
    g0, g1 = s0 / den, s1 / den

    erow = lax.broadcasted_iota(I32, (N_EXPERTS, tm), 0)
    member = ((erow == e0) | (erow == e1))
    tri = (lax.broadcasted_iota(I32, (tm, tm), 0) < lax.broadcasted_iota(I32, (tm, tm), 1))
    prefix = jnp.dot(member.astype(BF16), tri.astype(BF16), preferred_element_type=F32) + carry_ref[:, 0:1]
    r0 = jnp.sum(jnp.where(erow == e0, prefix, 0.0), axis=0, keepdims=True)
    r1 = jnp.sum(jnp.where(erow == e1, prefix, 0.0), axis=0, keepdims=True)
    carry_ref[...] = carry_ref[...] + jnp.sum(member.astype(F32), axis=1, keepdims=True)
    cnt_ref[...] = carry_ref[...]

    zi = jnp.zeros((1, tm), I32)
    er_ref[...] = jnp.concatenate([e0, e1, r0.astype(I32), r1.astype(I32), zi, zi, zi, zi], axis=0)
    zf = jnp.zeros((1, tm), F32)
    rec = jnp.concatenate([zf, zf, zf, zf, g0, g1, zf, zf, jnp.zeros((LANE - 8, tm), F32)], axis=0)
    gt_ref[...] = rec.T


def _router(x, w_hi_t, w_lo_t, bias_col):
    t, d = x.shape
    tm = _tile(t, ROUTER_TM)
    return pl.pallas_call(
        functools.partial(_router_kernel, tm=tm),
        out_shape=(jax.ShapeDtypeStruct((8, t), I32), jax.ShapeDtypeStruct((t, LANE), F32),
                   jax.ShapeDtypeStruct((N_EXPERTS, LANE), F32)),
        grid=(t // tm,),
        in_specs=[pl.BlockSpec((tm, d), lambda i: (i, 0)),
                  pl.BlockSpec((N_EXPERTS, d), lambda i: (0, 0)),
                  pl.BlockSpec((N_EXPERTS, d), lambda i: (0, 0)),
                  pl.BlockSpec((N_EXPERTS, 1), lambda i: (0, 0))],
        out_specs=(pl.BlockSpec((8, tm), lambda i: (0, i)),
                   pl.BlockSpec((tm, LANE), lambda i: (i, 0)),
                   pl.BlockSpec((N_EXPERTS, LANE), lambda i: (0, 0))),
        scratch_shapes=[pltpu.VMEM((N_EXPERTS, LANE), F32)],
        compiler_params=_params(("arbitrary",)),
        name="router",
    )(x, w_hi_t, w_lo_t, bias_col)


def _slot_rows_kernel(pad_start_ref, er_ref, o_ref):
    e = er_ref[ROW_E:ROW_E + TOP_K, :]
    base = jnp.zeros_like(e)
    for x in range(N_EXPERTS):
        base = jnp.where(e == x, pad_start_ref[x], base)
    rows = base + er_ref[ROW_RANK:ROW_RANK + TOP_K, :]
    o_ref[...] = jnp.concatenate([rows, jnp.zeros((8 - TOP_K, rows.shape[1]), I32)], axis=0)


def _slot_rows(pad_start, er):
    t = er.shape[1]
    tm = _tile(t, SLOT_TM)
    grid_spec = pltpu.PrefetchScalarGridSpec(
        num_scalar_prefetch=1,
        grid=(t // tm,),
        in_specs=[pl.BlockSpec((8, tm), lambda i, ps: (0, i))],
        out_specs=pl.BlockSpec((8, tm), lambda i, ps: (0, i)),
    )
    return pl.pallas_call(
        _slot_rows_kernel,
        out_shape=jax.ShapeDtypeStruct((8, t), I32),
        grid_spec=grid_spec,
        compiler_params=_params(("arbitrary",)),
        name="moe_slot_rows",
    )(pad_start, er)


def _dispatch_kernel(fill_end_ref, pad_end_ref, rows_ref, x_ref, xs_hbm, zblk, sem, zsem, *, tm):
    def row_copy(r, dst_row):
        return pltpu.make_async_copy(x_ref.at[pl.ds(r, 1), :], xs_hbm.at[pl.ds(dst_row, 1), :], sem)

    def issue(r, c):
        row_copy(r, rows_ref[0, r]).start()
        row_copy(r, rows_ref[1, r]).start()
        return c

    lax.fori_loop(0, tm, issue, 0, unroll=DMA_UNROLL)

    def zero_row(dst_row):
        return pltpu.make_async_copy(zblk.at[pl.ds(0, 1), :], xs_hbm.at[pl.ds(dst_row, 1), :], zsem)

    def zero_block(blk):
        dst = xs_hbm.at[pl.ds(pl.multiple_of(blk * MOE_BLOCK, MOE_BLOCK), MOE_BLOCK), :]
        return pltpu.make_async_copy(zblk, dst, zsem)

    @pl.when(pl.program_id(0) == 0)
    def _():
        zblk[...] = jnp.zeros_like(zblk)
        for e in range(N_EXPERTS):
            lo, hi = fill_end_ref[e], pad_end_ref[e]
            lax.fori_loop(lo, hi, lambda r, c: (zero_row(r).start(), c)[1], 0)
            lax.fori_loop(lo, hi, lambda r, c: (zero_row(0).wait(), c)[1], 0)
        first, last = pad_end_ref[N_EXPERTS - 1] // MOE_BLOCK, xs_hbm.shape[0] // MOE_BLOCK
        lax.fori_loop(first, last, lambda blk, c: (zero_block(blk).start(), c)[1], 0)
        lax.fori_loop(first, last, lambda blk, c: (zero_block(0).wait(), c)[1], 0)

    def drain(r, c):
        row_copy(0, 0).wait()
        row_copy(0, 0).wait()
        return c

    lax.fori_loop(0, tm, drain, 0, unroll=DMA_UNROLL)


def _dispatch(x, rows, fill_end, pad_end, n_rows):
    t, d = x.shape
    tm = _tile(t, DISPATCH_TM)
    grid_spec = pltpu.PrefetchScalarGridSpec(
        num_scalar_prefetch=2,
        grid=(t // tm,),
        in_specs=[pl.BlockSpec((8, tm), lambda i, fe, pe: (0, i), memory_space=pltpu.SMEM),
                  pl.BlockSpec((tm, d), lambda i, fe, pe: (i, 0))],
        out_specs=pl.BlockSpec(memory_space=pl.ANY),
        scratch_shapes=[pltpu.VMEM((MOE_BLOCK, d), F32), pltpu.SemaphoreType.DMA, pltpu.SemaphoreType.DMA],
    )
    return pl.pallas_call(
        functools.partial(_dispatch_kernel, tm=tm),
        out_shape=jax.ShapeDtypeStruct((n_rows, d), F32),
        grid_spec=grid_spec,
        compiler_params=_params(("arbitrary",)),
        name="moe_dispatch",
    )(fill_end, pad_end, rows, x)


def _expert_kernel(blk_e_ref, nact_ref, x_ref, w1_ref, w3_ref, w2_ref, o_ref):
    del blk_e_ref

    @pl.when(pl.program_id(0) < nact_ref[0])
    def _():
        x = x_ref[...].astype(BF16)
        h1 = jnp.dot(x, w1_ref[0], preferred_element_type=F32)
        h3 = jnp.dot(x, w3_ref[0], preferred_element_type=F32)
        h = (h1 * jax.nn.sigmoid(h1) * h3).astype(BF16)
        o_ref[...] = jnp.dot(h, w2_ref[0], preferred_element_type=F32)

    @pl.when(pl.program_id(0) >= nact_ref[0])
    def _():
        o_ref[...] = jnp.zeros_like(o_ref)


def _experts(xs, blk_e, n_active, w1, w3, w2):
    p, d = xs.shape
    f = w1.shape[2]
    nb = p // MOE_BLOCK
    grid_spec = pltpu.PrefetchScalarGridSpec(
        num_scalar_prefetch=2,
        grid=(nb,),
        in_specs=[pl.BlockSpec((MOE_BLOCK, d), lambda i, be, na: (jnp.minimum(i, na[0] - 1), 0)),
                  pl.BlockSpec((1, d, f), lambda i, be, na: (be[i], 0, 0)),
                  pl.BlockSpec((1, d, f), lambda i, be, na: (be[i], 0, 0)),
                  pl.BlockSpec((1, f, d), lambda i, be, na: (be[i], 0, 0))],
        out_specs=pl.BlockSpec((MOE_BLOCK, d), lambda i, be, na: (i, 0)),
    )
    return pl.pallas_call(
        _expert_kernel,
        out_shape=jax.ShapeDtypeStruct((p, d), F32),
        grid_spec=grid_spec,
        compiler_params=_params(("arbitrary",)),
        name="moe_experts",
    )(blk_e, n_active, xs, w1, w3, w2)


def _combine_kernel(rows_ref, next_rows_ref, x_ref, gt_ref, g_ref, b_ref, ys_hbm, o_ref, ob_ref, buf, sems, *, tm):
    i = pl.program_id(0)
    n = pl.num_programs(0)

    def row_copy(par, r, slot, src_row):
        return pltpu.make_async_copy(ys_hbm.at[pl.ds(src_row, 1), :], buf.at[par, slot, pl.ds(r, 1), :], sems.at[par])

    def gather(par, idx_ref):
        def issue(r, c):
            row_copy(par, r, 0, idx_ref[0, r]).start()
            row_copy(par, r, 1, idx_ref[1, r]).start()
            return c
        lax.fori_loop(0, tm, issue, 0, unroll=DMA_UNROLL)

    @pl.when(i == 0)
    def _():
        gather(0, rows_ref)

    @pl.when(i + 1 < n)
    def _():
        gather((i + 1) % 2, next_rows_ref)

    par = i % 2

    def drain(r, c):
        row_copy(par, 0, 0, 0).wait()
        row_copy(par, 0, 0, 0).wait()
        return c

    lax.fori_loop(0, tm, drain, 0, unroll=DMA_UNROLL)
    gt = gt_ref[...]
    ffn = gt[:, ROW_GATE:ROW_GATE + 1] * buf[par, 0] + gt[:, ROW_GATE + 1:ROW_GATE + 2] * buf[par, 1]
    y = _layer_norm(DEEPNORM_ALPHA * x_ref[...] + ffn, g_ref[...], b_ref[...])
    o_ref[...] = y
    ob_ref[...] = y.astype(ob_ref.dtype)


def _combine_ln(x, ys, rows, gt, g, b):
    t, d = x.shape
    tm = _tile(t, COMBINE_TM)
    n = t // tm
    row = pl.BlockSpec((tm, d), lambda i: (i, 0))
    vec = pl.BlockSpec((1, d), lambda i: (0, 0))
    return pl.pallas_call(
        functools.partial(_combine_kernel, tm=tm),
        out_shape=(jax.ShapeDtypeStruct((t, d), F32), jax.ShapeDtypeStruct((t, d), BF16)),
        grid=(n,),
        in_specs=[pl.BlockSpec((8, tm), lambda i: (0, i), memory_space=pltpu.SMEM),
                  pl.BlockSpec((8, tm), lambda i: (0, jnp.minimum(i + 1, n - 1)), memory_space=pltpu.SMEM),
                  row,
                  pl.BlockSpec((tm, LANE), lambda i: (i, 0)),
                  vec, vec,
                  pl.BlockSpec(memory_space=pl.ANY)],
        out_specs=(row, row),
        scratch_shapes=[pltpu.VMEM((2, TOP_K, tm, d), F32), pltpu.SemaphoreType.DMA((2,))],
        compiler_params=_params(("arbitrary",)),
        name="moe_combine_ln",
    )(rows, rows, x, gt, g, b, ys)


def _rope_tables(seq):
    def tab(pos, dim):
        inv = ROPE_THETA ** (-jnp.arange(0, dim, 2, dtype=F32) / dim)
        ang = pos.astype(F32)[:, None] * inv[None, :]
        return jnp.cos(ang), jnp.sin(ang)

    pos = jnp.arange(seq)
    cos1, sin1 = tab(pos, HEAD_DIM)
    cos_r, sin_r = tab(pos // GRID_W, HEAD_DIM // 2)
    cos_c, sin_c = tab(pos % GRID_W, HEAD_DIM // 2)
    z = jnp.zeros_like(sin_r)
    c1 = jnp.concatenate([cos1, cos1], axis=1)
    s1 = jnp.concatenate([-sin1, sin1], axis=1)
    cax = jnp.concatenate([cos_r, cos_r, cos_c, cos_c], axis=1)
    sa = jnp.concatenate([-sin_r, z, -sin_c, z], axis=1)
    sb = jnp.concatenate([z, sin_r, z, sin_c], axis=1)
    return c1, s1, cax, sa, sb


def _moe(x1, lw, shared):
    t, d = x1.shape
    er, gt, cnt = _router(x1, shared["wr_hi"], shared["wr_lo"], shared["r_bias"])
    counts = cnt[:, 0].astype(I32)
    padded = (counts + MOE_BLOCK - 1) // MOE_BLOCK * MOE_BLOCK
    pad_end = jnp.cumsum(padded)
    pad_start = (pad_end - padded).astype(I32)
    n_rows = t * TOP_K + N_EXPERTS * MOE_BLOCK
    nb = n_rows // MOE_BLOCK
    blk_row = jnp.arange(nb, dtype=I32)[:, None] * MOE_BLOCK
    blk_e = jnp.minimum(jnp.sum((pad_end[None, :] <= blk_row).astype(I32), axis=1), N_EXPERTS - 1)
    blk_e = blk_e + lw["layer"] * N_EXPERTS
    n_active = (pad_end[-1:] // MOE_BLOCK).astype(I32)
    rows = _slot_rows(pad_start, er)
    xs = _dispatch(x1, rows, (pad_start + counts).astype(I32), pad_end.astype(I32), n_rows)
    ys = _experts(xs, blk_e, n_active, shared["w1"], shared["w3"], shared["w2"])
    return _combine_ln(x1, ys, rows, gt, lw["ln2_g"], lw["ln2_b"])


def _trunk(x, layers, shared):
    b, s, d = x.shape
    t = b * s
    tables = _rope_tables(s)
    xf = x.reshape(t, d)
    xb = xf.astype(BF16)
    for l, lw in enumerate(layers):
        gains = lw["qk_gains"]
        gates = _gate_proj(xb, lw["w_gate"], lw["b_gate"])
        a_parts = [_attn_a(_project(xb, lw["w_a"][g], tables, gains, s, A_TN, A_TILE_KINDS, dil, f"proj_a{g}"))
                   for g, (_, dil) in enumerate(A_CONFIGS)]
        (o0, l0), (o1, l1), (o2, l2) = a_parts
        oa = _merge_a(o0.reshape(t, A_GROUP_COLS), l0.reshape(t, LANE), o1, l1, o2, l2, s)
        qkv_b = _project(xb, lw["w_b"], tables, gains, s, B_TN, B_TILE_KINDS, 1, "proj_b")
        ob = _attn_b(qkv_b.reshape(b, s, B_COLS))
        lambda_init = 0.8 - 0.6 * float(np.exp(-0.3 * l))
        qkv_c = _project(xb, lw["w_c"], tables, gains, s, C_TN, C_TILE_KINDS, 1, "proj_c")
        oc = _attn_c(qkv_c.reshape(b, s, C_COLS), lw["lam_rows"], lw["subln_g"], lambda_init)
        merged = _branch_proj(oa, ob, oc, gates, lw["w_proj_a"], lw["w_proj_b"], lw["w_proj_c"])
        x1 = _out_proj_ln(merged, lw["w_out"], xf, lw["ln1_g"], lw["ln1_b"])
        xf, xb = _moe(x1, lw, shared)
    return xf.reshape(b, s, d)


def kernel(x_prompt, x_sample, w_in, q_norm_g, k_norm_g, lambda_q1, lambda_k1, lambda_q2, lambda_k2, subln_g,
           w_gate, b_gate, w_proj_a, w_proj_b, w_proj_c, w_out, ln1_g, ln1_b, w_router, router_bias,
           w1, w3, w2, ln2_g, ln2_b):
    assert [dil for _, dil in A_CONFIGS] == [1, 4, 16] and all(w // (2 * dil) == A_RADIUS for w, dil in A_CONFIGS)
    depth = w_in.shape[0]
    layers = []
    part = A_HEADS * HEAD_DIM
    for l in range(depth):
        w_l = w_in[l].astype(BF16)
        w_a = [jnp.concatenate([w_l[:, p * part + g * A_GROUP_COLS: p * part + (g + 1) * A_GROUP_COLS] for p in range(3)],
                               axis=1) for g in range(len(A_CONFIGS))]
        layers.append(dict(
            w_a=w_a, w_b=w_l[:, A_COLS:A_COLS + B_COLS], w_c=w_l[:, A_COLS + B_COLS:],
            qk_gains=jnp.stack([q_norm_g[l], k_norm_g[l]]).astype(F32),
            lam_rows=jnp.stack([lambda_q1[l], lambda_k1[l], lambda_q2[l], lambda_k2[l]]).astype(F32),
            subln_g=subln_g[l].reshape(1, -1).astype(F32),
            w_gate=w_gate[l].astype(BF16),
            b_gate=b_gate[l].reshape(1, -1).astype(F32),
            w_proj_a=w_proj_a[l].astype(BF16),
            w_proj_b=w_proj_b[l].astype(BF16),
            w_proj_c=w_proj_c[l].astype(BF16),
            w_out=w_out[l].astype(BF16),
            ln1_g=ln1_g[l].reshape(1, -1).astype(F32),
            ln1_b=ln1_b[l].reshape(1, -1).astype(F32),
            layer=l,
            ln2_g=ln2_g[l].reshape(1, -1).astype(F32),
            ln2_b=ln2_b[l].reshape(1, -1).astype(F32),
        ))
    wr_t = w_router.astype(F32).T
    wr_hi = wr_t.astype(BF16)
    wr_lo = (wr_t - wr_hi.astype(F32)).astype(BF16)
    def stacked(w):
        return w.astype(BF16).reshape((depth * N_EXPERTS,) + w.shape[2:])

    shared = dict(wr_hi=wr_hi, wr_lo=wr_lo, r_bias=router_bias.astype(F32).reshape(-1, 1),
                  w1=stacked(w1), w3=stacked(w3), w2=stacked(w2))
    return (_trunk(x_prompt, layers, shared), _trunk(x_sample, layers, shared))
```

```python
import functools

import numpy as np
import jax
import jax.numpy as jnp
from jax import lax
from jax.experimental import pallas as pl
from jax.experimental.pallas import tpu as pltpu

F32 = jnp.float32
BF16 = jnp.bfloat16
I32 = jnp.int32

DEPTH = 2
HEAD_DIM = 128
GRID_W = 64
ROPE_THETA = 10000.0
A_CONFIGS = ((128, 1), (512, 4), (2048, 16))
A_HEADS_PER_GROUP = 6
A_HEADS = A_HEADS_PER_GROUP * len(A_CONFIGS)
A_GROUP_COLS = A_HEADS_PER_GROUP * HEAD_DIM
A_OUT = A_GROUP_COLS
B_Q_HEADS = 8
B_KV_HEADS = 2
B_GROUP = B_Q_HEADS // B_KV_HEADS
B_OUT = B_Q_HEADS * HEAD_DIM
C_HEADS = 4
C_OUT = C_HEADS * 2 * HEAD_DIM
N_BRANCH = 3
N_EXPERTS = 16
N_EXPERT_GROUPS = 4
EXPERTS_PER_GROUP = N_EXPERTS // N_EXPERT_GROUPS
TOP_K = 2
DEEPNORM_ALPHA = (2 * DEPTH) ** 0.25
LN_EPS = 1e-5
RMS_EPS = 1e-6
NEG_INF = -1e30
LOG2E = 1.4426950408889634
LN2 = 0.6931471805599453
ATTN_SCALE = HEAD_DIM ** -0.5 * LOG2E
A_RADIUS = 64

HPG = A_HEADS_PER_GROUP
B_Q0, B_K0, B_V0 = 0, B_Q_HEADS, B_Q_HEADS + B_KV_HEADS
B_COLS = (B_Q_HEADS + 2 * B_KV_HEADS) * HEAD_DIM
C_COLS = 6 * C_HEADS * HEAD_DIM
A_COLS = 3 * A_HEADS * HEAD_DIM
K_PLAIN, K_ROPE, K_ROPE_Q, K_AXIAL_Q, K_AXIAL_K = range(5)

LANE = 128
QKV_TM = 2048
A_TN, B_TN, C_TN = A_GROUP_COLS, 4 * LANE, 8 * LANE
PROJ_CHUNK_HEADS = 2
GATE_TM, GATE_TN = 2048, 1024
PROJ_TM, PROJ_TN = 1024, 512
OUT_TM = 512
OUT_SPLIT = 2
ROUTER_TM = 512
MOE_BLOCK = 256
COMBINE_TM = 512
DISPATCH_TM = 1024
B_TQ = 512
SLOT_TM = 2048
DMA_UNROLL = 8
ATTN_TQ = 256
A_UNITS_PER_STEP = 2
C_HEADS_PER_STEP = 4
MERGE_TM = 1024
VMEM_LIMIT = 56 * 1024 * 1024
ROW_E, ROW_RANK, ROW_GATE = 0, 2, 4


def _params(sem, vmem=VMEM_LIMIT):
    return pltpu.CompilerParams(dimension_semantics=sem, vmem_limit_bytes=vmem)


def _tile(n, t):
    t = min(n, t)
    assert n % t == 0, (n, t)
    return t


def _rope_head(xh, c1_ref, s1_ref):
    return xh * c1_ref[...] + pltpu.roll(xh, 64, 1) * s1_ref[...]


def _head_epilogue(kind, xh, c1_ref, s1_ref, cax_ref, sa_ref, sb_ref, g_ref):
    if kind == K_PLAIN:
        return xh
    if kind in (K_ROPE, K_ROPE_Q):
        y = _rope_head(xh, c1_ref, s1_ref)
    else:
        g = g_ref[0:1, :] if kind == K_AXIAL_Q else g_ref[1:2, :]
        sq = xh * xh
        sq_hi = sq.astype(BF16)
        sq_lo = (sq - sq_hi.astype(F32)).astype(BF16)
        ones = jnp.ones((LANE, LANE), BF16)
        ssum = jnp.dot(sq_hi, ones, preferred_element_type=F32) + jnp.dot(sq_lo, ones, preferred_element_type=F32)
        r = xh * lax.rsqrt(ssum * (1.0 / LANE) + RMS_EPS) * g
        y = r * cax_ref[...] + pltpu.roll(r, 96, 1) * sa_ref[...] + pltpu.roll(r, 32, 1) * sb_ref[...]
    return y * ATTN_SCALE if kind in (K_ROPE_Q, K_AXIAL_Q) else y


def _project_kernel(x_ref, w_ref, c1_ref, s1_ref, cax_ref, sa_ref, sb_ref, g_ref, o_ref, *scratch,
                    tile_kinds, dilation):
    j = pl.program_id(1)
    patterns = {}
    for idx, pattern in enumerate(tile_kinds):
        patterns.setdefault(pattern, []).append(idx)

    def emit(pattern):
        axial = any(kind in (K_AXIAL_Q, K_AXIAL_K) for kind in pattern)
        chunk = len(pattern) if axial else PROJ_CHUNK_HEADS
        for c0 in range(0, len(pattern), chunk):
            acc = jnp.dot(x_ref[...], w_ref[:, c0 * LANE:(c0 + chunk) * LANE], preferred_element_type=F32)
            for hh in range(chunk):
                h = c0 + hh
                cols = slice(h * LANE, (h + 1) * LANE)
                y = _head_epilogue(pattern[h], acc[:, hh * LANE:(hh + 1) * LANE],
                                   c1_ref, s1_ref, cax_ref, sa_ref, sb_ref, g_ref)
                if dilation == 1:
                    o_ref[0, 0, :, cols] = y.astype(o_ref.dtype)
                else:
                    scratch[0][h] = y

    for pattern, idxs in patterns.items():
        cond = functools.reduce(jnp.logical_or, [j == i for i in idxs])
        pl.when(cond)(functools.partial(emit, pattern))

    if dilation > 1:
        y_ref = scratch[0]
        rows = y_ref.shape[1] // dilation
        for c in range(dilation):
            for h in range(y_ref.shape[0]):
                o_ref[0, c, :, h * LANE:(h + 1) * LANE] = (
                    y_ref[h, pl.ds(c, rows, stride=dilation), :].astype(o_ref.dtype))


def _project(xb, w, tables, gains, seq, tn, tile_kinds, dilation, name):
    t, d = xb.shape
    n = w.shape[1]
    assert n == tn * len(tile_kinds) and all(len(p) * LANE == tn for p in tile_kinds)
    tm = _tile(seq, QKV_TM)
    nsb = seq // tm
    tab_spec = pl.BlockSpec((tm, LANE), lambda i, j: (i % nsb, 0))
    scratch = [] if dilation == 1 else [pltpu.VMEM((tn // LANE, tm, LANE), F32)]
    return pl.pallas_call(
        functools.partial(_project_kernel, tile_kinds=tuple(tile_kinds), dilation=dilation),
        out_shape=jax.ShapeDtypeStruct((t // seq, dilation, seq // dilation, n), BF16),
        grid=(t // tm, n // tn),
        in_specs=[pl.BlockSpec((tm, d), lambda i, j: (i, 0)),
                  pl.BlockSpec((d, tn), lambda i, j: (0, j)),
                  tab_spec, tab_spec, tab_spec, tab_spec, tab_spec,
                  pl.BlockSpec((2, LANE), lambda i, j: (0, 0))],
        out_specs=pl.BlockSpec((1, dilation, tm // dilation, tn), lambda i, j: (i // nsb, 0, i % nsb, j)),
        scratch_shapes=scratch,
        compiler_params=_params(("parallel", "arbitrary")),
        name=name,
    )(xb, w, *tables, gains)


A_TILE_KINDS = ((K_ROPE_Q,) * HPG, (K_ROPE,) * HPG, (K_PLAIN,) * HPG)
B_TILE_KINDS = ((K_AXIAL_Q,) * 4, (K_AXIAL_Q,) * 4, (K_AXIAL_K,) * B_KV_HEADS + (K_PLAIN,) * B_KV_HEADS)
C_TILE_KINDS = ((K_ROPE_Q,) * 8, (K_ROPE,) * 8, (K_PLAIN,) * 8)


def _gate_kernel(x_ref, w_ref, b_ref, o_ref):
    acc = jnp.dot(x_ref[...], w_ref[...], preferred_element_type=F32) + b_ref[...]
    o_ref[...] = jax.nn.sigmoid(acc).astype(o_ref.dtype)


def _gate_proj(xb, w, b):
    t, d = xb.shape
    n = w.shape[1]
    tm, tn = _tile(t, GATE_TM), _tile(n, GATE_TN)
    return pl.pallas_call(
        _gate_kernel,
        out_shape=jax.ShapeDtypeStruct((t, n), BF16),
        grid=(t // tm, n // tn),
        in_specs=[pl.BlockSpec((tm, d), lambda i, j: (i, 0)),
                  pl.BlockSpec((d, tn), lambda i, j: (0, j)),
                  pl.BlockSpec((1, tn), lambda i, j: (0, j))],
        out_specs=pl.BlockSpec((tm, tn), lambda i, j: (i, j)),
        compiler_params=_params(("parallel", "arbitrary")),
        name="gate_proj",
    )(xb, w, b)


def _attn_a_kernel(q_ref, k_ref, v_ref, o_ref, lse_ref, *, tq, win, sub_len):
    n_res, n_qb = q_ref.shape[1], q_ref.shape[2] // tq
    lane = lax.broadcasted_iota(I32, (tq, LANE), 1)
    for qb in range(n_qb):
        q0 = (pl.program_id(2) * n_qb + qb) * tq
        start = jnp.clip(q0 - A_RADIUS, 0, sub_len - win)
        if win % 16 == 0 and tq % 64 == 0:
            start = pl.multiple_of(start, 16)
        q_pos = q0 + lax.broadcasted_iota(I32, (tq, win), 0)
        k_pos = start + lax.broadcasted_iota(I32, (tq, win), 1)
        bias = jnp.where(jnp.abs(k_pos - q_pos) <= A_RADIUS, 0.0, NEG_INF).astype(F32)
        rows = slice(qb * tq, (qb + 1) * tq)
        for c in range(n_res):
            lse_all = jnp.zeros((tq, LANE), F32)
            for h in range(HPG):
                cols = slice(h * LANE, (h + 1) * LANE)
                q = q_ref[0, c, rows, cols]
                k = k_ref[0, c, pl.ds(start, win), cols]
                v = v_ref[0, c, pl.ds(start, win), cols]
                s = lax.dot_general(q, k, (((1,), (1,)), ((), ())), preferred_element_type=F32) + bias
                m = jnp.max(s, axis=-1, keepdims=True)
                p = jnp.exp2(s - m).astype(BF16)
                ov = jnp.dot(p, jnp.concatenate([v, jnp.ones_like(v)], axis=1), preferred_element_type=F32)
                den = ov[:, LANE:]
                o_ref[0, c, rows, cols] = (ov[:, :LANE] / den).astype(o_ref.dtype)
                lse_all = jnp.where(lane == h, (m + jnp.log2(den)) * LN2, lse_all)
            lse_ref[0, c, rows, :] = lse_all


def _attn_a(qkv4):
    b, r, sub_len, _ = qkv4.shape
    tq = _tile(sub_len, ATTN_TQ)
    win = min(tq + 2 * A_RADIUS, sub_len)
    units = A_UNITS_PER_STEP * (ATTN_TQ // tq)
    n_qb = min(units, sub_len // tq)
    n_res = min(max(units // n_qb, 1), r)
    assert (sub_len // tq) % n_qb == 0 and r % n_res == 0
    return pl.pallas_call(
        functools.partial(_attn_a_kernel, tq=tq, win=win, sub_len=sub_len),
        out_shape=(jax.ShapeDtypeStruct((b, r, sub_len, A_GROUP_COLS), BF16),
                   jax.ShapeDtypeStruct((b, r, sub_len, LANE), F32)),
        grid=(b, r // n_res, sub_len // (tq * n_qb)),
        in_specs=[pl.BlockSpec((1, n_res, tq * n_qb, A_GROUP_COLS), lambda bi, c, qi: (bi, c, qi, 0)),
                  pl.BlockSpec((1, n_res, sub_len, A_GROUP_COLS), lambda bi, c, qi: (bi, c, 0, 1)),
                  pl.BlockSpec((1, n_res, sub_len, A_GROUP_COLS), lambda bi, c, qi: (bi, c, 0, 2))],
        out_specs=(pl.BlockSpec((1, n_res, tq * n_qb, A_GROUP_COLS), lambda bi, c, qi: (bi, c, qi, 0)),
                   pl.BlockSpec((1, n_res, tq * n_qb, LANE), lambda bi, c, qi: (bi, c, qi, 0))),
        compiler_params=_params(("parallel", "parallel", "arbitrary")),
        name=f"attn_a_r{r}",
    )(qkv4, qkv4, qkv4)


def _merge_a_kernel(o0_ref, l0_ref, o1_ref, l1_ref, o2_ref, l2_ref, o_ref, so1, sl1, so2, sl2):
    def to_token_order(src_ref, dst_ref):
        r, rows = src_ref.shape[1], src_ref.shape[2]
        for c in range(r):
            for h in range(dst_ref.shape[0]):
                dst_ref[h, pl.ds(c, rows, stride=r), :] = src_ref[0, c, :, h * LANE:(h + 1) * LANE].astype(F32)

    to_token_order(o1_ref, so1)
    to_token_order(l1_ref, sl1)
    to_token_order(o2_ref, so2)
    to_token_order(l2_ref, sl2)
    l0, l1, l2 = l0_ref[...], sl1[0], sl2[0]
    m = jnp.maximum(jnp.maximum(l0, l1), l2)
    e0, e1, e2 = jnp.exp(l0 - m), jnp.exp(l1 - m), jnp.exp(l2 - m)
    den = e0 + e1 + e2
    w0, w1, w2 = e0 / den, e1 / den, e2 / den
    for h in range(HPG):
        cols = slice(h * LANE, (h + 1) * LANE)
        acc = (o0_ref[:, cols].astype(F32) * w0[:, h:h + 1]
               + so1[h] * w1[:, h:h + 1]
               + so2[h] * w2[:, h:h + 1])
        o_ref[:, cols] = acc.astype(o_ref.dtype)


def _merge_a(o0, l0, o1, l1, o2, l2, seq):
    t = o0.shape[0]
    tm = _tile(seq, MERGE_TM)
    nsb = seq // tm

    def res_spec(arr):
        r, n = arr.shape[1], arr.shape[3]
        return pl.BlockSpec((1, r, tm // r, n), lambda i: (i // nsb, 0, i % nsb, 0))

    return pl.pallas_call(
        _merge_a_kernel,
        out_shape=jax.ShapeDtypeStruct((t, A_OUT), BF16),
        grid=(t // tm,),
        in_specs=[pl.BlockSpec((tm, A_GROUP_COLS), lambda i: (i, 0)), pl.BlockSpec((tm, LANE), lambda i: (i, 0)),
                  res_spec(o1), res_spec(l1), res_spec(o2), res_spec(l2)],
        out_specs=pl.BlockSpec((tm, A_OUT), lambda i: (i, 0)),
        scratch_shapes=[pltpu.VMEM((HPG, tm, LANE), F32), pltpu.VMEM((1, tm, LANE), F32),
                        pltpu.VMEM((HPG, tm, LANE), F32), pltpu.VMEM((1, tm, LANE), F32)],
        compiler_params=_params(("parallel",)),
        name="merge_a",
    )(o0, l0, o1, l1, o2, l2)


def _softmax_pv(q, k, v):
    s = lax.dot_general(q, k, (((1,), (1,)), ((), ())), preferred_element_type=F32)
    m = jnp.max(s, axis=-1, keepdims=True)
    p = jnp.exp2(s - m)
    den = jnp.sum(p, axis=-1, keepdims=True)
    return jnp.dot(p.astype(BF16), v, preferred_element_type=F32) / den


def _attn_b_kernel(q_ref, k_ref, v_ref, o_ref):
    for h in range(B_KV_HEADS):
        k = k_ref[0, :, h * LANE:(h + 1) * LANE]
        v = v_ref[0, :, h * LANE:(h + 1) * LANE]
        v1 = jnp.concatenate([v, jnp.ones_like(v)], axis=1)
        for g in range(B_GROUP):
            cols = slice((h * B_GROUP + g) * LANE, (h * B_GROUP + g + 1) * LANE)
            s = lax.dot_general(q_ref[0, :, cols], k, (((1,), (1,)), ((), ())), preferred_element_type=F32)
            p = jnp.exp2(s - jnp.max(s, axis=-1, keepdims=True)).astype(BF16)
            ov = jnp.dot(p, v1, preferred_element_type=F32)
            o_ref[0, :, cols] = (ov[:, :LANE] / ov[:, LANE:]).astype(o_ref.dtype)


def _attn_b(qkv3):
    b, s, _ = qkv3.shape
    tq = _tile(s, B_TQ)
    kv_cols = B_KV_HEADS * LANE
    o = pl.pallas_call(
        _attn_b_kernel,
        out_shape=jax.ShapeDtypeStruct((b, s, B_OUT), BF16),
        grid=(b, s // tq),
        in_specs=[pl.BlockSpec((1, tq, B_OUT), lambda bi, qi: (bi, qi, 0)),
                  pl.BlockSpec((1, s, kv_cols), lambda bi, qi: (bi, 0, B_K0 * LANE // kv_cols)),
                  pl.BlockSpec((1, s, kv_cols), lambda bi, qi: (bi, 0, B_V0 * LANE // kv_cols))],
        out_specs=pl.BlockSpec((1, tq, B_OUT), lambda bi, qi: (bi, qi, 0)),
        compiler_params=_params(("parallel", "arbitrary")),
        name="attn_b",
    )(qkv3, qkv3, qkv3)
    return o.reshape(b * s, B_OUT)


def _attn_c_kernel(q_ref, k_ref, v_ref, lam_ref, g_ref, o_ref, *, lambda_init):
    lam_p = lam_ref[...].astype(F32)
    lam = (jnp.exp(jnp.sum(lam_p[0:1] * lam_p[1:2], axis=-1, keepdims=True))
           - jnp.exp(jnp.sum(lam_p[2:3] * lam_p[3:4], axis=-1, keepdims=True)) + lambda_init)
    for hh in range(C_HEADS_PER_STEP):
        c0 = hh * 2 * LANE
        first, second = slice(c0, c0 + LANE), slice(c0 + LANE, c0 + 2 * LANE)
        v = v_ref[0, :, c0:c0 + 2 * LANE]
        o = (_softmax_pv(q_ref[0, :, first], k_ref[0, :, first], v)
             - lam * _softmax_pv(q_ref[0, :, second], k_ref[0, :, second], v))
        o = o * lax.rsqrt(jnp.mean(o * o, axis=-1, keepdims=True) + RMS_EPS) * g_ref[...].astype(F32)
        o_ref[0, :, c0:c0 + 2 * LANE] = (o * (1.0 - lambda_init)).astype(o_ref.dtype)


def _attn_c(qkv3, lam_rows, subln_g, lambda_init):
    b, s, _ = qkv3.shape
    tq = _tile(s, ATTN_TQ)
    cols = C_HEADS_PER_STEP * 2 * LANE
    steps = C_HEADS // C_HEADS_PER_STEP
    o = pl.pallas_call(
        functools.partial(_attn_c_kernel, lambda_init=lambda_init),
        out_shape=jax.ShapeDtypeStruct((b, s, C_OUT), BF16),
        grid=(b, steps, s // tq),
        in_specs=[pl.BlockSpec((1, tq, cols), lambda bi, h, qi: (bi, qi, h)),
                  pl.BlockSpec((1, s, cols), lambda bi, h, qi: (bi, 0, steps + h)),
                  pl.BlockSpec((1, s, cols), lambda bi, h, qi: (bi, 0, 2 * steps + h)),
                  pl.BlockSpec((4, LANE), lambda bi, h, qi: (0, 0)),
                  pl.BlockSpec((1, 2 * LANE), lambda bi, h, qi: (0, 0))],
        out_specs=pl.BlockSpec((1, tq, cols), lambda bi, h, qi: (bi, qi, h)),
        compiler_params=_params(("parallel", "parallel", "arbitrary")),
        name="attn_c",
    )(qkv3, qkv3, qkv3, lam_rows, subln_g)
    return o.reshape(b * s, C_OUT)


def _branch_proj_kernel(oa_ref, ob_ref, oc_ref, g0_ref, g1_ref, g2_ref, wa_ref, wb_ref, wc_ref, o_ref):
    acc = g0_ref[...].astype(F32) * jnp.dot(oa_ref[...], wa_ref[...], preferred_element_type=F32)
    acc = acc + g1_ref[...].astype(F32) * jnp.dot(ob_ref[...], wb_ref[...], preferred_element_type=F32)
    acc = acc + g2_ref[...].astype(F32) * jnp.dot(oc_ref[...], wc_ref[...], preferred_element_type=F32)
    o_ref[...] = acc.astype(o_ref.dtype)


def _branch_proj(oa, ob, oc, gates, wa, wb, wc):
    t = oa.shape[0]
    d = wa.shape[1]
    tm, tn = _tile(t, PROJ_TM), _tile(d, PROJ_TN)
    nb = d // tn
    gspecs = [pl.BlockSpec((tm, tn), functools.partial(lambda i, j, br: (i, br * nb + j), br=br)) for br in range(N_BRANCH)]
    return pl.pallas_call(
        _branch_proj_kernel,
        out_shape=jax.ShapeDtypeStruct((t, d), BF16),
        grid=(t // tm, nb),
        in_specs=[pl.BlockSpec((tm, A_OUT), lambda i, j: (i, 0)),
                  pl.BlockSpec((tm, B_OUT), lambda i, j: (i, 0)),
                  pl.BlockSpec((tm, C_OUT), lambda i, j: (i, 0))] + gspecs +
                 [pl.BlockSpec((A_OUT, tn), lambda i, j: (0, j)),
                  pl.BlockSpec((B_OUT, tn), lambda i, j: (0, j)),
                  pl.BlockSpec((C_OUT, tn), lambda i, j: (0, j))],
        out_specs=pl.BlockSpec((tm, tn), lambda i, j: (i, j)),
        compiler_params=_params(("parallel", "arbitrary")),
        name="branch_proj",
    )(oa, ob, oc, gates, gates, gates, wa, wb, wc)


def _layer_norm(z, g, b):
    mu = jnp.mean(z, axis=-1, keepdims=True)
    zc = z - mu
    var = jnp.mean(zc * zc, axis=-1, keepdims=True)
    return zc * lax.rsqrt(var + LN_EPS) * g + b


def _out_ln_kernel(m_ref, w_ref, x_ref, g_ref, b_ref, o_ref):
    part = m_ref.shape[0] // OUT_SPLIT
    for r in range(OUT_SPLIT):
        rows = slice(r * part, (r + 1) * part)
        mix = jnp.dot(m_ref[rows, :], w_ref[...], preferred_element_type=F32)
        o_ref[rows, :] = _layer_norm(DEEPNORM_ALPHA * x_ref[rows, :] + mix, g_ref[...], b_ref[...])


def _out_proj_ln(merged, w_out, x, g, b):
    t, d = x.shape
    tm = _tile(t, OUT_TM)
    row = pl.BlockSpec((tm, d), lambda i: (i, 0))
    vec = pl.BlockSpec((1, d), lambda i: (0, 0))
    return pl.pallas_call(
        _out_ln_kernel,
        out_shape=jax.ShapeDtypeStruct((t, d), F32),
        grid=(t // tm,),
        in_specs=[row, pl.BlockSpec((d, d), lambda i: (0, 0)), row, vec, vec],
        out_specs=row,
        compiler_params=_params(("parallel",)),
        name="out_proj_ln",
    )(merged, w_out, x, g, b)


def _router_kernel(x_ref, whi_ref, wlo_ref, bias_ref, er_ref, gt_ref, cnt_ref, carry_ref, *, tm):
    @pl.when(pl.program_id(0) == 0)
    def _():
        carry_ref[...] = jnp.zeros_like(carry_ref)

    x = x_ref[...]
    x_hi = x.astype(BF16)
    x_lo = (x - x_hi.astype(F32)).astype(BF16)
    nt = (((1,), (1,)), ((), ()))
    logits = (lax.dot_general(whi_ref[...], x_hi, nt, preferred_element_type=F32)
              + lax.dot_general(wlo_ref[...], x_hi, nt, preferred_element_type=F32)
              + lax.dot_general(whi_ref[...], x_lo, nt, preferred_element_type=F32))
    scores = jax.nn.sigmoid(logits)
    biased = scores + bias_ref[...]

    def row(a, e):
        return a[e:e + 1, :]

    gscore = []
    for g in range(N_EXPERT_GROUPS):
        a, b, c, d = (row(biased, g * EXPERTS_PER_GROUP + i) for i in range(EXPERTS_PER_GROUP))
        hi1, lo1, hi2, lo2 = jnp.maximum(a, b), jnp.minimum(a, b), jnp.maximum(c, d), jnp.minimum(c, d)
        gscore.append(jnp.maximum(hi1, hi2) + jnp.maximum(jnp.minimum(hi1, hi2), jnp.maximum(lo1, lo2)))
    gsel = jnp.zeros((1, tm), I32)
    best = gscore[0]
    for g in range(1, N_EXPERT_GROUPS):
        better = gscore[g] > best
        gsel = jnp.where(better, g, gsel)
        best = jnp.where(better, gscore[g], best)

    def pick(a, i):
        out = row(a, i)
        for g in range(1, N_EXPERT_GROUPS):
            out = jnp.where(gsel == g, row(a, g * EXPERTS_PER_GROUP + i), out)
        return out

    bv = [pick(biased, i) for i in range(EXPERTS_PER_GROUP)]
    sv = [pick(scores, i) for i in range(EXPERTS_PER_GROUP)]
    i0 = jnp.zeros((1, tm), I32)
    b0 = bv[0]
    for i in range(1, EXPERTS_PER_GROUP):
        better = bv[i] > b0
        i0 = jnp.where(better, i, i0)
        b0 = jnp.where(better, bv[i], b0)
    i1 = jnp.full((1, tm), -1, I32)
    b1 = jnp.full((1, tm), -jnp.inf, F32)
    for i in range(EXPERTS_PER_GROUP):
        better = (i0 != i) & ((i1 < 0) | (bv[i] > b1))
        i1 = jnp.where(better, i, i1)
        b1 = jnp.where(better, bv[i], b1)
    s0 = jnp.zeros((1, tm), F32)
    s1 = jnp.zeros((1, tm), F32)
    for i in range(EXPERTS_PER_GROUP):
        s0 = jnp.where(i0 == i, sv[i], s0)
        s1 = jnp.where(i1 == i, sv[i], s1)
    e0 = gsel * EXPERTS_PER_GROUP + i0
    e1 = gsel * EXPERTS_PER_GROUP + i1
    den = s0 + s1
    g0, g1 = s0 / den, s1 / den

    erow = lax.broadcasted_iota(I32, (N_EXPERTS, tm), 0)
    member = ((erow == e0) | (erow == e1))
    tri = (lax.broadcasted_iota(I32, (tm, tm), 0) < lax.broadcasted_iota(I32, (tm, tm), 1))
    prefix = jnp.dot(member.astype(BF16), tri.astype(BF16), preferred_element_type=F32) + carry_ref[:, 0:1]
    r0 = jnp.sum(jnp.where(erow == e0, prefix, 0.0), axis=0, keepdims=True)
    r1 = jnp.sum(jnp.where(erow == e1, prefix, 0.0), axis=0, keepdims=True)
    carry_ref[...] = carry_ref[...] + jnp.sum(member.astype(F32), axis=1, keepdims=True)
    cnt_ref[...] = carry_ref[...]

    zi = jnp.zeros((1, tm), I32)
    er_ref[...] = jnp.concatenate([e0, e1, r0.astype(I32), r1.astype(I32), zi, zi, zi, zi], axis=0)
    zf = jnp.zeros((1, tm), F32)
    rec = jnp.concatenate([zf, zf, zf, zf, g0, g1, zf, zf, jnp.zeros((LANE - 8, tm), F32)], axis=0)
    gt_ref[...] = rec.T


def _router(x, w_hi_t, w_lo_t, bias_col):
    t, d = x.shape
    tm = _tile(t, ROUTER_TM)
    return pl.pallas_call(
        functools.partial(_router_kernel, tm=tm),
        out_shape=(jax.ShapeDtypeStruct((8, t), I32), jax.ShapeDtypeStruct((t, LANE), F32),
                   jax.ShapeDtypeStruct((N_EXPERTS, LANE), F32)),
        grid=(t // tm,),
        in_specs=[pl.BlockSpec((tm, d), lambda i: (i, 0)),
                  pl.BlockSpec((N_EXPERTS, d), lambda i: (0, 0)),
                  pl.BlockSpec((N_EXPERTS, d), lambda i: (0, 0)),
                  pl.BlockSpec((N_EXPERTS, 1), lambda i: (0, 0))],
        out_specs=(pl.BlockSpec((8, tm), lambda i: (0, i)),
                   pl.BlockSpec((tm, LANE), lambda i: (i, 0)),
                   pl.BlockSpec((N_EXPERTS, LANE), lambda i: (0, 0))),
        scratch_shapes=[pltpu.VMEM((N_EXPERTS, LANE), F32)],
        compiler_params=_params(("arbitrary",)),
        name="router",
    )(x, w_hi_t, w_lo_t, bias_col)


def _slot_rows_kernel(pad_start_ref, er_ref, o_ref):
    e = er_ref[ROW_E:ROW_E + TOP_K, :]
    base = jnp.zeros_like(e)
    for x in range(N_EXPERTS):
        base = jnp.where(e == x, pad_start_ref[x], base)
    rows = base + er_ref[ROW_RANK:ROW_RANK + TOP_K, :]
    o_ref[...] = jnp.concatenate([rows, jnp.zeros((8 - TOP_K, rows.shape[1]), I32)], axis=0)


def _slot_rows(pad_start, er):
    t = er.shape[1]
    tm = _tile(t, SLOT_TM)
    grid_spec = pltpu.PrefetchScalarGridSpec(
        num_scalar_prefetch=1,
        grid=(t // tm,),
        in_specs=[pl.BlockSpec((8, tm), lambda i, ps: (0, i))],
        out_specs=pl.BlockSpec((8, tm), lambda i, ps: (0, i)),
    )
    return pl.pallas_call(
        _slot_rows_kernel,
        out_shape=jax.ShapeDtypeStruct((8, t), I32),
        grid_spec=grid_spec,
        compiler_params=_params(("arbitrary",)),
        name="moe_slot_rows",
    )(pad_start, er)


def _dispatch_kernel(fill_end_ref, pad_end_ref, rows_ref, x_ref, xs_hbm, zblk, sem, zsem, *, tm):
    def row_copy(r, dst_row):
        return pltpu.make_async_copy(x_ref.at[pl.ds(r, 1), :], xs_hbm.at[pl.ds(dst_row, 1), :], sem)

    def issue(r, c):
        row_copy(r, rows_ref[0, r]).start()
        row_copy(r, rows_ref[1, r]).start()
        return c

    lax.fori_loop(0, tm, issue, 0, unroll=DMA_UNROLL)

    def zero_row(dst_row):
        return pltpu.make_async_copy(zblk.at[pl.ds(0, 1), :], xs_hbm.at[pl.ds(dst_row, 1), :], zsem)

    def zero_block(blk):
        dst = xs_hbm.at[pl.ds(pl.multiple_of(blk * MOE_BLOCK, MOE_BLOCK), MOE_BLOCK), :]
        return pltpu.make_async_copy(zblk, dst, zsem)

    @pl.when(pl.program_id(0) == 0)
    def _():
        zblk[...] = jnp.zeros_like(zblk)
        for e in range(N_EXPERTS):
            lo, hi = fill_end_ref[e], pad_end_ref[e]
            lax.fori_loop(lo, hi, lambda r, c: (zero_row(r).start(), c)[1], 0)
            lax.fori_loop(lo, hi, lambda r, c: (zero_row(0).wait(), c)[1], 0)
        first, last = pad_end_ref[N_EXPERTS - 1] // MOE_BLOCK, xs_hbm.shape[0] // MOE_BLOCK
        lax.fori_loop(first, last, lambda blk, c: (zero_block(blk).start(), c)[1], 0)
        lax.fori_loop(first, last, lambda blk, c: (zero_block(0).wait(), c)[1], 0)

    def drain(r, c):
        row_copy(0, 0).wait()
        row_copy(0, 0).wait()
        return c

    lax.fori_loop(0, tm, drain, 0, unroll=DMA_UNROLL)


def _dispatch(x, rows, fill_end, pad_end, n_rows):
    t, d = x.shape
    tm = _tile(t, DISPATCH_TM)
    grid_spec = pltpu.PrefetchScalarGridSpec(
        num_scalar_prefetch=2,
        grid=(t // tm,),
        in_specs=[pl.BlockSpec((8, tm), lambda i, fe, pe: (0, i), memory_space=pltpu.SMEM),
                  pl.BlockSpec((tm, d), lambda i, fe, pe: (i, 0))],
        out_specs=pl.BlockSpec(memory_space=pl.ANY),
        scratch_shapes=[pltpu.VMEM((MOE_BLOCK, d), F32), pltpu.SemaphoreType.DMA, pltpu.SemaphoreType.DMA],
    )
    return pl.pallas_call(
        functools.partial(_dispatch_kernel, tm=tm),
        out_shape=jax.ShapeDtypeStruct((n_rows, d), F32),
        grid_spec=grid_spec,
        compiler_params=_params(("arbitrary",)),
        name="moe_dispatch",
    )(fill_end, pad_end, rows, x)


def _expert_kernel(blk_e_ref, nact_ref, x_ref, w1_ref, w3_ref, w2_ref, o_ref):
    del blk_e_ref

    @pl.when(pl.program_id(0) < nact_ref[0])
    def _():
        x = x_ref[...].astype(BF16)
        h1 = jnp.dot(x, w1_ref[0], preferred_element_type=F32)
        h3 = jnp.dot(x, w3_ref[0], preferred_element_type=F32)
        h = (h1 * jax.nn.sigmoid(h1) * h3).astype(BF16)
        o_ref[...] = jnp.dot(h, w2_ref[0], preferred_element_type=F32)

    @pl.when(pl.program_id(0) >= nact_ref[0])
    def _():
        o_ref[...] = jnp.zeros_like(o_ref)


def _experts(xs, blk_e, n_active, w1, w3, w2):
    p, d = xs.shape
    f = w1.shape[2]
    nb = p // MOE_BLOCK
    grid_spec = pltpu.PrefetchScalarGridSpec(
        num_scalar_prefetch=2,
        grid=(nb,),
        in_specs=[pl.BlockSpec((MOE_BLOCK, d), lambda i, be, na: (jnp.minimum(i, na[0] - 1), 0)),
                  pl.BlockSpec((1, d, f), lambda i, be, na: (be[i], 0, 0)),
                  pl.BlockSpec((1, d, f), lambda i, be, na: (be[i], 0, 0)),
                  pl.BlockSpec((1, f, d), lambda i, be, na: (be[i], 0, 0))],
        out_specs=pl.BlockSpec((MOE_BLOCK, d), lambda i, be, na: (i, 0)),
    )
    return pl.pallas_call(
        _expert_kernel,
        out_shape=jax.ShapeDtypeStruct((p, d), F32),
        grid_spec=grid_spec,
        compiler_params=_params(("arbitrary",)),
        name="moe_experts",
    )(blk_e, n_active, xs, w1, w3, w2)


def _combine_kernel(rows_ref, next_rows_ref, x_ref, gt_ref, g_ref, b_ref, ys_hbm, o_ref, ob_ref, buf, sems, *, tm):
    i = pl.program_id(0)
    n = pl.num_programs(0)

    def row_copy(par, r, slot, src_row):
        return pltpu.make_async_copy(ys_hbm.at[pl.ds(src_row, 1), :], buf.at[par, slot, pl.ds(r, 1), :], sems.at[par])

    def gather(par, idx_ref):
        def issue(r, c):
            row_copy(par, r, 0, idx_ref[0, r]).start()
            row_copy(par, r, 1, idx_ref[1, r]).start()
            return c
        lax.fori_loop(0, tm, issue, 0, unroll=DMA_UNROLL)

    @pl.when(i == 0)
    def _():
        gather(0, rows_ref)

    @pl.when(i + 1 < n)
    def _():
        gather((i + 1) % 2, next_rows_ref)

    par = i % 2

    def drain(r, c):
        row_copy(par, 0, 0, 0).wait()
        row_copy(par, 0, 0, 0).wait()
        return c

    lax.fori_loop(0, tm, drain, 0, unroll=DMA_UNROLL)
    gt = gt_ref[...]
    ffn = gt[:, ROW_GATE:ROW_GATE + 1] * buf[par, 0] + gt[:, ROW_GATE + 1:ROW_GATE + 2] * buf[par, 1]
    y = _layer_norm(DEEPNORM_ALPHA * x_ref[...] + ffn, g_ref[...], b_ref[...])
    o_ref[...] = y
    ob_ref[...] = y.astype(ob_ref.dtype)


def _combine_ln(x, ys, rows, gt, g, b):
    t, d = x.shape
    tm = _tile(t, COMBINE_TM)
    n = t // tm
    row = pl.BlockSpec((tm, d), lambda i: (i, 0))
    vec = pl.BlockSpec((1, d), lambda i: (0, 0))
    return pl.pallas_call(
        functools.partial(_combine_kernel, tm=tm),
        out_shape=(jax.ShapeDtypeStruct((t, d), F32), jax.ShapeDtypeStruct((t, d), BF16)),
        grid=(n,),
        in_specs=[pl.BlockSpec((8, tm), lambda i: (0, i), memory_space=pltpu.SMEM),
                  pl.BlockSpec((8, tm), lambda i: (0, jnp.minimum(i + 1, n - 1)), memory_space=pltpu.SMEM),
                  row,
                  pl.BlockSpec((tm, LANE), lambda i: (i, 0)),
                  vec, vec,
                  pl.BlockSpec(memory_space=pl.ANY)],
        out_specs=(row, row),
        scratch_shapes=[pltpu.VMEM((2, TOP_K, tm, d), F32), pltpu.SemaphoreType.DMA((2,))],
        compiler_params=_params(("arbitrary",)),
        name="moe_combine_ln",
    )(rows, rows, x, gt, g, b, ys)


def _rope_tables(seq):
    def tab(pos, dim):
        inv = ROPE_THETA ** (-jnp.arange(0, dim, 2, dtype=F32) / dim)
        ang = pos.astype(F32)[:, None] * inv[None, :]
        return jnp.cos(ang), jnp.sin(ang)

    pos = jnp.arange(seq)
    cos1, sin1 = tab(pos, HEAD_DIM)
    cos_r, sin_r = tab(pos // GRID_W, HEAD_DIM // 2)
    cos_c, sin_c = tab(pos % GRID_W, HEAD_DIM // 2)
    z = jnp.zeros_like(sin_r)
    c1 = jnp.concatenate([cos1, cos1], axis=1)
    s1 = jnp.concatenate([-sin1, sin1], axis=1)
    cax = jnp.concatenate([cos_r, cos_r, cos_c, cos_c], axis=1)
    sa = jnp.concatenate([-sin_r, z, -sin_c, z], axis=1)
    sb = jnp.concatenate([z, sin_r, z, sin_c], axis=1)
    return c1, s1, cax, sa, sb


def _moe(x1, lw, shared):
    t, d = x1.shape
    er, gt, cnt = _router(x1, shared["wr_hi"], shared["wr_lo"], shared["r_bias"])
    counts = cnt[:, 0].astype(I32)
    padded = (counts + MOE_BLOCK - 1) // MOE_BLOCK * MOE_BLOCK
    pad_end = jnp.cumsum(padded)
    pad_start = (pad_end - padded).astype(I32)
    n_rows = t * TOP_K + N_EXPERTS * MOE_BLOCK
    nb = n_rows // MOE_BLOCK
    blk_row = jnp.arange(nb, dtype=I32)[:, None] * MOE_BLOCK
    blk_e = jnp.minimum(jnp.sum((pad_end[None, :] <= blk_row).astype(I32), axis=1), N_EXPERTS - 1)
    blk_e = blk_e + lw["layer"] * N_EXPERTS
    n_active = (pad_end[-1:] // MOE_BLOCK).astype(I32)
    rows = _slot_rows(pad_start, er)
    xs = _dispatch(x1, rows, (pad_start + counts).astype(I32), pad_end.astype(I32), n_rows)
    ys = _experts(xs, blk_e, n_active, shared["w1"], shared["w3"], shared["w2"])
    return _combine_ln(x1, ys, rows, gt, lw["ln2_g"], lw["ln2_b"])


def _trunk(x, layers, shared):
    b, s, d = x.shape
    t = b * s
    tables = _rope_tables(s)
    xf = x.reshape(t, d)
    xb = xf.astype(BF16)
    for l, lw in enumerate(layers):
        gains = lw["qk_gains"]
        gates = _gate_proj(xb, lw["w_gate"], lw["b_gate"])
        a_parts = [_attn_a(_project(xb, lw["w_a"][g], tables, gains, s, A_TN, A_TILE_KINDS, dil, f"proj_a{g}"))
                   for g, (_, dil) in enumerate(A_CONFIGS)]
        (o0, l0), (o1, l1), (o2, l2) = a_parts
        oa = _merge_a(o0.reshape(t, A_GROUP_COLS), l0.reshape(t, LANE), o1, l1, o2, l2, s)
        qkv_b = _project(xb, lw["w_b"], tables, gains, s, B_TN, B_TILE_KINDS, 1, "proj_b")
        ob = _attn_b(qkv_b.reshape(b, s, B_COLS))
        lambda_init = 0.8 - 0.6 * float(np.exp(-0.3 * l))
        qkv_c = _project(xb, lw["w_c"], tables, gains, s, C_TN, C_TILE_KINDS, 1, "proj_c")
        oc = _attn_c(qkv_c.reshape(b, s, C_COLS), lw["lam_rows"], lw["subln_g"], lambda_init)
        merged = _branch_proj(oa, ob, oc, gates, lw["w_proj_a"], lw["w_proj_b"], lw["w_proj_c"])
        x1 = _out_proj_ln(merged, lw["w_out"], xf, lw["ln1_g"], lw["ln1_b"])
        xf, xb = _moe(x1, lw, shared)
    return xf.reshape(b, s, d)


def kernel(x_prompt, x_sample, w_in, q_norm_g, k_norm_g, lambda_q1, lambda_k1, lambda_q2, lambda_k2, subln_g,
           w_gate, b_gate, w_proj_a, w_proj_b, w_proj_c, w_out, ln1_g, ln1_b, w_router, router_bias,
           w1, w3, w2, ln2_g, ln2_b):
    assert [dil for _, dil in A_CONFIGS] == [1, 4, 16] and all(w // (2 * dil) == A_RADIUS for w, dil in A_CONFIGS)
    depth = w_in.shape[0]
    layers = []
    part = A_HEADS * HEAD_DIM
    for l in range(depth):
        w_l = w_in[l].astype(BF16)
        w_a = [jnp.concatenate([w_l[:, p * part + g * A_GROUP_COLS: p * part + (g + 1) * A_GROUP_COLS] for p in range(3)],
                               axis=1) for g in range(len(A_CONFIGS))]
        layers.append(dict(
            w_a=w_a, w_b=w_l[:, A_COLS:A_COLS + B_COLS], w_c=w_l[:, A_COLS + B_COLS:],
            qk_gains=jnp.stack([q_norm_g[l], k_norm_g[l]]).astype(F32),
            lam_rows=jnp.stack([lambda_q1[l], lambda_k1[l], lambda_q2[l], lambda_k2[l]]).astype(F32),
            subln_g=subln_g[l].reshape(1, -1).astype(F32),
            w_gate=w_gate[l].astype(BF16),
            b_gate=b_gate[l].reshape(1, -1).astype(F32),
            w_proj_a=w_proj_a[l].astype(BF16),
            w_proj_b=w_proj_b[l].astype(BF16),
            w_proj_c=w_proj_c[l].astype(BF16),
            w_out=w_out[l].astype(BF16),
            ln1_g=ln1_g[l].reshape(1, -1).astype(F32),
            ln1_b=ln1_b[l].reshape(1, -1).astype(F32),
            layer=l,
            ln2_g=ln2_g[l].reshape(1, -1).astype(F32),
            ln2_b=ln2_b[l].reshape(1, -1).astype(F32),
        ))
    wr_t = w_router.astype(F32).T
    wr_hi = wr_t.astype(BF16)
    wr_lo = (wr_t - wr_hi.astype(F32)).astype(BF16)
    def stacked(w):
        return w.astype(BF16).reshape((depth * N_EXPERTS,) + w.shape[2:])

    shared = dict(wr_hi=wr_hi, wr_lo=wr_lo, r_bias=router_bias.astype(F32).reshape(-1, 1),
                  w1=stacked(w1), w3=stacked(w3), w2=stacked(w2))
    return (_trunk(x_prompt, layers, shared), _trunk(x_sample, layers, shared))
```

```python
import functools

import numpy as np
import jax
import jax.numpy as jnp
from jax import lax
from jax.experimental import pallas as pl
from jax.experimental.pallas import tpu as pltpu

F32 = jnp.float32
BF16 = jnp.bfloat16
I32 = jnp.int32

DEPTH = 2
HEAD_DIM = 128
GRID_W = 64
ROPE_THETA = 10000.0
A_CONFIGS = ((128, 1), (512, 4), (2048, 16))
A_HEADS_PER_GROUP = 6
A_HEADS = A_HEADS_PER_GROUP * len(A_CONFIGS)
A_GROUP_COLS = A_HEADS_PER_GROUP * HEAD_DIM
A_OUT = A_GROUP_COLS
B_Q_HEADS = 8
B_KV_HEADS = 2
B_GROUP = B_Q_HEADS // B_KV_HEADS
B_OUT = B_Q_HEADS * HEAD_DIM
C_HEADS = 4
C_OUT = C_HEADS * 2 * HEAD_DIM
N_BRANCH = 3
N_EXPERTS = 16
N_EXPERT_GROUPS = 4
EXPERTS_PER_GROUP = N_EXPERTS // N_EXPERT_GROUPS
TOP_K = 2
DEEPNORM_ALPHA = (2 * DEPTH) ** 0.25
LN_EPS = 1e-5
RMS_EPS = 1e-6
NEG_INF = -1e30
LOG2E = 1.4426950408889634
LN2 = 0.6931471805599453
ATTN_SCALE = HEAD_DIM ** -0.5 * LOG2E
A_RADIUS = 64

HPG = A_HEADS_PER_GROUP
B_Q0, B_K0, B_V0 = 0, B_Q_HEADS, B_Q_HEADS + B_KV_HEADS
B_COLS = (B_Q_HEADS + 2 * B_KV_HEADS) * HEAD_DIM
C_COLS = 6 * C_HEADS * HEAD_DIM
A_COLS = 3 * A_HEADS * HEAD_DIM
K_PLAIN, K_ROPE, K_ROPE_Q, K_AXIAL_Q, K_AXIAL_K = range(5)

LANE = 128
QKV_TM = 2048
A_TN, B_TN, C_TN = A_GROUP_COLS, 4 * LANE, 8 * LANE
PROJ_CHUNK_HEADS = 2
GATE_TM, GATE_TN = 2048, 1024
PROJ_TM, PROJ_TN = 1024, 512
OUT_TM = 512
OUT_SPLIT = 2
ROUTER_TM = 512
MOE_BLOCK = 256
COMBINE_TM = 512
DISPATCH_TM = 1024
B_TQ = 512
SLOT_TM = 2048
DMA_UNROLL = 8
ATTN_TQ = 256
A_UNITS_PER_STEP = 2
C_HEADS_PER_STEP = 4
MERGE_TM = 1024
VMEM_LIMIT = 56 * 1024 * 1024
ROW_E, ROW_RANK, ROW_GATE = 0, 2, 4


def _params(sem, vmem=VMEM_LIMIT):
    return pltpu.CompilerParams(dimension_semantics=sem, vmem_limit_bytes=vmem)


def _tile(n, t):
    t = min(n, t)
    assert n % t == 0, (n, t)
    return t


def _rope_head(xh, c1_ref, s1_ref):
    return xh * c1_ref[...] + pltpu.roll(xh, 64, 1) * s1_ref[...]


def _head_epilogue(kind, xh, c1_ref, s1_ref, cax_ref, sa_ref, sb_ref, g_ref):
    if kind == K_PLAIN:
        return xh
    if kind in (K_ROPE, K_ROPE_Q):
        y = _rope_head(xh, c1_ref, s1_ref)
    else:
        g = g_ref[0:1, :] if kind == K_AXIAL_Q else g_ref[1:2, :]
        sq = xh * xh
        sq_hi = sq.astype(BF16)
        sq_lo = (sq - sq_hi.astype(F32)).astype(BF16)
        ones = jnp.ones((LANE, LANE), BF16)
        ssum = jnp.dot(sq_hi, ones, preferred_element_type=F32) + jnp.dot(sq_lo, ones, preferred_element_type=F32)
        r = xh * lax.rsqrt(ssum * (1.0 / LANE) + RMS_EPS) * g
        y = r * cax_ref[...] + pltpu.roll(r, 96, 1) * sa_ref[...] + pltpu.roll(r, 32, 1) * sb_ref[...]
    return y * ATTN_SCALE if kind in (K_ROPE_Q, K_AXIAL_Q) else y


def _project_kernel(x_ref, w_ref, c1_ref, s1_ref, cax_ref, sa_ref, sb_ref, g_ref, o_ref, *scratch,
                    tile_kinds, dilation):
    j = pl.program_id(1)
    patterns = {}
    for idx, pattern in enumerate(tile_kinds):
        patterns.setdefault(pattern, []).append(idx)

    def emit(pattern):
        axial = any(kind in (K_AXIAL_Q, K_AXIAL_K) for kind in pattern)
        chunk = len(pattern) if axial else PROJ_CHUNK_HEADS
        for c0 in range(0, len(pattern), chunk):
            acc = jnp.dot(x_ref[...], w_ref[:, c0 * LANE:(c0 + chunk) * LANE], preferred_element_type=F32)
            for hh in range(chunk):
                h = c0 + hh
                cols = slice(h * LANE, (h + 1) * LANE)
                y = _head_epilogue(pattern[h], acc[:, hh * LANE:(hh + 1) * LANE],
                                   c1_ref, s1_ref, cax_ref, sa_ref, sb_ref, g_ref)
                if dilation == 1:
                    o_ref[0, 0, :, cols] = y.astype(o_ref.dtype)
                else:
                    scratch[0][h] = y

    for pattern, idxs in patterns.items():
        cond = functools.reduce(jnp.logical_or, [j == i for i in idxs])
        pl.when(cond)(functools.partial(emit, pattern))

    if dilation > 1:
        y_ref = scratch[0]
        rows = y_ref.shape[1] // dilation
        for c in range(dilation):
            for h in range(y_ref.shape[0]):
                o_ref[0, c, :, h * LANE:(h + 1) * LANE] = (
                    y_ref[h, pl.ds(c, rows, stride=dilation), :].astype(o_ref.dtype))


def _project(xb, w, tables, gains, seq, tn, tile_kinds, dilation, name):
    t, d = xb.shape
    n = w.shape[1]
    assert n == tn * len(tile_kinds) and all(len(p) * LANE == tn for p in tile_kinds)
    tm = _tile(seq, QKV_TM)
    nsb = seq // tm
    tab_spec = pl.BlockSpec((tm, LANE), lambda i, j: (i % nsb, 0))
    scratch = [] if dilation == 1 else [pltpu.VMEM((tn // LANE, tm, LANE), F32)]
    return pl.pallas_call(
        functools.partial(_project_kernel, tile_kinds=tuple(tile_kinds), dilation=dilation),
        out_shape=jax.ShapeDtypeStruct((t // seq, dilation, seq // dilation, n), BF16),
        grid=(t // tm, n // tn),
        in_specs=[pl.BlockSpec((tm, d), lambda i, j: (i, 0)),
                  pl.BlockSpec((d, tn), lambda i, j: (0, j)),
                  tab_spec, tab_spec, tab_spec, tab_spec, tab_spec,
                  pl.BlockSpec((2, LANE), lambda i, j: (0, 0))],
        out_specs=pl.BlockSpec((1, dilation, tm // dilation, tn), lambda i, j: (i // nsb, 0, i % nsb, j)),
        scratch_shapes=scratch,
        compiler_params=_params(("parallel", "arbitrary")),
        name=name,
    )(xb, w, *tables, gains)


A_TILE_KINDS = ((K_ROPE_Q,) * HPG, (K_ROPE,) * HPG, (K_PLAIN,) * HPG)
B_TILE_KINDS = ((K_AXIAL_Q,) * 4, (K_AXIAL_Q,) * 4, (K_AXIAL_K,) * B_KV_HEADS + (K_PLAIN,) * B_KV_HEADS)
C_TILE_KINDS = ((K_ROPE_Q,) * 8, (K_ROPE,) * 8, (K_PLAIN,) * 8)


def _gate_kernel(x_ref, w_ref, b_ref, o_ref):
    acc = jnp.dot(x_ref[...], w_ref[...], preferred_element_type=F32) + b_ref[...]
    o_ref[...] = jax.nn.sigmoid(acc).astype(o_ref.dtype)


def _gate_proj(xb, w, b):
    t, d = xb.shape
    n = w.shape[1]
    tm, tn = _tile(t, GATE_TM), _tile(n, GATE_TN)
    return pl.pallas_call(
        _gate_kernel,
        out_shape=jax.ShapeDtypeStruct((t, n), BF16),
        grid=(t // tm, n // tn),
        in_specs=[pl.BlockSpec((tm, d), lambda i, j: (i, 0)),
                  pl.BlockSpec((d, tn), lambda i, j: (0, j)),
                  pl.BlockSpec((1, tn), lambda i, j: (0, j))],
        out_specs=pl.BlockSpec((tm, tn), lambda i, j: (i, j)),
        compiler_params=_params(("parallel", "arbitrary")),
        name="gate_proj",
    )(xb, w, b)


def _attn_a_kernel(q_ref, k_ref, v_ref, o_ref, lse_ref, *, tq, win, sub_len):
    n_res, n_qb = q_ref.shape[1], q_ref.shape[2] // tq
    lane = lax.broadcasted_iota(I32, (tq, LANE), 1)
    for qb in range(n_qb):
        q0 = (pl.program_id(2) * n_qb + qb) * tq
        start = jnp.clip(q0 - A_RADIUS, 0, sub_len - win)
        if win % 16 == 0 and tq % 64 == 0:
            start = pl.multiple_of(start, 16)
        q_pos = q0 + lax.broadcasted_iota(I32, (tq, win), 0)
        k_pos = start + lax.broadcasted_iota(I32, (tq, win), 1)
        bias = jnp.where(jnp.abs(k_pos - q_pos) <= A_RADIUS, 0.0, NEG_INF).astype(F32)
        rows = slice(qb * tq, (qb + 1) * tq)
        for c in range(n_res):
            lse_all = jnp.zeros((tq, LANE), F32)
            for h in range(HPG):
                cols = slice(h * LANE, (h + 1) * LANE)
                q = q_ref[0, c, rows, cols]
                k = k_ref[0, c, pl.ds(start, win), cols]
                v = v_ref[0, c, pl.ds(start, win), cols]
                s = lax.dot_general(q, k, (((1,), (1,)), ((), ())), preferred_element_type=F32) + bias
                m = jnp.max(s, axis=-1, keepdims=True)
                p = jnp.exp2(s - m).astype(BF16)
                ov = jnp.dot(p, jnp.concatenate([v, jnp.ones_like(v)], axis=1), preferred_element_type=F32)
                den = ov[:, LANE:]
                o_ref[0, c, rows, cols] = (ov[:, :LANE] / den).astype(o_ref.dtype)
                lse_all = jnp.where(lane == h, (m + jnp.log2(den)) * LN2, lse_all)
            lse_ref[0, c, rows, :] = lse_all


def _attn_a(qkv4):
    b, r, sub_len, _ = qkv4.shape
    tq = _tile(sub_len, ATTN_TQ)
    win = min(tq + 2 * A_RADIUS, sub_len)
    units = A_UNITS_PER_STEP * (ATTN_TQ // tq)
    n_qb = min(units, sub_len // tq)
    n_res = min(max(units // n_qb, 1), r)
    assert (sub_len // tq) % n_qb == 0 and r % n_res == 0
    return pl.pallas_call(
        functools.partial(_attn_a_kernel, tq=tq, win=win, sub_len=sub_len),
        out_shape=(jax.ShapeDtypeStruct((b, r, sub_len, A_GROUP_COLS), BF16),
                   jax.ShapeDtypeStruct((b, r, sub_len, LANE), F32)),
        grid=(b, r // n_res, sub_len // (tq * n_qb)),
        in_specs=[pl.BlockSpec((1, n_res, tq * n_qb, A_GROUP_COLS), lambda bi, c, qi: (bi, c, qi, 0)),
                  pl.BlockSpec((1, n_res, sub_len, A_GROUP_COLS), lambda bi, c, qi: (bi, c, 0, 1)),
                  pl.BlockSpec((1, n_res, sub_len, A_GROUP_COLS), lambda bi, c, qi: (bi, c, 0, 2))],
        out_specs=(pl.BlockSpec((1, n_res, tq * n_qb, A_GROUP_COLS), lambda bi, c, qi: (bi, c, qi, 0)),
                   pl.BlockSpec((1, n_res, tq * n_qb, LANE), lambda bi, c, qi: (bi, c, qi, 0))),
        compiler_params=_params(("parallel", "parallel", "arbitrary")),
        name=f"attn_a_r{r}",
    )(qkv4, qkv4, qkv4)


def _merge_a_kernel(o0_ref, l0_ref, o1_ref, l1_ref, o2_ref, l2_ref, o_ref, so1, sl1, so2, sl2):
    def to_token_order(src_ref, dst_ref):
        r, rows = src_ref.shape[1], src_ref.shape[2]
        for c in range(r):
            for h in range(dst_ref.shape[0]):
                dst_ref[h, pl.ds(c, rows, stride=r), :] = src_ref[0, c, :, h * LANE:(h + 1) * LANE].astype(F32)

    to_token_order(o1_ref, so1)
    to_token_order(l1_ref, sl1)
    to_token_order(o2_ref, so2)
    to_token_order(l2_ref, sl2)
    l0, l1, l2 = l0_ref[...], sl1[0], sl2[0]
    m = jnp.maximum(jnp.maximum(l0, l1), l2)
    e0, e1, e2 = jnp.exp(l0 - m), jnp.exp(l1 - m), jnp.exp(l2 - m)
    den = e0 + e1 + e2
    w0, w1, w2 = e0 / den, e1 / den, e2 / den
    for h in range(HPG):
        cols = slice(h * LANE, (h + 1) * LANE)
        acc = (o0_ref[:, cols].astype(F32) * w0[:, h:h + 1]
               + so1[h] * w1[:, h:h + 1]
               + so2[h] * w2[:, h:h + 1])
        o_ref[:, cols] = acc.astype(o_ref.dtype)


def _merge_a(o0, l0, o1, l1, o2, l2, seq):
    t = o0.shape[0]
    tm = _tile(seq, MERGE_TM)
    nsb = seq // tm

    def res_spec(arr):
        r, n = arr.shape[1], arr.shape[3]
        return pl.BlockSpec((1, r, tm // r, n), lambda i: (i // nsb, 0, i % nsb, 0))

    return pl.pallas_call(
        _merge_a_kernel,
        out_shape=jax.ShapeDtypeStruct((t, A_OUT), BF16),
        grid=(t // tm,),
        in_specs=[pl.BlockSpec((tm, A_GROUP_COLS), lambda i: (i, 0)), pl.BlockSpec((tm, LANE), lambda i: (i, 0)),
                  res_spec(o1), res_spec(l1), res_spec(o2), res_spec(l2)],
        out_specs=pl.BlockSpec((tm, A_OUT), lambda i: (i, 0)),
        scratch_shapes=[pltpu.VMEM((HPG, tm, LANE), F32), pltpu.VMEM((1, tm, LANE), F32),
                        pltpu.VMEM((HPG, tm, LANE), F32), pltpu.VMEM((1, tm, LANE), F32)],
        compiler_params=_params(("parallel",)),
        name="merge_a",
    )(o0, l0, o1, l1, o2, l2)


def _softmax_pv(q, k, v):
    s = lax.dot_general(q, k, (((1,), (1,)), ((), ())), preferred_element_type=F32)
    m = jnp.max(s, axis=-1, keepdims=True)
    p = jnp.exp2(s - m)
    den = jnp.sum(p, axis=-1, keepdims=True)
    return jnp.dot(p.astype(BF16), v, preferred_element_type=F32) / den


def _attn_b_kernel(q_ref, k_ref, v_ref, o_ref):
    for h in range(B_KV_HEADS):
        k = k_ref[0, :, h * LANE:(h + 1) * LANE]
        v = v_ref[0, :, h * LANE:(h + 1) * LANE]
        v1 = jnp.concatenate([v, jnp.ones_like(v)], axis=1)
        for g in range(B_GROUP):
            cols = slice((h * B_GROUP + g) * LANE, (h * B_GROUP + g + 1) * LANE)
            s = lax.dot_general(q_ref[0, :, cols], k, (((1,), (1,)), ((), ())), preferred_element_type=F32)
            p = jnp.exp2(s - jnp.max(s, axis=-1, keepdims=True)).astype(BF16)
            ov = jnp.dot(p, v1, preferred_element_type=F32)
            o_ref[0, :, cols] = (ov[:, :LANE] / ov[:, LANE:]).astype(o_ref.dtype)


def _attn_b(qkv3):
    b, s, _ = qkv3.shape
    tq = _tile(s, B_TQ)
    kv_cols = B_KV_HEADS * LANE
    o = pl.pallas_call(
        _attn_b_kernel,
        out_shape=jax.ShapeDtypeStruct((b, s, B_OUT), BF16),
        grid=(b, s // tq),
        in_specs=[pl.BlockSpec((1, tq, B_OUT), lambda bi, qi: (bi, qi, 0)),
                  pl.BlockSpec((1, s, kv_cols), lambda bi, qi: (bi, 0, B_K0 * LANE // kv_cols)),
                  pl.BlockSpec((1, s, kv_cols), lambda bi, qi: (bi, 0, B_V0 * LANE // kv_cols))],
        out_specs=pl.BlockSpec((1, tq, B_OUT), lambda bi, qi: (bi, qi, 0)),
        compiler_params=_params(("parallel", "arbitrary")),
        name="attn_b",
    )(qkv3, qkv3, qkv3)
    return o.reshape(b * s, B_OUT)


def _attn_c_kernel(q_ref, k_ref, v_ref, lam_ref, g_ref, o_ref, *, lambda_init):
    lam_p = lam_ref[...].astype(F32)
    lam = (jnp.exp(jnp.sum(lam_p[0:1] * lam_p[1:2], axis=-1, keepdims=True))
           - jnp.exp(jnp.sum(lam_p[2:3] * lam_p[3:4], axis=-1, keepdims=True)) + lambda_init)
    for hh in range(C_HEADS_PER_STEP):
        c0 = hh * 2 * LANE
        first, second = slice(c0, c0 + LANE), slice(c0 + LANE, c0 + 2 * LANE)
        v = v_ref[0, :, c0:c0 + 2 * LANE]
        o = (_softmax_pv(q_ref[0, :, first], k_ref[0, :, first], v)
             - lam * _softmax_pv(q_ref[0, :, second], k_ref[0, :, second], v))
        o = o * lax.rsqrt(jnp.mean(o * o, axis=-1, keepdims=True) + RMS_EPS) * g_ref[...].astype(F32)
        o_ref[0, :, c0:c0 + 2 * LANE] = (o * (1.0 - lambda_init)).astype(o_ref.dtype)


def _attn_c(qkv3, lam_rows, subln_g, lambda_init):
    b, s, _ = qkv3.shape
    tq = _tile(s, ATTN_TQ)
    cols = C_HEADS_PER_STEP * 2 * LANE
    steps = C_HEADS // C_HEADS_PER_STEP
    o = pl.pallas_call(
        functools.partial(_attn_c_kernel, lambda_init=lambda_init),
        out_shape=jax.ShapeDtypeStruct((b, s, C_OUT), BF16),
        grid=(b, steps, s // tq),
        in_specs=[pl.BlockSpec((1, tq, cols), lambda bi, h, qi: (bi, qi, h)),
                  pl.BlockSpec((1, s, cols), lambda bi, h, qi: (bi, 0, steps + h)),
                  pl.BlockSpec((1, s, cols), lambda bi, h, qi: (bi, 0, 2 * steps + h)),
                  pl.BlockSpec((4, LANE), lambda bi, h, qi: (0, 0)),
                  pl.BlockSpec((1, 2 * LANE), lambda bi, h, qi: (0, 0))],
        out_specs=pl.BlockSpec((1, tq, cols), lambda bi, h, qi: (bi, qi, h)),
        compiler_params=_params(("parallel", "parallel", "arbitrary")),
        name="attn_c",
    )(qkv3, qkv3, qkv3, lam_rows, subln_g)
    return o.reshape(b * s, C_OUT)


def _branch_proj_kernel(oa_ref, ob_ref, oc_ref, g0_ref, g1_ref, g2_ref, wa_ref, wb_ref, wc_ref, o_ref):
    acc = g0_ref[...].astype(F32) * jnp.dot(oa_ref[...], wa_ref[...], preferred_element_type=F32)
    acc = acc + g1_ref[...].astype(F32) * jnp.dot(ob_ref[...], wb_ref[...], preferred_element_type=F32)
    acc = acc + g2_ref[...].astype(F32) * jnp.dot(oc_ref[...], wc_ref[...], preferred_element_type=F32)
    o_ref[...] = acc.astype(o_ref.dtype)


def _branch_proj(oa, ob, oc, gates, wa, wb, wc):
    t = oa.shape[0]
    d = wa.shape[1]
    tm, tn = _tile(t, PROJ_TM), _tile(d, PROJ_TN)
    nb = d // tn
    gspecs = [pl.BlockSpec((tm, tn), functools.partial(lambda i, j, br: (i, br * nb + j), br=br)) for br in range(N_BRANCH)]
    return pl.pallas_call(
        _branch_proj_kernel,
        out_shape=jax.ShapeDtypeStruct((t, d), BF16),
        grid=(t // tm, nb),
        in_specs=[pl.BlockSpec((tm, A_OUT), lambda i, j: (i, 0)),
                  pl.BlockSpec((tm, B_OUT), lambda i, j: (i, 0)),
                  pl.BlockSpec((tm, C_OUT), lambda i, j: (i, 0))] + gspecs +
                 [pl.BlockSpec((A_OUT, tn), lambda i, j: (0, j)),
                  pl.BlockSpec((B_OUT, tn), lambda i, j: (0, j)),
                  pl.BlockSpec((C_OUT, tn), lambda i, j: (0, j))],
        out_specs=pl.BlockSpec((tm, tn), lambda i, j: (i, j)),
        compiler_params=_params(("parallel", "arbitrary")),
        name="branch_proj",
    )(oa, ob, oc, gates, gates, gates, wa, wb, wc)


def _layer_norm(z, g, b):
    mu = jnp.mean(z, axis=-1, keepdims=True)
    zc = z - mu
    var = jnp.mean(zc * zc, axis=-1, keepdims=True)
    return zc * lax.rsqrt(var + LN_EPS) * g + b


def _out_ln_kernel(m_ref, w_ref, x_ref, g_ref, b_ref, o_ref):
    part = m_ref.shape[0] // OUT_SPLIT
    for r in range(OUT_SPLIT):
        rows = slice(r * part, (r + 1) * part)
        mix = jnp.dot(m_ref[rows, :], w_ref[...], preferred_element_type=F32)
        o_ref[rows, :] = _layer_norm(DEEPNORM_ALPHA * x_ref[rows, :] + mix, g_ref[...], b_ref[...])


def _out_proj_ln(merged, w_out, x, g, b):
    t, d = x.shape
    tm = _tile(t, OUT_TM)
    row = pl.BlockSpec((tm, d), lambda i: (i, 0))
    vec = pl.BlockSpec((1, d), lambda i: (0, 0))
    return pl.pallas_call(
        _out_ln_kernel,
        out_shape=jax.ShapeDtypeStruct((t, d), F32),
        grid=(t // tm,),
        in_specs=[row, pl.BlockSpec((d, d), lambda i: (0, 0)), row, vec, vec],
        out_specs=row,
        compiler_params=_params(("parallel",)),
        name="out_proj_ln",
    )(merged, w_out, x, g, b)


def _router_kernel(x_ref, whi_ref, wlo_ref, bias_ref, er_ref, gt_ref, cnt_ref, carry_ref, *, tm):
    @pl.when(pl.program_id(0) == 0)
    def _():
        carry_ref[...] = jnp.zeros_like(carry_ref)

    x = x_ref[...]
    x_hi = x.astype(BF16)
    x_lo = (x - x_hi.astype(F32)).astype(BF16)
    nt = (((1,), (1,)), ((), ()))
    logits = (lax.dot_general(whi_ref[...], x_hi, nt, preferred_element_type=F32)
              + lax.dot_general(wlo_ref[...], x_hi, nt, preferred_element_type=F32)
              + lax.dot_general(whi_ref[...], x_lo, nt, preferred_element_type=F32))
    scores = jax.nn.sigmoid(logits)
    biased = scores + bias_ref[...]

    def row(a, e):
        return a[e:e + 1, :]

    gscore = []
    for g in range(N_EXPERT_GROUPS):
        a, b, c, d = (row(biased, g * EXPERTS_PER_GROUP + i) for i in range(EXPERTS_PER_GROUP))
        hi1, lo1, hi2, lo2 = jnp.maximum(a, b), jnp.minimum(a, b), jnp.maximum(c, d), jnp.minimum(c, d)
        gscore.append(jnp.maximum(hi1, hi2) + jnp.maximum(jnp.minimum(hi1, hi2), jnp.maximum(lo1, lo2)))
    gsel = jnp.zeros((1, tm), I32)
    best = gscore[0]
    for g in range(1, N_EXPERT_GROUPS):
        better = gscore[g] > best
        gsel = jnp.where(better, g, gsel)
        best = jnp.where(better, gscore[g], best)

    def pick(a, i):
        out = row(a, i)
        for g in range(1, N_EXPERT_GROUPS):
            out = jnp.where(gsel == g, row(a, g * EXPERTS_PER_GROUP + i), out)
        return out

    bv = [pick(biased, i) for i in range(EXPERTS_PER_GROUP)]
    sv = [pick(scores, i) for i in range(EXPERTS_PER_GROUP)]
    i0 = jnp.zeros((1, tm), I32)
    b0 = bv[0]
    for i in range(1, EXPERTS_PER_GROUP):
        better = bv[i] > b0
        i0 = jnp.where(better, i, i0)
        b0 = jnp.where(better, bv[i], b0)
    i1 = jnp.full((1, tm), -1, I32)
    b1 = jnp.full((1, tm), -jnp.inf, F32)
    for i in range(EXPERTS_PER_GROUP):
        better = (i0 != i) & ((i1 < 0) | (bv[i] > b1))
        i1 = jnp.where(better, i, i1)
        b1 = jnp.where(better, bv[i], b1)
    s0 = jnp.zeros((1, tm), F32)
    s1 = jnp.zeros((1, tm), F32)
    for i in range(EXPERTS_PER_GROUP):
        s0 = jnp.where(i0 == i, sv[i], s0)
        s1 = jnp.where(i1 == i, sv[i], s1)
    e0 = gsel * EXPERTS_PER_GROUP + i0
    e1 = gsel * EXPERTS_PER_GROUP + i1
    den = s0 + s1
    g0, g1 = s0 / den, s1 / den

    erow = lax.broadcasted_iota(I32, (N_EXPERTS, tm), 0)
    member = ((erow == e0) | (erow == e1))
    tri = (lax.broadcasted_iota(I32, (tm, tm), 0) < lax.broadcasted_iota(I32, (tm, tm), 1))
    prefix = jnp.dot(member.astype(BF16), tri.astype(BF16), preferred_element_type=F32) + carry_ref[:, 0:1]
    r0 = jnp.sum(jnp.where(erow == e0, prefix, 0.0), axis=0, keepdims=True)
    r1 = jnp.sum(jnp.where(erow == e1, prefix, 0.0), axis=0, keepdims=True)
    carry_ref[...] = carry_ref[...] + jnp.sum(member.astype(F32), axis=1, keepdims=True)
    cnt_ref[...] = carry_ref[...]

    zi = jnp.zeros((1, tm), I32)
    er_ref[...] = jnp.concatenate([e0, e1, r0.astype(I32), r1.astype(I32), zi, zi, zi, zi], axis=0)
    zf = jnp.zeros((1, tm), F32)
    rec = jnp.concatenate([zf, zf, zf, zf, g0, g1, zf, zf, jnp.zeros((LANE - 8, tm), F32)], axis=0)
    gt_ref[...] = rec.T


def _router(x, w_hi_t, w_lo_t, bias_col):
    t, d = x.shape
    tm = _tile(t, ROUTER_TM)
    return pl.pallas_call(
        functools.partial(_router_kernel, tm=tm),
        out_shape=(jax.ShapeDtypeStruct((8, t), I32), jax.ShapeDtypeStruct((t, LANE), F32),
                   jax.ShapeDtypeStruct((N_EXPERTS, LANE), F32)),
        grid=(t // tm,),
        in_specs=[pl.BlockSpec((tm, d), lambda i: (i, 0)),
                  pl.BlockSpec((N_EXPERTS, d), lambda i: (0, 0)),
                  pl.BlockSpec((N_EXPERTS, d), lambda i: (0, 0)),
                  pl.BlockSpec((N_EXPERTS, 1), lambda i: (0, 0))],
        out_specs=(pl.BlockSpec((8, tm), lambda i: (0, i)),
                   pl.BlockSpec((tm, LANE), lambda i: (i, 0)),
                   pl.BlockSpec((N_EXPERTS, LANE), lambda i: (0, 0))),
        scratch_shapes=[pltpu.VMEM((N_EXPERTS, LANE), F32)],
        compiler_params=_params(("arbitrary",)),
        name="router",
    )(x, w_hi_t, w_lo_t, bias_col)


def _slot_rows_kernel(pad_start_ref, er_ref, o_ref):
    e = er_ref[ROW_E:ROW_E + TOP_K, :]
    base = jnp.zeros_like(e)
    for x in range(N_EXPERTS):
        base = jnp.where(e == x, pad_start_ref[x], base)
    rows = base + er_ref[ROW_RANK:ROW_RANK + TOP_K, :]
    o_ref[...] = jnp.concatenate([rows, jnp.zeros((8 - TOP_K, rows.shape[1]), I32)], axis=0)


def _slot_rows(pad_start, er):
    t = er.shape[1]
    tm = _tile(t, SLOT_TM)
    grid_spec = pltpu.PrefetchScalarGridSpec(
        num_scalar_prefetch=1,
        grid=(t // tm,),
        in_specs=[pl.BlockSpec((8, tm), lambda i, ps: (0, i))],
        out_specs=pl.BlockSpec((8, tm), lambda i, ps: (0, i)),
    )
    return pl.pallas_call(
        _slot_rows_kernel,
        out_shape=jax.ShapeDtypeStruct((8, t), I32),
        grid_spec=grid_spec,
        compiler_params=_params(("arbitrary",)),
        name="moe_slot_rows",
    )(pad_start, er)


def _dispatch_kernel(fill_end_ref, pad_end_ref, rows_ref, x_ref, xs_hbm, zblk, sem, zsem, *, tm):
    def row_copy(r, dst_row):
        return pltpu.make_async_copy(x_ref.at[pl.ds(r, 1), :], xs_hbm.at[pl.ds(dst_row, 1), :], sem)

    def issue(r, c):
        row_copy(r, rows_ref[0, r]).start(priority=0)
        row_copy(r, rows_ref[1, r]).start(priority=1)
        return c

    lax.fori_loop(0, tm, issue, 0, unroll=DMA_UNROLL)

    def zero_row(dst_row):
        return pltpu.make_async_copy(zblk.at[pl.ds(0, 1), :], xs_hbm.at[pl.ds(dst_row, 1), :], zsem)

    def zero_block(blk):
        dst = xs_hbm.at[pl.ds(pl.multiple_of(blk * MOE_BLOCK, MOE_BLOCK), MOE_BLOCK), :]
        return pltpu.make_async_copy(zblk, dst, zsem)

    @pl.when(pl.program_id(0) == 0)
    def _():
        zblk[...] = jnp.zeros_like(zblk)
        for e in range(N_EXPERTS):
            lo, hi = fill_end_ref[e], pad_end_ref[e]
            lax.fori_loop(lo, hi, lambda r, c: (zero_row(r).start(), c)[1], 0)
            lax.fori_loop(lo, hi, lambda r, c: (zero_row(0).wait(), c)[1], 0)
        first, last = pad_end_ref[N_EXPERTS - 1] // MOE_BLOCK, xs_hbm.shape[0] // MOE_BLOCK
        lax.fori_loop(first, last, lambda blk, c: (zero_block(blk).start(), c)[1], 0)
        lax.fori_loop(first, last, lambda blk, c: (zero_block(0).wait(), c)[1], 0)

    def drain(r, c):
        row_copy(0, 0).wait()
        row_copy(0, 0).wait()
        return c

    lax.fori_loop(0, tm, drain, 0, unroll=DMA_UNROLL)


def _dispatch(x, rows, fill_end, pad_end, n_rows):
    t, d = x.shape
    tm = _tile(t, DISPATCH_TM)
    grid_spec = pltpu.PrefetchScalarGridSpec(
        num_scalar_prefetch=2,
        grid=(t // tm,),
        in_specs=[pl.BlockSpec((8, tm), lambda i, fe, pe: (0, i), memory_space=pltpu.SMEM),
                  pl.BlockSpec((tm, d), lambda i, fe, pe: (i, 0))],
        out_specs=pl.BlockSpec(memory_space=pl.ANY),
        scratch_shapes=[pltpu.VMEM((MOE_BLOCK, d), F32), pltpu.SemaphoreType.DMA, pltpu.SemaphoreType.DMA],
    )
    return pl.pallas_call(
        functools.partial(_dispatch_kernel, tm=tm),
        out_shape=jax.ShapeDtypeStruct((n_rows, d), F32),
        grid_spec=grid_spec,
        compiler_params=_params(("arbitrary",)),
        name="moe_dispatch",
    )(fill_end, pad_end, rows, x)


def _expert_kernel(blk_e_ref, nact_ref, x_ref, w1_ref, w3_ref, w2_ref, o_ref):
    del blk_e_ref

    @pl.when(pl.program_id(0) < nact_ref[0])
    def _():
        x = x_ref[...].astype(BF16)
        h1 = jnp.dot(x, w1_ref[0], preferred_element_type=F32)
        h3 = jnp.dot(x, w3_ref[0], preferred_element_type=F32)
        h = (h1 * jax.nn.sigmoid(h1) * h3).astype(BF16)
        o_ref[...] = jnp.dot(h, w2_ref[0], preferred_element_type=F32)

    @pl.when(pl.program_id(0) >= nact_ref[0])
    def _():
        o_ref[...] = jnp.zeros_like(o_ref)


def _experts(xs, blk_e, n_active, w1, w3, w2):
    p, d = xs.shape
    f = w1.shape[2]
    nb = p // MOE_BLOCK
    grid_spec = pltpu.PrefetchScalarGridSpec(
        num_scalar_prefetch=2,
        grid=(nb,),
        in_specs=[pl.BlockSpec((MOE_BLOCK, d), lambda i, be, na: (jnp.minimum(i, na[0] - 1), 0)),
                  pl.BlockSpec((1, d, f), lambda i, be, na: (be[i], 0, 0)),
                  pl.BlockSpec((1, d, f), lambda i, be, na: (be[i], 0, 0)),
                  pl.BlockSpec((1, f, d), lambda i, be, na: (be[i], 0, 0))],
        out_specs=pl.BlockSpec((MOE_BLOCK, d), lambda i, be, na: (i, 0)),
    )
    return pl.pallas_call(
        _expert_kernel,
        out_shape=jax.ShapeDtypeStruct((p, d), F32),
        grid_spec=grid_spec,
        compiler_params=_params(("arbitrary",)),
        name="moe_experts",
    )(blk_e, n_active, xs, w1, w3, w2)


def _combine_kernel(rows_ref, next_rows_ref, x_ref, gt_ref, g_ref, b_ref, ys_hbm, o_ref, ob_ref, buf, sems, *, tm):
    i = pl.program_id(0)
    n = pl.num_programs(0)

    def row_copy(par, r, slot, src_row):
        return pltpu.make_async_copy(ys_hbm.at[pl.ds(src_row, 1), :], buf.at[par, slot, pl.ds(r, 1), :], sems.at[par])

    def gather(par, idx_ref):
        def issue(r, c):
            row_copy(par, r, 0, idx_ref[0, r]).start(priority=0)
            row_copy(par, r, 1, idx_ref[1, r]).start(priority=1)
            return c
        lax.fori_loop(0, tm, issue, 0, unroll=DMA_UNROLL)

    @pl.when(i == 0)
    def _():
        gather(0, rows_ref)

    @pl.when(i + 1 < n)
    def _():
        gather((i + 1) % 2, next_rows_ref)

    par = i % 2

    def drain(r, c):
        row_copy(par, 0, 0, 0).wait()
        row_copy(par, 0, 0, 0).wait()
        return c

    lax.fori_loop(0, tm, drain, 0, unroll=DMA_UNROLL)
    gt = gt_ref[...]
    ffn = gt[:, ROW_GATE:ROW_GATE + 1] * buf[par, 0] + gt[:, ROW_GATE + 1:ROW_GATE + 2] * buf[par, 1]
    y = _layer_norm(DEEPNORM_ALPHA * x_ref[...] + ffn, g_ref[...], b_ref[...])
    o_ref[...] = y
    ob_ref[...] = y.astype(ob_ref.dtype)


def _combine_ln(x, ys, rows, gt, g, b):
    t, d = x.shape
    tm = _tile(t, COMBINE_TM)
    n = t // tm
    row = pl.BlockSpec((tm, d), lambda i: (i, 0))
    vec = pl.BlockSpec((1, d), lambda i: (0, 0))
    return pl.pallas_call(
        functools.partial(_combine_kernel, tm=tm),
        out_shape=(jax.ShapeDtypeStruct((t, d), F32), jax.ShapeDtypeStruct((t, d), BF16)),
        grid=(n,),
        in_specs=[pl.BlockSpec((8, tm), lambda i: (0, i), memory_space=pltpu.SMEM),
                  pl.BlockSpec((8, tm), lambda i: (0, jnp.minimum(i + 1, n - 1)), memory_space=pltpu.SMEM),
                  row,
                  pl.BlockSpec((tm, LANE), lambda i: (i, 0)),
                  vec, vec,
                  pl.BlockSpec(memory_space=pl.ANY)],
        out_specs=(row, row),
        scratch_shapes=[pltpu.VMEM((2, TOP_K, tm, d), F32), pltpu.SemaphoreType.DMA((2,))],
        compiler_params=_params(("arbitrary",)),
        name="moe_combine_ln",
    )(rows, rows, x, gt, g, b, ys)


def _rope_tables(seq):
    def tab(pos, dim):
        inv = ROPE_THETA ** (-jnp.arange(0, dim, 2, dtype=F32) / dim)
        ang = pos.astype(F32)[:, None] * inv[None, :]
        return jnp.cos(ang), jnp.sin(ang)

    pos = jnp.arange(seq)
    cos1, sin1 = tab(pos, HEAD_DIM)
    cos_r, sin_r = tab(pos // GRID_W, HEAD_DIM // 2)
    cos_c, sin_c = tab(pos % GRID_W, HEAD_DIM // 2)
    z = jnp.zeros_like(sin_r)
    c1 = jnp.concatenate([cos1, cos1], axis=1)
    s1 = jnp.concatenate([-sin1, sin1], axis=1)
    cax = jnp.concatenate([cos_r, cos_r, cos_c, cos_c], axis=1)
    sa = jnp.concatenate([-sin_r, z, -sin_c, z], axis=1)
    sb = jnp.concatenate([z, sin_r, z, sin_c], axis=1)
    return c1, s1, cax, sa, sb


def _moe(x1, lw, shared):
    t, d = x1.shape
    er, gt, cnt = _router(x1, shared["wr_hi"], shared["wr_lo"], shared["r_bias"])
    counts = cnt[:, 0].astype(I32)
    padded = (counts + MOE_BLOCK - 1) // MOE_BLOCK * MOE_BLOCK
    pad_end = jnp.cumsum(padded)
    pad_start = (pad_end - padded).astype(I32)
    n_rows = t * TOP_K + N_EXPERTS * MOE_BLOCK
    nb = n_rows // MOE_BLOCK
    blk_row = jnp.arange(nb, dtype=I32)[:, None] * MOE_BLOCK
    blk_e = jnp.minimum(jnp.sum((pad_end[None, :] <= blk_row).astype(I32), axis=1), N_EXPERTS - 1)
    blk_e = blk_e + lw["layer"] * N_EXPERTS
    n_active = (pad_end[-1:] // MOE_BLOCK).astype(I32)
    rows = _slot_rows(pad_start, er)
    xs = _dispatch(x1, rows, (pad_start + counts).astype(I32), pad_end.astype(I32), n_rows)
    ys = _experts(xs, blk_e, n_active, shared["w1"], shared["w3"], shared["w2"])
    return _combine_ln(x1, ys, rows, gt, lw["ln2_g"], lw["ln2_b"])


def _trunk(x, layers, shared):
    b, s, d = x.shape
    t = b * s
    tables = _rope_tables(s)
    xf = x.reshape(t, d)
    xb = xf.astype(BF16)
    for l, lw in enumerate(layers):
        gains = lw["qk_gains"]
        gates = _gate_proj(xb, lw["w_gate"], lw["b_gate"])
        a_parts = [_attn_a(_project(xb, lw["w_a"][g], tables, gains, s, A_TN, A_TILE_KINDS, dil, f"proj_a{g}"))
                   for g, (_, dil) in enumerate(A_CONFIGS)]
        (o0, l0), (o1, l1), (o2, l2) = a_parts
        oa = _merge_a(o0.reshape(t, A_GROUP_COLS), l0.reshape(t, LANE), o1, l1, o2, l2, s)
        qkv_b = _project(xb, lw["w_b"], tables, gains, s, B_TN, B_TILE_KINDS, 1, "proj_b")
        ob = _attn_b(qkv_b.reshape(b, s, B_COLS))
        lambda_init = 0.8 - 0.6 * float(np.exp(-0.3 * l))
        qkv_c = _project(xb, lw["w_c"], tables, gains, s, C_TN, C_TILE_KINDS, 1, "proj_c")
        oc = _attn_c(qkv_c.reshape(b, s, C_COLS), lw["lam_rows"], lw["subln_g"], lambda_init)
        merged = _branch_proj(oa, ob, oc, gates, lw["w_proj_a"], lw["w_proj_b"], lw["w_proj_c"])
        x1 = _out_proj_ln(merged, lw["w_out"], xf, lw["ln1_g"], lw["ln1_b"])
        xf, xb = _moe(x1, lw, shared)
    return xf.reshape(b, s, d)


def kernel(x_prompt, x_sample, w_in, q_norm_g, k_norm_g, lambda_q1, lambda_k1, lambda_q2, lambda_k2, subln_g,
           w_gate, b_gate, w_proj_a, w_proj_b, w_proj_c, w_out, ln1_g, ln1_b, w_router, router_bias,
           w1, w3, w2, ln2_g, ln2_b):
    assert [dil for _, dil in A_CONFIGS] == [1, 4, 16] and all(w // (2 * dil) == A_RADIUS for w, dil in A_CONFIGS)
    depth = w_in.shape[0]
    layers = []
    part = A_HEADS * HEAD_DIM
    for l in range(depth):
        w_l = w_in[l].astype(BF16)
        w_a = [jnp.concatenate([w_l[:, p * part + g * A_GROUP_COLS: p * part + (g + 1) * A_GROUP_COLS] for p in range(3)],
                               axis=1) for g in range(len(A_CONFIGS))]
        layers.append(dict(
            w_a=w_a, w_b=w_l[:, A_COLS:A_COLS + B_COLS], w_c=w_l[:, A_COLS + B_COLS:],
            qk_gains=jnp.stack([q_norm_g[l], k_norm_g[l]]).astype(F32),
            lam_rows=jnp.stack([lambda_q1[l], lambda_k1[l], lambda_q2[l], lambda_k2[l]]).astype(F32),
            subln_g=subln_g[l].reshape(1, -1).astype(F32),
            w_gate=w_gate[l].astype(BF16),
            b_gate=b_gate[l].reshape(1, -1).astype(F32),
            w_proj_a=w_proj_a[l].astype(BF16),
            w_proj_b=w_proj_b[l].astype(BF16),
            w_proj_c=w_proj_c[l].astype(BF16),
            w_out=w_out[l].astype(BF16),
            ln1_g=ln1_g[l].reshape(1, -1).astype(F32),
            ln1_b=ln1_b[l].reshape(1, -1).astype(F32),
            layer=l,
            ln2_g=ln2_g[l].reshape(1, -1).astype(F32),
            ln2_b=ln2_b[l].reshape(1, -1).astype(F32),
        ))
    wr_t = w_router.astype(F32).T
    wr_hi = wr_t.astype(BF16)
    wr_lo = (wr_t - wr_hi.astype(F32)).astype(BF16)
    def stacked(w):
        return w.astype(BF16).reshape((depth * N_EXPERTS,) + w.shape[2:])

    shared = dict(wr_hi=wr_hi, wr_lo=wr_lo, r_bias=router_bias.astype(F32).reshape(-1, 1),
                  w1=stacked(w1), w3=stacked(w3), w2=stacked(w2))
    return (_trunk(x_prompt, layers, shared), _trunk(x_sample, layers, shared))
```
